```python
import jax, jax.numpy as jnp
from jax import lax
import numpy as np


D_MODEL = 1024
BATCH = 16
SEQ = 2048
DEPTH = 2
DEC_BATCH = 8
DEC_SEQ = 2048
PAST_LEN = 128

EPS = 1e-6
N_MEM = 256
RET_HEADS = 4
RET_DK = 128
RET_DV = 128
RET_QK = RET_HEADS * RET_DK
RET_W = RET_HEADS * RET_DV
RET_CHUNK = 128
ROPE_BASE = 10000.0
GDN_HEADS = 4
GDN_DK = 128
GDN_DV = 128
GDN_QK = GDN_HEADS * GDN_DK
GDN_W = GDN_HEADS * GDN_DV
GDN_CHUNK = 64
CONV_W = 3
MIX_W = RET_W + GDN_W
IN_W = 3 * RET_QK - RET_QK + RET_W + RET_W + 2 * GDN_QK + 2 * GDN_W + 4 * GDN_HEADS
XA_HEADS = 4
XA_DH = D_MODEL // XA_HEADS
PEER_HEADS = 8
PEER_NKEYS = 128
PEER_EXPERTS = PEER_NKEYS * PEER_NKEYS
PEER_DKEY = 256
PEER_DHALF = PEER_DKEY // 2
PEER_TOPK = 16
PEER_TOKEN_BLOCK = 128

kernel_name = 'hybrid_retention_gdn_peer_encoder'


def rms_norm(x, w):
    xf = x.astype(jnp.float32)
    y = xf * lax.rsqrt(jnp.mean(xf * xf, axis=-1, keepdims=True) + EPS)
    return (y * w.astype(jnp.float32)).astype(x.dtype)


def l2_norm(x):
    return x * lax.rsqrt(jnp.sum(x * x, axis=-1, keepdims=True) + EPS)


def rotary(x):
    s, d = x.shape[1], x.shape[-1]
    inv = ROPE_BASE ** (-jnp.arange(0, d, 2, dtype=jnp.float32) / d)
    ang = jnp.arange(s, dtype=jnp.float32)[:, None] * inv[None, :]
    cos = jnp.cos(ang)[None, :, None, :]
    sin = jnp.sin(ang)[None, :, None, :]
    xf = x.astype(jnp.float32)
    x1, x2 = xf[..., 0::2], xf[..., 1::2]
    return jnp.stack([x1 * cos - x2 * sin, x1 * sin + x2 * cos], axis=-1).reshape(x.shape)


def retention_log_decays():
    h = jnp.arange(RET_HEADS, dtype=jnp.float32)
    fwd = jnp.log1p(-jnp.exp2(-5.0 - h))
    bwd = jnp.log1p(-jnp.exp2(-5.5 - h))
    return fwd, bwd


def retention_dir(q, k, v, log_gamma, strict):
    b, h, s, dk = q.shape
    dv = v.shape[-1]
    c = RET_CHUNK
    n = s // c
    qc = q.astype(jnp.float32).reshape(b, h, n, c, dk)
    kc = k.astype(jnp.float32).reshape(b, h, n, c, dk)
    vc = v.astype(jnp.float32).reshape(b, h, n, c, dv)
    pos = jnp.arange(c, dtype=jnp.float32)
    diff = pos[:, None] - pos[None, :]
    mask = diff > 0 if strict else diff >= 0
    lg = log_gamma[:, None, None]
    dmat = jnp.where(mask[None], jnp.exp(lg * jnp.where(mask, diff, 0.0)[None]), 0.0)
    scores = jnp.einsum('bhncd,bhnjd->bhncj', qc, kc) * dmat[None, :, None]
    o_intra = jnp.einsum('bhncj,bhnje->bhnce', scores, vc)
    k_dec = kc * jnp.exp(log_gamma[:, None] * (c - 1.0 - pos)[None])[None, :, None, :, None]
    chunk_kv = jnp.einsum('bhncd,bhnce->nbhde', k_dec, vc)
    chunk_decay = jnp.exp(log_gamma * c)[None, :, None, None]

    def step(state, kv):
        return state * chunk_decay + kv, state

    _, s_prev = lax.scan(step, jnp.zeros((b, h, dk, dv), jnp.float32), chunk_kv)
    q_dec = qc * jnp.exp(log_gamma[:, None] * (pos + 1.0)[None])[None, :, None, :, None]
    o_inter = jnp.einsum('bhncd,nbhde->bhnce', q_dec, s_prev)
    return (o_intra + o_inter).reshape(b, h, s, dv)


def gated_delta_dir(q, k, v, g, beta):
    b, h, s, dk = q.shape
    dv = v.shape[-1]
    c = GDN_CHUNK
    n = s // c
    f32 = jnp.float32
    qc = q.astype(f32).reshape(b, h, n, c, dk)
    kc = k.astype(f32).reshape(b, h, n, c, dk)
    vc = v.astype(f32).reshape(b, h, n, c, dv)
    gc = jnp.cumsum(g.astype(f32).reshape(b, h, n, c), axis=-1)
    bc = beta.astype(f32).reshape(b, h, n, c)
    idx = jnp.arange(c)
    lower = idx[:, None] >= idx[None, :]
    strict_lower = idx[:, None] > idx[None, :]
    decay = jnp.exp(jnp.where(lower, gc[..., :, None] - gc[..., None, :], -jnp.inf))
    kb = kc * bc[..., None]
    vb = vc * bc[..., None]
    m = jnp.where(strict_lower, jnp.einsum('bhnid,bhnjd->bhnij', kb, kc) * decay, 0.0)
    eye = jnp.broadcast_to(jnp.eye(c, dtype=f32), m.shape)
    t = lax.linalg.triangular_solve(eye + m, eye, left_side=True, lower=True)
    u = jnp.matmul(t, vb)
    w = jnp.matmul(t, kb * jnp.exp(gc)[..., None])
    qk = jnp.where(lower, jnp.einsum('bhnid,bhnjd->bhnij', qc, kc) * decay, 0.0)
    q_dec = qc * jnp.exp(gc)[..., None]
    g_last = gc[..., -1]
    k_tail = kc * jnp.exp(g_last[..., None] - gc)[..., None]
    xs = tuple(jnp.moveaxis(a, 2, 0) for a in (u, w, qk, q_dec, k_tail, g_last))

    def step(state, inp):
        u_n, w_n, qk_n, qd_n, kt_n, gl_n = inp
        v_new = u_n - jnp.matmul(w_n, state)
        o = jnp.matmul(qd_n, state) + jnp.matmul(qk_n, v_new)
        state = state * jnp.exp(gl_n)[..., None, None] + jnp.matmul(jnp.swapaxes(kt_n, -1, -2), v_new)
        return state, o

    _, o = lax.scan(step, jnp.zeros((b, h, dk, dv), f32), xs)
    return jnp.moveaxis(o, 0, 2).reshape(b, h, s, dv)


def token_mixer(h, w_in, ret_norm, gdn_conv, gdn_a_log, gdn_dt_bias, gdn_norm, w_out):
    b, s, _ = h.shape
    proj = h @ w_in
    sizes = [RET_QK, RET_QK, RET_W, RET_W, GDN_QK, GDN_QK, GDN_W, GDN_W, 2 * GDN_HEADS, 2 * GDN_HEADS]
    cuts, acc = [], 0
    for sz in sizes[:-1]:
        acc += sz
        cuts.append(acc)
    rq, rk, rv, rg, gq, gk, gv, gz, gb, ga = jnp.split(proj, cuts, axis=-1)

    flip = lambda a: jnp.flip(a, axis=2)
    q = rotary(rq.reshape(b, s, RET_HEADS, RET_DK)).transpose(0, 2, 1, 3)
    k = (rotary(rk.reshape(b, s, RET_HEADS, RET_DK)) * (RET_DK ** -0.5)).transpose(0, 2, 1, 3)
    v = rv.reshape(b, s, RET_HEADS, RET_DV).astype(jnp.float32).transpose(0, 2, 1, 3)
    lg_f, lg_b = retention_log_decays()
    o_r = retention_dir(q, k, v, lg_f, False) + flip(retention_dir(flip(q), flip(k), flip(v), lg_b, True))
    o_r = o_r.transpose(0, 2, 1, 3)
    mu = jnp.mean(o_r, axis=-1, keepdims=True)
    var = jnp.mean(jnp.square(o_r - mu), axis=-1, keepdims=True)
    o_r = ((o_r - mu) * lax.rsqrt(var + EPS)).reshape(b, s, RET_W)
    o_r = o_r * ret_norm.astype(jnp.float32) * jax.nn.silu(rg.astype(jnp.float32))

    qkv = jnp.concatenate([gq, gk, gv], axis=-1)
    pad = CONV_W // 2
    qkv_p = jnp.pad(qkv, ((0, 0), (pad, pad), (0, 0)))
    conv = sum(qkv_p[:, j:j + s] * gdn_conv[j] for j in range(CONV_W))
    conv = jax.nn.silu(conv.astype(jnp.float32))
    cq, ck, cv = jnp.split(conv, [GDN_QK, 2 * GDN_QK], axis=-1)
    q2 = (l2_norm(cq.reshape(b, s, GDN_HEADS, GDN_DK)) * (GDN_DK ** -0.5)).transpose(0, 2, 1, 3)
    k2 = l2_norm(ck.reshape(b, s, GDN_HEADS, GDN_DK)).transpose(0, 2, 1, 3)
    v2 = cv.reshape(b, s, GDN_HEADS, GDN_DV).transpose(0, 2, 1, 3)
    beta = jax.nn.sigmoid(gb.astype(jnp.float32).reshape(b, s, 2, GDN_HEADS))
    g = -jnp.exp(gdn_a_log.astype(jnp.float32)) * jax.nn.softplus(
        ga.astype(jnp.float32).reshape(b, s, 2, GDN_HEADS) + gdn_dt_bias.astype(jnp.float32))
    g_f, g_b = g[:, :, 0].transpose(0, 2, 1), g[:, :, 1].transpose(0, 2, 1)
    be_f, be_b = beta[:, :, 0].transpose(0, 2, 1), beta[:, :, 1].transpose(0, 2, 1)
    o_g = gated_delta_dir(q2, k2, v2, g_f, be_f) + flip(
        gated_delta_dir(flip(q2), flip(k2), flip(v2), flip(g_b), flip(be_b)))
    o_g = o_g.transpose(0, 2, 1, 3)
    o_g = o_g * lax.rsqrt(jnp.mean(o_g * o_g, axis=-1, keepdims=True) + EPS) * gdn_norm.astype(jnp.float32)
    o_g = o_g * jax.nn.silu(gz.astype(jnp.float32).reshape(b, s, GDN_HEADS, GDN_DV))
    o_g = o_g.reshape(b, s, GDN_W)

    merged = jnp.concatenate([o_r, o_g], axis=-1).astype(h.dtype)
    return merged @ w_out


def memory_cross_attention(h, mem_n, w_xq, w_xkv, w_xo):
    b, s, _ = h.shape
    m = mem_n.shape[1]
    q = (h @ w_xq).reshape(b, s, XA_HEADS, XA_DH)
    kv = (mem_n @ w_xkv).reshape(b, m, 2, XA_HEADS, XA_DH)
    k, v = kv[:, :, 0], kv[:, :, 1]
    scores = jnp.einsum('bshd,bmhd->bhsm', q, k).astype(jnp.float32) * (XA_DH ** -0.5)
    p = jax.nn.softmax(scores, axis=-1).astype(v.dtype)
    o = jnp.einsum('bhsm,bmhd->bshd', p, v).reshape(b, s, D_MODEL)
    return o @ w_xo


def peer_ffn(h, peer_wq, peer_keys, peer_u, peer_v):
    b, s, d = h.shape
    t = b * s
    x = h.reshape(t, d)
    q = (x @ peer_wq).reshape(t, PEER_HEADS, 2, PEER_DHALF)
    sc = jnp.einsum('thcd,hcnd->thcn', q, peer_keys).astype(jnp.float32)
    s1, i1 = lax.top_k(sc[:, :, 0], PEER_TOPK)
    s2, i2 = lax.top_k(sc[:, :, 1], PEER_TOPK)
    cand_s = (s1[..., :, None] + s2[..., None, :]).reshape(t, PEER_HEADS, PEER_TOPK * PEER_TOPK)
    cand_i = (i1[..., :, None] * PEER_NKEYS + i2[..., None, :]).reshape(t, PEER_HEADS, PEER_TOPK * PEER_TOPK)
    top_s, top_pos = lax.top_k(cand_s, PEER_TOPK)
    eidx = jnp.take_along_axis(cand_i, top_pos, axis=-1)
    gate = jax.nn.softmax(top_s, axis=-1)
    nb = t // PEER_TOKEN_BLOCK

    def block(args):
        xb, eb, gb = args
        u = peer_u[eb]
        a = jax.nn.gelu(jnp.einsum('thkd,td->thk', u, xb).astype(jnp.float32))
        wgt = (a * gb).astype(xb.dtype)
        return jnp.einsum('thk,thkd->td', wgt, peer_v[eb])

    out = lax.map(block, (x.reshape(nb, PEER_TOKEN_BLOCK, d),
                          eidx.reshape(nb, PEER_TOKEN_BLOCK, PEER_HEADS, PEER_TOPK),
                          gate.reshape(nb, PEER_TOKEN_BLOCK, PEER_HEADS, PEER_TOPK)))
    return out.reshape(b, s, d)


def encoder_trunk(x, mem, norm_mix, w_in, ret_norm, gdn_conv, gdn_a_log, gdn_dt_bias, gdn_norm, w_out,
                  norm_xa, norm_mem, w_xq, w_xkv, w_xo, norm_ffn, peer_wq, peer_keys, peer_u, peer_v,
                  norm_final):
    for l in range(DEPTH):
        hn = rms_norm(x, norm_mix[l])
        x = x + token_mixer(hn, w_in[l], ret_norm[l], gdn_conv[l], gdn_a_log[l], gdn_dt_bias[l],
                            gdn_norm[l], w_out[l]).astype(x.dtype)
        hn = rms_norm(x, norm_xa[l])
        mn = rms_norm(mem, norm_mem[l])
        x = x + memory_cross_attention(hn, mn, w_xq[l], w_xkv[l], w_xo[l]).astype(x.dtype)
        hn = rms_norm(x, norm_ffn[l])
        x = x + peer_ffn(hn, peer_wq[l], peer_keys[l], peer_u[l], peer_v[l]).astype(x.dtype)
    return rms_norm(x, norm_final)


def setup_inputs(seed: int = 0) -> dict:
    key = jax.random.key(seed)
    ks = jax.random.split(key, 26)
    f32 = jnp.float32
    nrm = lambda k, shape, scale: jax.random.normal(k, shape, f32) * scale
    gain = lambda k, shape: 1.0 + 0.02 * jax.random.normal(k, shape, f32)
    a_log = jnp.log(jax.random.uniform(ks[8], (DEPTH, 2, GDN_HEADS), f32, minval=1.0, maxval=16.0))
    dt = jnp.exp(jax.random.uniform(ks[9], (DEPTH, 2, GDN_HEADS), f32,
                                    minval=float(np.log(1e-3)), maxval=float(np.log(1e-1))))
    dt_bias = dt + jnp.log(-jnp.expm1(-dt))
    return {
        'x_prompt': nrm(ks[0], (BATCH, SEQ, D_MODEL), 1.0),
        'x_sample': nrm(ks[1], (DEC_BATCH, DEC_SEQ, D_MODEL), 1.0),
        'mem_prompt': nrm(ks[2], (BATCH, N_MEM, D_MODEL), 1.0),
        'mem_sample': nrm(ks[3], (DEC_BATCH, N_MEM, D_MODEL), 1.0),
        'norm_mix': gain(ks[4], (DEPTH, D_MODEL)),
        'w_in': nrm(ks[5], (DEPTH, D_MODEL, IN_W), D_MODEL ** -0.5),
        'ret_norm': gain(ks[6], (DEPTH, RET_W)),
        'gdn_conv': nrm(ks[7], (DEPTH, CONV_W, 2 * GDN_QK + GDN_W), CONV_W ** -0.5),
        'gdn_a_log': a_log,
        'gdn_dt_bias': dt_bias,
        'gdn_norm': gain(ks[10], (DEPTH, GDN_DV)),
        'w_out': nrm(ks[11], (DEPTH, MIX_W, D_MODEL), MIX_W ** -0.5),
        'norm_xa': gain(ks[12], (DEPTH, D_MODEL)),
        'norm_mem': gain(ks[13], (DEPTH, D_MODEL)),
        'w_xq': nrm(ks[14], (DEPTH, D_MODEL, D_MODEL), D_MODEL ** -0.5),
        'w_xkv': nrm(ks[15], (DEPTH, D_MODEL, 2 * D_MODEL), D_MODEL ** -0.5),
        'w_xo': nrm(ks[16], (DEPTH, D_MODEL, D_MODEL), D_MODEL ** -0.5),
        'norm_ffn': gain(ks[17], (DEPTH, D_MODEL)),
        'peer_wq': nrm(ks[18], (DEPTH, D_MODEL, PEER_HEADS * PEER_DKEY), D_MODEL ** -0.5),
        'peer_keys': nrm(ks[19], (DEPTH, PEER_HEADS, 2, PEER_NKEYS, PEER_DHALF), PEER_DHALF ** -0.5),
        'peer_u': nrm(ks[20], (DEPTH, PEER_EXPERTS, D_MODEL), D_MODEL ** -0.5),
        'peer_v': nrm(ks[21], (DEPTH, PEER_EXPERTS, D_MODEL), PEER_HEADS ** -0.5),
        'norm_final': gain(ks[22], (D_MODEL,)),
    }


def reference(x_prompt, x_sample, mem_prompt, mem_sample, norm_mix, w_in, ret_norm, gdn_conv, gdn_a_log,
              gdn_dt_bias, gdn_norm, w_out, norm_xa, norm_mem, w_xq, w_xkv, w_xo, norm_ffn, peer_wq,
              peer_keys, peer_u, peer_v, norm_final):
    y_prompt = encoder_trunk(x_prompt, mem_prompt, norm_mix, w_in, ret_norm, gdn_conv, gdn_a_log, gdn_dt_bias,
                             gdn_norm, w_out, norm_xa, norm_mem, w_xq, w_xkv, w_xo, norm_ffn, peer_wq,
                             peer_keys, peer_u, peer_v, norm_final)
    y_sample = encoder_trunk(x_sample, mem_sample, norm_mix, w_in, ret_norm, gdn_conv, gdn_a_log, gdn_dt_bias,
                             gdn_norm, w_out, norm_xa, norm_mem, w_xq, w_xkv, w_xo, norm_ffn, peer_wq,
                             peer_keys, peer_u, peer_v, norm_final)
    return (y_prompt, y_sample)
```

```python
import functools

import jax
import jax.numpy as jnp
from jax import lax
from jax.experimental import pallas as pl
from jax.experimental.pallas import tpu as pltpu

F32 = jnp.float32
BF16 = jnp.bfloat16
HIGHEST = lax.Precision.HIGHEST

EPS = 1e-6
LANES = 128
HEAD_DIM = 128
RET_HEADS = 4
GDN_HEADS = 4
RET_CHUNK = 128
GDN_CHUNK = 64
GDN_SUPER = 256
ROPE_BASE = 10000.0
XA_HEADS = 4
PEER_HEADS = 8
PEER_NKEYS = 128
PEER_TOPK = 16
VMEM_LIMIT = 56 * 1024 * 1024

_NT = (((1,), (1,)), ((), ()))
_TN = (((0,), (0,)), ((), ()))


def _params(n_axes):
    return pltpu.CompilerParams(
        dimension_semantics=("arbitrary",) * n_axes, vmem_limit_bytes=VMEM_LIMIT)


def _dot(a, b):
    return jnp.dot(a, b, preferred_element_type=F32)


def _dot_nt(a, b):
    return lax.dot_general(a, b, _NT, preferred_element_type=F32)


def _dot_tn(a, b):
    return lax.dot_general(a, b, _TN, preferred_element_type=F32)


def _hdot(a, b):
    return jnp.dot(a, b, precision=HIGHEST, preferred_element_type=F32)


def _rms(x, g):
    return x * lax.rsqrt(jnp.mean(x * x, axis=-1, keepdims=True) + EPS) * g


def _norm_matmul_body(x_ref, g_ref, w_ref, o_ref, xn_ref):
    @pl.when(pl.program_id(1) == 0)
    def _():
        xn_ref[...] = _rms(x_ref[...], g_ref[...]).astype(BF16)

    o_ref[...] = _dot(xn_ref[...], w_ref[...]).astype(o_ref.dtype)


def norm_matmul(x, g, w, *, tm, tn, out_dtype=F32):
    t, d = x.shape
    n = w.shape[1]
    return pl.pallas_call(
        _norm_matmul_body,
        out_shape=jax.ShapeDtypeStruct((t, n), out_dtype),
        grid=(t // tm, n // tn),
        in_specs=[pl.BlockSpec((tm, d), lambda i, j: (i, 0)),
                  pl.BlockSpec((1, d), lambda i, j: (0, 0)),
                  pl.BlockSpec((d, tn), lambda i, j: (0, j))],
        out_specs=pl.BlockSpec((tm, tn), lambda i, j: (i, j)),
        scratch_shapes=[pltpu.VMEM((tm, d), BF16)],
        compiler_params=_params(2),
        name="norm_matmul",
    )(x, g.reshape(1, d), w)


def _norm_matmul_side_body(x_ref, g_ref, w_ref, ws_ref, o_ref, os_ref, xn_ref):
    @pl.when(pl.program_id(1) == 0)
    def _():
        xn = _rms(x_ref[...], g_ref[...]).astype(BF16)
        xn_ref[...] = xn
        os_ref[...] = _dot_nt(ws_ref[...], xn)

    o_ref[...] = _dot(xn_ref[...], w_ref[...])


def norm_matmul_side(x, g, w, ws_t, *, tm, tn):
    t, d = x.shape
    n = w.shape[1]
    rows = ws_t.shape[0]
    return pl.pallas_call(
        _norm_matmul_side_body,
        out_shape=(jax.ShapeDtypeStruct((t, n), F32),
                   jax.ShapeDtypeStruct((rows, t), F32)),
        grid=(t // tm, n // tn),
        in_specs=[pl.BlockSpec((tm, d), lambda i, j: (i, 0)),
                  pl.BlockSpec((1, d), lambda i, j: (0, 0)),
                  pl.BlockSpec((d, tn), lambda i, j: (0, j)),
                  pl.BlockSpec((rows, d), lambda i, j: (0, 0))],
        out_specs=(pl.BlockSpec((tm, tn), lambda i, j: (i, j)),
                   pl.BlockSpec((rows, tm), lambda i, j: (0, i))),
        scratch_shapes=[pltpu.VMEM((tm, d), BF16)],
        compiler_params=_params(2),
        name="norm_matmul_side",
    )(x, g.reshape(1, d), w, ws_t)


def _out_proj_body(x_ref, a1_ref, a2_ref, w1_ref, w2_ref, o_ref):
    o_ref[...] = (x_ref[...] + _dot(a1_ref[...], w1_ref[...])
                  + _dot(a2_ref[...], w2_ref[...]))


def out_proj_residual(x, a1, a2, w1, w2, *, tm):
    t, d = x.shape
    k1, k2 = a1.shape[1], a2.shape[1]
    return pl.pallas_call(
        _out_proj_body,
        out_shape=jax.ShapeDtypeStruct((t, d), F32),
        grid=(t // tm,),
        in_specs=[pl.BlockSpec((tm, d), lambda i: (i, 0)),
                  pl.BlockSpec((tm, k1), lambda i: (i, 0)),
                  pl.BlockSpec((tm, k2), lambda i: (i, 0)),
                  pl.BlockSpec((k1, d), lambda i: (0, 0)),
                  pl.BlockSpec((k2, d), lambda i: (0, 0))],
        out_specs=pl.BlockSpec((tm, d), lambda i: (i, 0)),
        compiler_params=_params(1),
        name="out_proj_residual",
    )(x, a1, a2, w1, w2)


def _retention_body(q_ref, k_ref, v_ref, g_ref, cos_ref, sin_ref, dec_ref, rn_ref,
                    o_ref, qr_s, kr_s, o_s):
    s_len = q_ref.shape[0]
    c = RET_CHUNK
    n = s_len // c
    lane = lax.broadcasted_iota(jnp.int32, (s_len, HEAD_DIM), 1)
    even = (lane & 1) == 0
    cos = cos_ref[...]
    sin = sin_ref[...]

    def rot(x):
        xs = jnp.where(even, pltpu.roll(x, HEAD_DIM - 1, 1), pltpu.roll(x, 1, 1))
        return x * cos + xs * sin

    qr_s[...] = rot(q_ref[...])
    kr_s[...] = rot(k_ref[...]) * (HEAD_DIM ** -0.5)

    dmat = dec_ref[0]
    qd_f, kd_f, qd_b, kd_b = dec_ref[1], dec_ref[2], dec_ref[3], dec_ref[4]
    gc_f, gc_b = dec_ref[5], dec_ref[6]

    def rows(i):
        return pl.ds(pl.multiple_of(i * c, c), c)

    def fwd(i, st):
        r = rows(i)
        qc, kc, vc = qr_s[r, :], kr_s[r, :], v_ref[r, :]
        vb = vc.astype(BF16)
        sc = _dot_nt(qc.astype(BF16), kc.astype(BF16)) * dmat
        o = _dot(sc.astype(BF16), vb) + _dot((qc * qd_f).astype(BF16), st.astype(BF16))
        o_s[r, :] = o
        return st * gc_f + _dot_tn((kc * kd_f).astype(BF16), vb)

    lax.fori_loop(0, n, fwd, jnp.zeros((HEAD_DIM, HEAD_DIM), F32))

    def bwd(t, st):
        r = rows(n - 1 - t)
        qc, kc, vc = qr_s[r, :], kr_s[r, :], v_ref[r, :]
        o_s[r, :] = o_s[r, :] + _dot((qc * qd_b).astype(BF16), st.astype(BF16))
        return st * gc_b + _dot_tn((kc * kd_b).astype(BF16), vc.astype(BF16))

    lax.fori_loop(0, n, bwd, jnp.zeros((HEAD_DIM, HEAD_DIM), F32))

    o = o_s[...]
    mu = jnp.mean(o, axis=-1, keepdims=True)
    var = jnp.mean(jnp.square(o - mu), axis=-1, keepdims=True)
    y = (o - mu) * lax.rsqrt(var + EPS)
    y = y * rn_ref[...] * jax.nn.silu(g_ref[...])
    o_ref[...] = y.astype(o_ref.dtype)


def _retention_tables(s_len):
    d = HEAD_DIM
    inv = ROPE_BASE ** (-jnp.arange(0, d, 2, dtype=F32) / d)
    ang = jnp.arange(s_len, dtype=F32)[:, None] * inv[None, :]
    cos = jnp.repeat(jnp.cos(ang), 2, axis=1)
    sin = jnp.stack([-jnp.sin(ang), jnp.sin(ang)], axis=-1).reshape(s_len, d)
    h = jnp.arange(RET_HEADS, dtype=F32)
    lg_f = jnp.log1p(-jnp.exp2(-5.0 - h))[:, None, None]
    lg_b = jnp.log1p(-jnp.exp2(-5.5 - h))[:, None, None]
    c = RET_CHUNK
    pos = jnp.arange(c, dtype=F32)
    diff = (pos[:, None] - pos[None, :])[None]
    dmat = jnp.where(diff >= 0, jnp.exp(lg_f * jnp.where(diff >= 0, diff, 0.0)),
                     jnp.exp(lg_b * jnp.where(diff < 0, -diff, 0.0)))
    col = lambda v: jnp.broadcast_to(v, (RET_HEADS, c, d))
    p = pos[None, :, None]
    dec = jnp.stack([
        dmat,
        col(jnp.exp(lg_f * (p + 1.0))),
        col(jnp.exp(lg_f * (c - 1.0 - p))),
        col(jnp.exp(lg_b * (c - p))),
        col(jnp.exp(lg_b * p)),
        col(jnp.exp(lg_f * c)),
        col(jnp.exp(lg_b * c)),
    ], axis=1)
    return cos, sin, dec


def retention(proj, ret_norm, b, s_len):
    cos, sin, dec = _retention_tables(s_len)
    hd = HEAD_DIM
    head_blk = lambda off: pl.BlockSpec((None, s_len, hd), lambda i, h: (i, 0, off + h))
    return pl.pallas_call(
        _retention_body,
        out_shape=jax.ShapeDtypeStruct((b, s_len, RET_HEADS * hd), BF16),
        grid=(b, RET_HEADS),
        in_specs=[head_blk(0), head_blk(RET_HEADS), head_blk(2 * RET_HEADS), head_blk(3 * RET_HEADS),
                  pl.BlockSpec((s_len, hd), lambda i, h: (0, 0)),
                  pl.BlockSpec((s_len, hd), lambda i, h: (0, 0)),
                  pl.BlockSpec((None, 7, RET_CHUNK, hd), lambda i, h: (h, 0, 0, 0)),
                  pl.BlockSpec((1, hd), lambda i, h: (0, h))],
        out_specs=pl.BlockSpec((None, s_len, hd), lambda i, h: (i, 0, h)),
        scratch_shapes=[pltpu.VMEM((s_len, hd), F32)] * 3,
        compiler_params=_params(2),
        name="retention",
    )(proj, proj, proj, proj, cos, sin, dec, ret_norm.reshape(1, -1))


def _gdn_gates_body(raw_ref, al_ref, dt_ref, o_ref):
    s_len = raw_ref.shape[-1]
    nh = 2 * GDN_HEADS
    gb = raw_ref[0:nh, :]
    ga = raw_ref[nh:2 * nh, :]
    beta = jax.nn.sigmoid(gb)
    g = -jnp.exp(al_ref[...]) * jax.nn.softplus(ga + dt_ref[...])
    pos = lax.broadcasted_iota(jnp.int32, (nh, s_len), 1) & (GDN_CHUNK - 1)
    fwd = g
    rev = g
    k = 1
    while k < GDN_CHUNK:
        fwd = fwd + jnp.where(pos >= k, pltpu.roll(fwd, k, 1), 0.0)
        rev = rev + jnp.where(pos < GDN_CHUNK - k, pltpu.roll(rev, s_len - k, 1), 0.0)
        k *= 2
    row = lax.broadcasted_iota(jnp.int32, (nh, s_len), 0)
    o_ref[0:nh, :] = jnp.where(row < GDN_HEADS, fwd, rev)
    o_ref[nh:2 * nh, :] = beta
    o_ref[2 * nh:3 * nh, :] = fwd + rev - g


def gdn_gates(raw, a_log, dt_bias, b, s_len):
    nh = 2 * GDN_HEADS
    return pl.pallas_call(
        _gdn_gates_body,
        out_shape=jax.ShapeDtypeStruct((b, 3 * nh, s_len), F32),
        grid=(b,),
        in_specs=[pl.BlockSpec((2 * nh, s_len), lambda i: (0, i)),
                  pl.BlockSpec((nh, 1), lambda i: (0, 0)),
                  pl.BlockSpec((nh, 1), lambda i: (0, 0))],
        out_specs=pl.BlockSpec((None, 3 * nh, s_len), lambda i: (i, 0, 0)),
        compiler_params=_params(1),
        name="gdn_gates",
    )(raw, a_log.reshape(nh, 1), dt_bias.reshape(nh, 1))


def _gdn_prep_body(x_ref, w_ref, o_ref):
    s_len = x_ref.shape[0]
    j = pl.program_id(1)
    x = x_ref[...]
    t = lax.broadcasted_iota(jnp.int32, x.shape, 0)
    prev = jnp.where(t == 0, 0.0, pltpu.roll(x, 1, 0))
    nxt = jnp.where(t == s_len - 1, 0.0, pltpu.roll(x, s_len - 1, 0))
    conv = prev * w_ref[0:1, :] + x * w_ref[1:2, :] + nxt * w_ref[2:3, :]
    act = jax.nn.silu(conv)
    inv = lax.rsqrt(jnp.sum(act * act, axis=-1, keepdims=True) + EPS)
    is_q = j < GDN_HEADS
    is_v = j >= 2 * GDN_HEADS
    scale = jnp.where(is_v, 1.0, jnp.where(is_q, inv * (HEAD_DIM ** -0.5), inv))
    o_ref[...] = act * scale


def gdn_prep(proj, conv_w, b, s_len):
    hd = HEAD_DIM
    nblk = 3 * GDN_HEADS
    first = 4 * RET_HEADS
    return pl.pallas_call(
        _gdn_prep_body,
        out_shape=jax.ShapeDtypeStruct((b, s_len, nblk * hd), F32),
        grid=(b, nblk),
        in_specs=[pl.BlockSpec((None, s_len, hd), lambda i, j: (i, 0, first + j)),
                  pl.BlockSpec((3, hd), lambda i, j: (0, j))],
        out_specs=pl.BlockSpec((None, s_len, hd), lambda i, j: (i, 0, j)),
        compiler_params=_params(2),
        name="gdn_prep",
    )(proj, conv_w)


def _tri_inverse(m, ri, ci):
    blk16 = (ri >> 4) == (ci >> 4)
    blk32 = (ri >> 5) == (ci >> 5)
    eye = (ri == ci).astype(F32)
    nm = jnp.where(blk16, -m, 0.0)
    p = eye + nm
    for _ in range(3):
        nm = _hdot(nm, nm)
        p = p + _hdot(p, nm)
    c1 = jnp.where(blk32 & jnp.logical_not(blk16), m, 0.0)
    p = p - _hdot(_hdot(p, c1), p)
    c2 = jnp.where(blk32, 0.0, m)
    p = p - _hdot(_hdot(p, c2), p)
    return p


def _gdn_super_chunk(q, k, v, gcol, grow, bcol, glcol, st, rev):
    r = GDN_SUPER
    c = GDN_CHUNK
    ri = lax.broadcasted_iota(jnp.int32, (r, r), 0)
    ci = lax.broadcasted_iota(jnp.int32, (r, r), 1)
    same = (ri >> 6) == (ci >> 6)
    if rev:
        incl = same & (ri <= ci)
        strict = same & (ri < ci)
    else:
        incl = same & (ri >= ci)
        strict = same & (ri > ci)
    decay = jnp.exp(jnp.where(incl, gcol - grow, -jnp.inf))
    kb = k * bcol
    vb = v * bcol
    k16 = k.astype(BF16)
    m = jnp.where(strict, _dot_nt(kb.astype(BF16), k16) * decay, 0.0)
    qk = jnp.where(incl, _dot_nt(q.astype(BF16), k16) * decay, 0.0).astype(BF16)
    t = _tri_inverse(m, ri, ci)
    eg = jnp.exp(gcol)
    rhs = jnp.concatenate([vb, kb * eg], axis=1).astype(BF16)
    uw = _dot(t.astype(BF16), rhs)
    u = uw[:, :HEAD_DIM]
    w = uw[:, HEAD_DIM:].astype(BF16)
    qd = (q * eg).astype(BF16)
    kt = (k * jnp.exp(glcol - gcol)).astype(BF16)
    outs = [None] * (r // c)
    order = range(r // c - 1, -1, -1) if rev else range(r // c)
    for i in order:
        sl = slice(i * c, (i + 1) * c)
        s16 = st.astype(BF16)
        vn = u[sl] - _dot(w[sl], s16)
        vn16 = vn.astype(BF16)
        pair = slice((i // 2) * 2 * c, (i // 2 + 1) * 2 * c)
        zero = jnp.zeros_like(vn16)
        vpad = jnp.concatenate([zero, vn16] if i % 2 else [vn16, zero], axis=0)
        outs[i] = _dot(qd[sl], s16) + _dot(qk[sl, pair], vpad)
        st = st * jnp.exp(glcol[i * c:i * c + 1, :]) + _dot_tn(kt[sl], vn16)
    return jnp.concatenate(outs, axis=0), st


def _gdn_core_body(q_ref, k_ref, v_ref, z_ref, col_ref, row_ref, gn_ref, o_ref, o_s):
    s_len = q_ref.shape[0]
    r = GDN_SUPER
    n = s_len // r

    def run(rev):
        base = 3 if rev else 0

        def step(t, st):
            i = (n - 1 - t) if rev else t
            rows = pl.ds(pl.multiple_of(i * r, r), r)
            cols = col_ref[rows, :]
            gcol = cols[:, base:base + 1]
            bcol = cols[:, base + 1:base + 2]
            glcol = cols[:, base + 2:base + 3]
            grow = row_ref[i][base:base + 1, :]
            o, st = _gdn_super_chunk(q_ref[rows, :], k_ref[rows, :], v_ref[rows, :],
                                     gcol, grow, bcol, glcol, st, rev)
            if rev:
                o_s[rows, :] = o_s[rows, :] + o
            else:
                o_s[rows, :] = o
            return st

        lax.fori_loop(0, n, step, jnp.zeros((HEAD_DIM, HEAD_DIM), F32))

    run(False)
    run(True)
    o = o_s[...]
    y = o * lax.rsqrt(jnp.mean(o * o, axis=-1, keepdims=True) + EPS) * gn_ref[...]
    o_ref[...] = (y * jax.nn.silu(z_ref[...])).astype(o_ref.dtype)


def gdn_core(qkv, proj, gate_col, gate_row, gdn_norm, b, s_len):
    hd = HEAD_DIM
    z_first = 4 * RET_HEADS + 3 * GDN_HEADS
    qkv_blk = lambda off: pl.BlockSpec((None, s_len, hd), lambda i, h: (i, 0, off + h))
    return pl.pallas_call(
        _gdn_core_body,
        out_shape=jax.ShapeDtypeStruct((b, s_len, GDN_HEADS * hd), BF16),
        grid=(b, GDN_HEADS),
        in_specs=[qkv_blk(0), qkv_blk(GDN_HEADS), qkv_blk(2 * GDN_HEADS),
                  pl.BlockSpec((None, s_len, hd), lambda i, h: (i, 0, z_first + h)),
                  pl.BlockSpec((None, None, s_len, 8), lambda i, h: (i, h, 0, 0)),
                  pl.BlockSpec((None, None, s_len // GDN_SUPER, 8, GDN_SUPER),
                               lambda i, h: (i, h, 0, 0, 0)),
                  pl.BlockSpec((1, hd), lambda i, h: (0, 0))],
        out_specs=pl.BlockSpec((None, s_len, hd), lambda i, h: (i, 0, h)),
        scratch_shapes=[pltpu.VMEM((s_len, hd), F32)],
        compiler_params=_params(2),
        name="gdn_core",
    )(qkv, qkv, qkv, proj, gate_col, gate_row, gdn_norm.reshape(1, hd))


def _xattn_body(x_ref, g_ref, wq_ref, kv_ref, wo_ref, o_ref):
    x = x_ref[...]
    d = x.shape[1]
    dh = d // XA_HEADS
    q = _dot(_rms(x, g_ref[...]).astype(BF16), wq_ref[...])
    outs = []
    for h in range(XA_HEADS):
        kh = kv_ref[:, h * dh:(h + 1) * dh].astype(BF16)
        vh = kv_ref[:, d + h * dh:d + (h + 1) * dh].astype(BF16)
        sc = _dot_nt(q[:, h * dh:(h + 1) * dh].astype(BF16), kh) * (dh ** -0.5)
        p = jax.nn.softmax(sc, axis=-1)
        outs.append(_dot(p.astype(BF16), vh).astype(BF16))
    o = jnp.concatenate(outs, axis=1)
    o_ref[...] = x + _dot(o, wo_ref[...])


def xattn_residual(x, g, wq, kv, wo, b, s_len, *, ts):
    t, d = x.shape
    m = kv.shape[1]
    nblk = s_len // ts
    return pl.pallas_call(
        _xattn_body,
        out_shape=jax.ShapeDtypeStruct((t, d), F32),
        grid=(b, nblk),
        in_specs=[pl.BlockSpec((ts, d), lambda i, j: (i * nblk + j, 0)),
                  pl.BlockSpec((1, d), lambda i, j: (0, 0)),
                  pl.BlockSpec((d, d), lambda i, j: (0, 0)),
                  pl.BlockSpec((None, m, 2 * d), lambda i, j: (i, 0, 0)),
                  pl.BlockSpec((d, d), lambda i, j: (0, 0))],
        out_specs=pl.BlockSpec((ts, d), lambda i, j: (i * nblk + j, 0)),
        compiler_params=_params(2),
        name="xattn_residual",
    )(x, g.reshape(1, d), wq, kv, wo)


def _extract_topk(s, n_rows, k):
    iota = lax.broadcasted_iota(jnp.int32, s.shape, 0).astype(F32)
    rank = jnp.full(s.shape, float(k), F32)
    vals = []
    for r in range(k):
        m = jnp.max(s, axis=0, keepdims=True)
        idx = jnp.min(jnp.where(s == m, iota, float(n_rows)), axis=0, keepdims=True)
        hit = iota == idx
        rank = jnp.where(hit, float(r), rank)
        s = jnp.where(hit, -jnp.inf, s)
        vals.append(m)
    return rank, vals


def _peer_route_body(q_ref, keys_ref, c1_ref, n1_ref, e2_ref, r2_ref):
    tb = q_ref.shape[0]
    nk = PEER_NKEYS
    kk = PEER_TOPK
    for blk in range(tb // LANES):
        tok = slice(blk * LANES, (blk + 1) * LANES)
        q = q_ref[tok, :].astype(BF16)
        s1 = _dot_nt(keys_ref[0], q[:, :nk])
        s2 = _dot_nt(keys_ref[1], q[:, nk:])
        r1, v1 = _extract_topk(s1, nk, kk)
        r2, v2 = _extract_topk(s2, nk, kk)
        v2m = jnp.concatenate(v2, axis=0)
        cand = jnp.concatenate([v1[a] + v2m for a in range(kk)], axis=0)
        rc, _ = _extract_topk(cand, kk * kk, kk)
        sel = rc < float(kk)
        top = v1[0] + v2[0]
        z = jnp.sum(jnp.where(sel, jnp.exp(cand - top), 0.0), axis=0, keepdims=True)
        sel_f = jnp.where(sel, 1.0, 0.0)
        n1 = jnp.zeros((nk, LANES), F32)
        for a in range(kk):
            cnt = jnp.sum(sel_f[a * kk:(a + 1) * kk, :], axis=0, keepdims=True)
            n1 = jnp.where(r1 == float(a), cnt, n1)
        c1_ref[:, tok] = jnp.where(r1 < float(kk), jnp.exp(s1 - v1[0]) / z, 0.0)
        n1_ref[:, tok] = n1
        e2_ref[:, tok] = jnp.exp(s2 - v2[0])
        r2_ref[:, tok] = r2


def peer_route(q, keys, *, tb):
    t = q.shape[0]
    nk = PEER_NKEYS
    out = jax.ShapeDtypeStruct((PEER_HEADS, nk, t), F32)
    ospec = pl.BlockSpec((None, nk, tb), lambda i, h: (h, 0, i))
    return pl.pallas_call(
        _peer_route_body,
        out_shape=(out,) * 4,
        grid=(t // tb, PEER_HEADS),
        in_specs=[pl.BlockSpec((tb, 2 * nk), lambda i, h: (i, h)),
                  pl.BlockSpec((None, 2, nk, nk), lambda i, h: (h, 0, 0, 0))],
        out_specs=(ospec,) * 4,
        compiler_params=_params(2),
        name="peer_route",
    )(q, keys)


def _peer_dense_body(x_ref, g_ref, c1_ref, n1_ref, e2_ref, r2_ref, u_ref, vt_ref, *rest, final):
    if final:
        gf_ref, o_ref, xn_s, w_s, acc_s = rest
    else:
        o_ref, xn_s, w_s, acc_s = rest
    j = pl.program_id(1)
    nk = PEER_NKEYS
    grp = u_ref.shape[0] // nk

    @pl.when(j == 0)
    def _():
        xn_s[...] = _rms(x_ref[...], g_ref[...]).astype(BF16)
        acc_s[...] = jnp.zeros_like(acc_s)

    xn = xn_s[...]
    rows = pl.ds(pl.multiple_of(j * grp, grp), grp)
    for ii in range(grp):
        st = _dot_nt(u_ref[ii * nk:(ii + 1) * nk, :], xn)
        gate = jnp.zeros(st.shape, F32)
        for h in range(PEER_HEADS):
            c1 = c1_ref[h, rows, :][ii:ii + 1, :]
            n1 = n1_ref[h, rows, :][ii:ii + 1, :]
            gate = gate + jnp.where(r2_ref[h] < n1, e2_ref[h], 0.0) * c1
        w_s[ii * nk:(ii + 1) * nk, :] = (jax.nn.gelu(st) * gate).astype(BF16)
    acc_s[...] += _dot(vt_ref[...], w_s[...])

    @pl.when(j == pl.num_programs(1) - 1)
    def _():
        y = x_ref[...] + acc_s[...].T
        if final:
            y = _rms(y, gf_ref[...])
        o_ref[...] = y


def peer_dense_residual(x, g, route, u, vt, g_final, *, tb, grp):
    t, d = x.shape
    nk = PEER_NKEYS
    eb = grp * nk
    rspec = pl.BlockSpec((PEER_HEADS, nk, tb), lambda i, j: (0, 0, i))
    in_specs = [pl.BlockSpec((tb, d), lambda i, j: (i, 0)),
                pl.BlockSpec((1, d), lambda i, j: (0, 0)),
                rspec, rspec, rspec, rspec,
                pl.BlockSpec((eb, d), lambda i, j: (j, 0)),
                pl.BlockSpec((d, eb), lambda i, j: (0, j))]
    args = [x, g.reshape(1, d), *route, u, vt]
    final = g_final is not None
    if final:
        in_specs.append(pl.BlockSpec((1, d), lambda i, j: (0, 0)))
        args.append(g_final.reshape(1, d))
    return pl.pallas_call(
        functools.partial(_peer_dense_body, final=final),
        out_shape=jax.ShapeDtypeStruct((t, d), F32),
        grid=(t // tb, (nk * nk) // eb),
        in_specs=in_specs,
        out_specs=pl.BlockSpec((tb, d), lambda i, j: (i, 0)),
        scratch_shapes=[pltpu.VMEM((tb, d), BF16),
                        pltpu.VMEM((eb, tb), BF16),
                        pltpu.VMEM((d, tb), F32)],
        compiler_params=_params(2),
        name="peer_dense_final" if final else "peer_dense",
    )(*args)


def _pick(n, prefs):
    for p in prefs:
        if n % p == 0:
            return p
    return n


def _encoder(x3, mem3, norm_mix, w_in, ret_norm, gdn_conv, gdn_a_log, gdn_dt_bias, gdn_norm, w_out,
             norm_xa, norm_mem, w_xq, w_xkv, w_xo, norm_ffn, peer_wq, peer_keys, peer_u, peer_v,
             norm_final):
    b, s_len, d = x3.shape
    n_mem = mem3.shape[1]
    t = b * s_len
    depth = w_in.shape[0]
    x = x3.reshape(t, d)
    mem = mem3.reshape(b * n_mem, d)
    n_main = 4 * RET_HEADS * HEAD_DIM + 4 * GDN_HEADS * HEAD_DIM
    tm = _pick(t, (1024, 512, 256, 128))
    tmm = _pick(b * n_mem, (1024, 512, 256, 128))
    ts = _pick(s_len, (512, 256, 128))
    tb_route = _pick(t, (256, 128))
    tb_dense = _pick(t, (512, 256, 128))
    nh = 2 * GDN_HEADS

    for l in range(depth):
        w_main = w_in[l, :, :n_main].astype(BF16)
        w_gate_t = w_in[l, :, n_main:].T.astype(BF16)
        proj, graw = norm_matmul_side(x, norm_mix[l], w_main, w_gate_t, tm=tm, tn=1024)
        proj3 = proj.reshape(b, s_len, n_main)
        o_r = retention(proj3, ret_norm[l], b, s_len)
        gates = gdn_gates(graw, gdn_a_log[l], gdn_dt_bias[l], b, s_len)
        gates = gates.reshape(b, 3, 2, GDN_HEADS, s_len).transpose(0, 3, 2, 1, 4)
        gate_row = jnp.pad(gates.reshape(b, GDN_HEADS, 6, s_len), ((0, 0), (0, 0), (0, 2), (0, 0)))
        gate_col = jnp.swapaxes(gate_row, 2, 3)
        gate_row = gate_row.reshape(b, GDN_HEADS, 8, s_len // GDN_SUPER, GDN_SUPER).transpose(0, 1, 3, 2, 4)
        qkv = gdn_prep(proj3, gdn_conv[l], b, s_len)
        o_g = gdn_core(qkv, proj3, gate_col, gate_row, gdn_norm[l], b, s_len)
        w_o = w_out[l].astype(BF16)
        n_r = RET_HEADS * HEAD_DIM
        x = out_proj_residual(x, o_r.reshape(t, -1), o_g.reshape(t, -1), w_o[:n_r], w_o[n_r:], tm=tm)

        kv = norm_matmul(mem, norm_mem[l], w_xkv[l].astype(BF16), tm=tmm, tn=1024)
        x = xattn_residual(x, norm_xa[l], w_xq[l].astype(BF16), kv.reshape(b, n_mem, 2 * d),
                           w_xo[l].astype(BF16), b, s_len, ts=ts)

        pq = norm_matmul(x, norm_ffn[l], peer_wq[l].astype(BF16), tm=tm, tn=1024)
        route = peer_route(pq, peer_keys[l].astype(BF16), tb=tb_route)
        x = peer_dense_residual(x, norm_ffn[l], route, peer_u[l].astype(BF16),
                                peer_v[l].astype(BF16).T,
                                norm_final if l == depth - 1 else None, tb=tb_dense, grp=8)
    return x.reshape(b, s_len, d)


def kernel(x_prompt, x_sample, mem_prompt, mem_sample, norm_mix, w_in, ret_norm, gdn_conv, gdn_a_log,
           gdn_dt_bias, gdn_norm, w_out, norm_xa, norm_mem, w_xq, w_xkv, w_xo, norm_ffn, peer_wq,
           peer_keys, peer_u, peer_v, norm_final):
    weights = (norm_mix, w_in, ret_norm, gdn_conv, gdn_a_log, gdn_dt_bias, gdn_norm, w_out,
               norm_xa, norm_mem, w_xq, w_xkv, w_xo, norm_ffn, peer_wq, peer_keys, peer_u, peer_v,
               norm_final)
    if x_prompt.shape[1:] == x_sample.shape[1:] and mem_prompt.shape[1:] == mem_sample.shape[1:]:
        nb = x_prompt.shape[0]
        y = _encoder(jnp.concatenate([x_prompt, x_sample], axis=0),
                     jnp.concatenate([mem_prompt, mem_sample], axis=0), *weights)
        return (y[:nb], y[nb:])
    return (_encoder(x_prompt, mem_prompt, *weights), _encoder(x_sample, mem_sample, *weights))
```

```python
import functools

import jax
import jax.numpy as jnp
from jax import lax
from jax.experimental import pallas as pl
from jax.experimental.pallas import tpu as pltpu

F32 = jnp.float32
BF16 = jnp.bfloat16

EPS = 1e-6
LANES = 128
HEAD_DIM = 128
RET_HEADS = 4
GDN_HEADS = 4
RET_CHUNK = 128
GDN_CHUNK = 64
GDN_SUPER = 256
GDN_HEADS_PER_STEP = 2
ROPE_BASE = 10000.0
XA_HEADS = 4
PEER_HEADS = 8
PEER_NKEYS = 128
PEER_TOPK = 16
VMEM_LIMIT = 56 * 1024 * 1024

_NT = (((1,), (1,)), ((), ()))
_TN = (((0,), (0,)), ((), ()))


def _params(n_axes):
    return pltpu.CompilerParams(
        dimension_semantics=("arbitrary",) * n_axes, vmem_limit_bytes=VMEM_LIMIT)


def _dot(a, b):
    return jnp.dot(a, b, preferred_element_type=F32)


def _dot_nt(a, b):
    return lax.dot_general(a, b, _NT, preferred_element_type=F32)


def _dot_tn(a, b):
    return lax.dot_general(a, b, _TN, preferred_element_type=F32)


def _split(a):
    hi = a.astype(BF16)
    return hi, (a - hi.astype(F32)).astype(BF16)


def _dot3(a, b):
    (ah, al), (bh, bl) = a, b
    return _dot(jnp.concatenate([ah, ah, al], axis=1), jnp.concatenate([bh, bl, bh], axis=0))


def _rms(x, g):
    return x * lax.rsqrt(jnp.mean(x * x, axis=-1, keepdims=True) + EPS) * g


def _norm_matmul_body(x_ref, g_ref, w_ref, o_ref, xn_ref):
    @pl.when(pl.program_id(1) == 0)
    def _():
        xn_ref[...] = _rms(x_ref[...], g_ref[...]).astype(BF16)

    o_ref[...] = _dot(xn_ref[...], w_ref[...]).astype(o_ref.dtype)


def norm_matmul(x, g, w, *, tm, tn, out_dtype=F32):
    t, d = x.shape
    n = w.shape[1]
    return pl.pallas_call(
        _norm_matmul_body,
        out_shape=jax.ShapeDtypeStruct((t, n), out_dtype),
        grid=(t // tm, n // tn),
        in_specs=[pl.BlockSpec((tm, d), lambda i, j: (i, 0)),
                  pl.BlockSpec((1, d), lambda i, j: (0, 0)),
                  pl.BlockSpec((d, tn), lambda i, j: (0, j))],
        out_specs=pl.BlockSpec((tm, tn), lambda i, j: (i, j)),
        scratch_shapes=[pltpu.VMEM((tm, d), BF16)],
        compiler_params=_params(2),
        name="norm_matmul",
    )(x, g.reshape(1, d), w)


def _norm_matmul_side_body(x_ref, g_ref, w_ref, ws_ref, o_ref, os_ref, xn_ref):
    @pl.when(pl.program_id(1) == 0)
    def _():
        xn = _rms(x_ref[...], g_ref[...]).astype(BF16)
        xn_ref[...] = xn
        os_ref[...] = _dot_nt(ws_ref[...], xn)

    o_ref[...] = _dot(xn_ref[...], w_ref[...])


def norm_matmul_side(x, g, w, ws_t, *, tm, tn):
    t, d = x.shape
    n = w.shape[1]
    rows = ws_t.shape[0]
    return pl.pallas_call(
        _norm_matmul_side_body,
        out_shape=(jax.ShapeDtypeStruct((t, n), F32),
                   jax.ShapeDtypeStruct((rows, t), F32)),
        grid=(t // tm, n // tn),
        in_specs=[pl.BlockSpec((tm, d), lambda i, j: (i, 0)),
                  pl.BlockSpec((1, d), lambda i, j: (0, 0)),
                  pl.BlockSpec((d, tn), lambda i, j: (0, j)),
                  pl.BlockSpec((rows, d), lambda i, j: (0, 0))],
        out_specs=(pl.BlockSpec((tm, tn), lambda i, j: (i, j)),
                   pl.BlockSpec((rows, tm), lambda i, j: (0, i))),
        scratch_shapes=[pltpu.VMEM((tm, d), BF16)],
        compiler_params=_params(2),
        name="norm_matmul_side",
    )(x, g.reshape(1, d), w, ws_t)


def _out_proj_body(x_ref, a1_ref, a2_ref, w1_ref, w2_ref, o_ref):
    o_ref[...] = (x_ref[...] + _dot(a1_ref[...], w1_ref[...])
                  + _dot(a2_ref[...], w2_ref[...]))


def out_proj_residual(x, a1, a2, w1, w2, *, tm):
    t, d = x.shape
    k1, k2 = a1.shape[1], a2.shape[1]
    return pl.pallas_call(
        _out_proj_body,
        out_shape=jax.ShapeDtypeStruct((t, d), F32),
        grid=(t // tm,),
        in_specs=[pl.BlockSpec((tm, d), lambda i: (i, 0)),
                  pl.BlockSpec((tm, k1), lambda i: (i, 0)),
                  pl.BlockSpec((tm, k2), lambda i: (i, 0)),
                  pl.BlockSpec((k1, d), lambda i: (0, 0)),
                  pl.BlockSpec((k2, d), lambda i: (0, 0))],
        out_specs=pl.BlockSpec((tm, d), lambda i: (i, 0)),
        compiler_params=_params(1),
        name="out_proj_residual",
    )(x, a1, a2, w1, w2)


def _retention_body(q_ref, k_ref, v_ref, g_ref, cos_ref, sin_ref, dec_ref, rn_ref,
                    o_ref, qr_s, kr_s, o_s):
    s_len = q_ref.shape[0]
    c = RET_CHUNK
    n = s_len // c
    lane = lax.broadcasted_iota(jnp.int32, (s_len, HEAD_DIM), 1)
    even = (lane & 1) == 0
    cos = cos_ref[...]
    sin = sin_ref[...]

    def rot(x):
        xs = jnp.where(even, pltpu.roll(x, HEAD_DIM - 1, 1), pltpu.roll(x, 1, 1))
        return x * cos + xs * sin

    qr_s[...] = rot(q_ref[...])
    kr_s[...] = rot(k_ref[...]) * (HEAD_DIM ** -0.5)

    dmat = dec_ref[0]
    qd_f, kd_f, qd_b, kd_b = dec_ref[1], dec_ref[2], dec_ref[3], dec_ref[4]
    gc_f, gc_b = dec_ref[5], dec_ref[6]

    def rows(i):
        return pl.ds(pl.multiple_of(i * c, c), c)

    def fwd(i, st):
        r = rows(i)
        qc, kc, vc = qr_s[r, :], kr_s[r, :], v_ref[r, :]
        vb = vc.astype(BF16)
        sc = _dot_nt(qc.astype(BF16), kc.astype(BF16)) * dmat
        o = _dot(sc.astype(BF16), vb) + _dot((qc * qd_f).astype(BF16), st.astype(BF16))
        o_s[r, :] = o
        return st * gc_f + _dot_tn((kc * kd_f).astype(BF16), vb)

    lax.fori_loop(0, n, fwd, jnp.zeros((HEAD_DIM, HEAD_DIM), F32))

    def bwd(t, st):
        r = rows(n - 1 - t)
        qc, kc, vc = qr_s[r, :], kr_s[r, :], v_ref[r, :]
        o_s[r, :] = o_s[r, :] + _dot((qc * qd_b).astype(BF16), st.astype(BF16))
        return st * gc_b + _dot_tn((kc * kd_b).astype(BF16), vc.astype(BF16))

    lax.fori_loop(0, n, bwd, jnp.zeros((HEAD_DIM, HEAD_DIM), F32))

    o = o_s[...]
    mu = jnp.mean(o, axis=-1, keepdims=True)
    var = jnp.mean(jnp.square(o - mu), axis=-1, keepdims=True)
    y = (o - mu) * lax.rsqrt(var + EPS)
    y = y * rn_ref[...] * jax.nn.silu(g_ref[...])
    o_ref[...] = y.astype(o_ref.dtype)


def _retention_tables(s_len):
    d = HEAD_DIM
    inv = ROPE_BASE ** (-jnp.arange(0, d, 2, dtype=F32) / d)
    ang = jnp.arange(s_len, dtype=F32)[:, None] * inv[None, :]
    cos = jnp.repeat(jnp.cos(ang), 2, axis=1)
    sin = jnp.stack([-jnp.sin(ang), jnp.sin(ang)], axis=-1).reshape(s_len, d)
    h = jnp.arange(RET_HEADS, dtype=F32)
    lg_f = jnp.log1p(-jnp.exp2(-5.0 - h))[:, None, None]
    lg_b = jnp.log1p(-jnp.exp2(-5.5 - h))[:, None, None]
    c = RET_CHUNK
    pos = jnp.arange(c, dtype=F32)
    diff = (pos[:, None] - pos[None, :])[None]
    dmat = jnp.where(diff >= 0, jnp.exp(lg_f * jnp.where(diff >= 0, diff, 0.0)),
                     jnp.exp(lg_b * jnp.where(diff < 0, -diff, 0.0)))
    col = lambda v: jnp.broadcast_to(v, (RET_HEADS, c, d))
    p = pos[None, :, None]
    dec = jnp.stack([
        dmat,
        col(jnp.exp(lg_f * (p + 1.0))),
        col(jnp.exp(lg_f * (c - 1.0 - p))),
        col(jnp.exp(lg_b * (c - p))),
        col(jnp.exp(lg_b * p)),
        col(jnp.exp(lg_f * c)),
        col(jnp.exp(lg_b * c)),
    ], axis=1)
    return cos, sin, dec


def retention(proj, ret_norm, b, s_len):
    cos, sin, dec = _retention_tables(s_len)
    hd = HEAD_DIM
    head_blk = lambda off: pl.BlockSpec((None, s_len, hd), lambda i, h: (i, 0, off + h))
    return pl.pallas_call(
        _retention_body,
        out_shape=jax.ShapeDtypeStruct((b, s_len, RET_HEADS * hd), BF16),
        grid=(b, RET_HEADS),
        in_specs=[head_blk(0), head_blk(RET_HEADS), head_blk(2 * RET_HEADS), head_blk(3 * RET_HEADS),
                  pl.BlockSpec((s_len, hd), lambda i, h: (0, 0)),
                  pl.BlockSpec((s_len, hd), lambda i, h: (0, 0)),
                  pl.BlockSpec((None, 7, RET_CHUNK, hd), lambda i, h: (h, 0, 0, 0)),
                  pl.BlockSpec((1, hd), lambda i, h: (0, h))],
        out_specs=pl.BlockSpec((None, s_len, hd), lambda i, h: (i, 0, h)),
        scratch_shapes=[pltpu.VMEM((s_len, hd), F32)] * 3,
        compiler_params=_params(2),
        name="retention",
    )(proj, proj, proj, proj, cos, sin, dec, ret_norm.reshape(1, -1))


def _gdn_gates_body(raw_ref, al_ref, dt_ref, o_ref):
    s_len = raw_ref.shape[-1]
    nh = 2 * GDN_HEADS
    gb = raw_ref[0:nh, :]
    ga = raw_ref[nh:2 * nh, :]
    beta = jax.nn.sigmoid(gb)
    g = -jnp.exp(al_ref[...]) * jax.nn.softplus(ga + dt_ref[...])
    pos = lax.broadcasted_iota(jnp.int32, (nh, s_len), 1) & (GDN_CHUNK - 1)
    fwd = g
    rev = g
    k = 1
    while k < GDN_CHUNK:
        fwd = fwd + jnp.where(pos >= k, pltpu.roll(fwd, k, 1), 0.0)
        rev = rev + jnp.where(pos < GDN_CHUNK - k, pltpu.roll(rev, s_len - k, 1), 0.0)
        k *= 2
    row = lax.broadcasted_iota(jnp.int32, (nh, s_len), 0)
    o_ref[0:nh, :] = jnp.where(row < GDN_HEADS, fwd, rev)
    o_ref[nh:2 * nh, :] = beta
    o_ref[2 * nh:3 * nh, :] = fwd + rev - g


def gdn_gates(raw, a_log, dt_bias, b, s_len):
    nh = 2 * GDN_HEADS
    return pl.pallas_call(
        _gdn_gates_body,
        out_shape=jax.ShapeDtypeStruct((b, 3 * nh, s_len), F32),
        grid=(b,),
        in_specs=[pl.BlockSpec((2 * nh, s_len), lambda i: (0, i)),
                  pl.BlockSpec((nh, 1), lambda i: (0, 0)),
                  pl.BlockSpec((nh, 1), lambda i: (0, 0))],
        out_specs=pl.BlockSpec((None, 3 * nh, s_len), lambda i: (i, 0, 0)),
        compiler_params=_params(1),
        name="gdn_gates",
    )(raw, a_log.reshape(nh, 1), dt_bias.reshape(nh, 1))


def _gdn_prep_body(x_ref, w_ref, o_ref):
    s_len = x_ref.shape[0]
    j = pl.program_id(1)
    x = x_ref[...]
    t = lax.broadcasted_iota(jnp.int32, x.shape, 0)
    prev = jnp.where(t == 0, 0.0, pltpu.roll(x, 1, 0))
    nxt = jnp.where(t == s_len - 1, 0.0, pltpu.roll(x, s_len - 1, 0))
    conv = prev * w_ref[0:1, :] + x * w_ref[1:2, :] + nxt * w_ref[2:3, :]
    act = jax.nn.silu(conv)
    inv = lax.rsqrt(jnp.sum(act * act, axis=-1, keepdims=True) + EPS)
    is_q = j < GDN_HEADS
    is_v = j >= 2 * GDN_HEADS
    scale = jnp.where(is_v, 1.0, jnp.where(is_q, inv * (HEAD_DIM ** -0.5), inv))
    o_ref[...] = act * scale


def gdn_prep(proj, conv_w, b, s_len):
    hd = HEAD_DIM
    nblk = 3 * GDN_HEADS
    first = 4 * RET_HEADS
    return pl.pallas_call(
        _gdn_prep_body,
        out_shape=jax.ShapeDtypeStruct((b, s_len, nblk * hd), F32),
        grid=(b, nblk),
        in_specs=[pl.BlockSpec((None, s_len, hd), lambda i, j: (i, 0, first + j)),
                  pl.BlockSpec((3, hd), lambda i, j: (0, j))],
        out_specs=pl.BlockSpec((None, s_len, hd), lambda i, j: (i, 0, j)),
        compiler_params=_params(2),
        name="gdn_prep",
    )(proj, conv_w)


def _lockstep(gens):
    results = [None] * len(gens)
    live = list(range(len(gens)))
    while live:
        for i in list(live):
            try:
                next(gens[i])
            except StopIteration as done:
                results[i] = done.value
                live.remove(i)
    return results


def _tri_inverse(m, ri, ci):
    blk16 = (ri >> 4) == (ci >> 4)
    blk32 = (ri >> 5) == (ci >> 5)
    eye = (ri == ci).astype(F32)
    nm = _split(jnp.where(blk16, -m, 0.0))
    p = eye + jnp.where(blk16, -m, 0.0)
    for _ in range(3):
        nm = _split(_dot3(nm, nm))
        yield
        p = p + _dot3(_split(p), nm)
        yield
    for off in (jnp.where(blk32 & jnp.logical_not(blk16), m, 0.0), jnp.where(blk32, 0.0, m)):
        ps = _split(p)
        pc = _split(_dot3(ps, _split(off)))
        yield
        p = p - _dot3(pc, ps)
        yield
    return p


def _gdn_super_chunk(q, k, v, gcol, grow, bcol, glcol, st, rev):
    r = GDN_SUPER
    c = GDN_CHUNK
    ri = lax.broadcasted_iota(jnp.int32, (r, r), 0)
    ci = lax.broadcasted_iota(jnp.int32, (r, r), 1)
    same = (ri >> 6) == (ci >> 6)
    if rev:
        incl = same & (ri <= ci)
        strict = same & (ri < ci)
    else:
        incl = same & (ri >= ci)
        strict = same & (ri > ci)
    decay = jnp.exp(jnp.where(incl, gcol - grow, -jnp.inf))
    kb = k * bcol
    vb = v * bcol
    k16 = k.astype(BF16)
    m = jnp.where(strict, _dot_nt(kb.astype(BF16), k16) * decay, 0.0)
    qk = jnp.where(incl, _dot_nt(q.astype(BF16), k16) * decay, 0.0).astype(BF16)
    yield
    t = yield from _tri_inverse(m, ri, ci)
    eg = jnp.exp(gcol)
    rhs = jnp.concatenate([vb, kb * eg], axis=1).astype(BF16)
    uw = _dot(t.astype(BF16), rhs)
    u = uw[:, :HEAD_DIM]
    w = uw[:, HEAD_DIM:].astype(BF16)
    qd = (q * eg).astype(BF16)
    kt = (k * jnp.exp(glcol - gcol)).astype(BF16)
    yield
    outs = [None] * (r // c)
    order = range(r // c - 1, -1, -1) if rev else range(r // c)
    for i in order:
        sl = slice(i * c, (i + 1) * c)
        s16 = st.astype(BF16)
        vn = u[sl] - _dot(w[sl], s16)
        vn16 = vn.astype(BF16)
        yield
        pair = slice((i // 2) * 2 * c, (i // 2 + 1) * 2 * c)
        zero = jnp.zeros_like(vn16)
        vpad = jnp.concatenate([zero, vn16] if i % 2 else [vn16, zero], axis=0)
        outs[i] = _dot(qd[sl], s16) + _dot(qk[sl, pair], vpad)
        st = st * jnp.exp(glcol[i * c:i * c + 1, :]) + _dot_tn(kt[sl], vn16)
        yield
    return jnp.concatenate(outs, axis=0), st


def _gdn_core_body(q_ref, k_ref, v_ref, z_ref, col_ref, row_ref, gn_ref, o_ref, of_s, ob_s):
    s_len = q_ref.shape[0]
    hd = HEAD_DIM
    heads = q_ref.shape[1] // hd
    r = GDN_SUPER
    n = s_len // r

    def one(hh, i, st, rev):
        base = 3 if rev else 0
        rows = pl.ds(pl.multiple_of(i * r, r), r)
        lanes = slice(hh * hd, (hh + 1) * hd)
        cols = col_ref[hh, rows, :]
        gcol = cols[:, base:base + 1]
        bcol = cols[:, base + 1:base + 2]
        glcol = cols[:, base + 2:base + 3]
        grow = row_ref[hh, i][base:base + 1, :]
        return _gdn_super_chunk(q_ref[rows, lanes], k_ref[rows, lanes], v_ref[rows, lanes],
                                gcol, grow, bcol, glcol, st, rev)

    def step(t, carry):
        fwd_rows = pl.ds(pl.multiple_of(t * r, r), r)
        bwd_rows = pl.ds(pl.multiple_of((n - 1 - t) * r, r), r)
        gens = []
        for hh in range(heads):
            gens.append(one(hh, t, carry[2 * hh], False))
            gens.append(one(hh, n - 1 - t, carry[2 * hh + 1], True))
        res = _lockstep(gens)
        for hh in range(heads):
            lanes = slice(hh * hd, (hh + 1) * hd)
            of_s[fwd_rows, lanes] = res[2 * hh][0]
            ob_s[bwd_rows, lanes] = res[2 * hh + 1][0]
        return tuple(st for _, st in res)

    zero = jnp.zeros((hd, hd), F32)
    lax.fori_loop(0, n, step, (zero,) * (2 * heads))
    for hh in range(heads):
        lanes = slice(hh * hd, (hh + 1) * hd)
        o = of_s[:, lanes] + ob_s[:, lanes]
        y = o * lax.rsqrt(jnp.mean(o * o, axis=-1, keepdims=True) + EPS) * gn_ref[...]
        o_ref[:, lanes] = (y * jax.nn.silu(z_ref[:, lanes])).astype(o_ref.dtype)


def gdn_core(qkv, proj, gate_col, gate_row, gdn_norm, b, s_len, *, heads):
    hd = HEAD_DIM
    w = heads * hd
    ng = GDN_HEADS // heads
    z_first = (4 * RET_HEADS + 3 * GDN_HEADS) // heads
    qkv_blk = lambda off: pl.BlockSpec((None, s_len, w), lambda i, h: (i, 0, off + h))
    return pl.pallas_call(
        _gdn_core_body,
        out_shape=jax.ShapeDtypeStruct((b, s_len, GDN_HEADS * hd), BF16),
        grid=(b, ng),
        in_specs=[qkv_blk(0), qkv_blk(ng), qkv_blk(2 * ng),
                  pl.BlockSpec((None, s_len, w), lambda i, h: (i, 0, z_first + h)),
                  pl.BlockSpec((None, heads, s_len, 8), lambda i, h: (i, h, 0, 0)),
                  pl.BlockSpec((None, heads, s_len // GDN_SUPER, 8, GDN_SUPER),
                               lambda i, h: (i, h, 0, 0, 0)),
                  pl.BlockSpec((1, hd), lambda i, h: (0, 0))],
        out_specs=pl.BlockSpec((None, s_len, w), lambda i, h: (i, 0, h)),
        scratch_shapes=[pltpu.VMEM((s_len, w), F32)] * 2,
        compiler_params=_params(2),
        name="gdn_core",
    )(qkv, qkv, qkv, proj, gate_col, gate_row, gdn_norm.reshape(1, hd))


def _xattn_body(x_ref, g_ref, wq_ref, kv_ref, wo_ref, o_ref):
    x = x_ref[...]
    d = x.shape[1]
    dh = d // XA_HEADS
    q = _dot(_rms(x, g_ref[...]).astype(BF16), wq_ref[...])
    outs = []
    for h in range(XA_HEADS):
        kh = kv_ref[:, h * dh:(h + 1) * dh].astype(BF16)
        vh = kv_ref[:, d + h * dh:d + (h + 1) * dh].astype(BF16)
        sc = _dot_nt(q[:, h * dh:(h + 1) * dh].astype(BF16), kh) * (dh ** -0.5)
        p = jax.nn.softmax(sc, axis=-1)
        outs.append(_dot(p.astype(BF16), vh).astype(BF16))
    o = jnp.concatenate(outs, axis=1)
    o_ref[...] = x + _dot(o, wo_ref[...])


def xattn_residual(x, g, wq, kv, wo, b, s_len, *, ts):
    t, d = x.shape
    m = kv.shape[1]
    nblk = s_len // ts
    return pl.pallas_call(
        _xattn_body,
        out_shape=jax.ShapeDtypeStruct((t, d), F32),
        grid=(b, nblk),
        in_specs=[pl.BlockSpec((ts, d), lambda i, j: (i * nblk + j, 0)),
                  pl.BlockSpec((1, d), lambda i, j: (0, 0)),
                  pl.BlockSpec((d, d), lambda i, j: (0, 0)),
                  pl.BlockSpec((None, m, 2 * d), lambda i, j: (i, 0, 0)),
                  pl.BlockSpec((d, d), lambda i, j: (0, 0))],
        out_specs=pl.BlockSpec((ts, d), lambda i, j: (i * nblk + j, 0)),
        compiler_params=_params(2),
        name="xattn_residual",
    )(x, g.reshape(1, d), wq, kv, wo)


def _extract_topk(s, n_rows, k):
    iota = lax.broadcasted_iota(jnp.int32, s.shape, 0).astype(F32)
    rank = jnp.full(s.shape, float(k), F32)
    vals = []
    for r in range(k):
        m = jnp.max(s, axis=0, keepdims=True)
        idx = jnp.min(jnp.where(s == m, iota, float(n_rows)), axis=0, keepdims=True)
        hit = iota == idx
        rank = jnp.where(hit, float(r), rank)
        s = jnp.where(hit, -jnp.inf, s)
        vals.append(m)
    return rank, vals


def _peer_route_body(q_ref, keys_ref, c1_ref, n1_ref, e2_ref, r2_ref):
    tb = q_ref.shape[0]
    nk = PEER_NKEYS
    kk = PEER_TOPK
    for blk in range(tb // LANES):
        tok = slice(blk * LANES, (blk + 1) * LANES)
        q = q_ref[tok, :].astype(BF16)
        s1 = _dot_nt(keys_ref[0], q[:, :nk])
        s2 = _dot_nt(keys_ref[1], q[:, nk:])
        r1, v1 = _extract_topk(s1, nk, kk)
        r2, v2 = _extract_topk(s2, nk, kk)
        v1m = jnp.concatenate(v1, axis=0)
        v2m = jnp.concatenate(v2, axis=0)
        row8 = lax.broadcasted_iota(jnp.int32, (8, LANES), 0)
        groups = [v1[0] + v2m[0:8], v1[0] + v2m[8:16]]
        for a in range(1, 8):
            groups.append(jnp.where(row8 < kk // (a + 1), v1[a] + v2m[0:8], -jnp.inf))
        groups.append(v1m[8:16] + v2[0])
        cand = jnp.concatenate(groups, axis=0)
        rc, _ = _extract_topk(cand, cand.shape[0], kk)
        sel = rc < float(kk)
        top = v1[0] + v2[0]
        z = jnp.sum(jnp.where(sel, jnp.exp(cand - top), 0.0), axis=0, keepdims=True)
        sel_f = jnp.where(sel, 1.0, 0.0)
        n1 = jnp.zeros((nk, LANES), F32)
        for a in range(kk):
            if a == 0:
                cnt = jnp.sum(sel_f[0:16], axis=0, keepdims=True)
            elif a < 8:
                cnt = jnp.sum(sel_f[8 + 8 * a:16 + 8 * a], axis=0, keepdims=True)
            else:
                cnt = sel_f[64 + a:65 + a]
            n1 = jnp.where(r1 == float(a), cnt, n1)
        c1_ref[:, tok] = jnp.where(r1 < float(kk), jnp.exp(s1 - v1[0]) / z, 0.0)
        n1_ref[:, tok] = n1
        e2_ref[:, tok] = jnp.exp(s2 - v2[0]).astype(e2_ref.dtype)
        r2_ref[:, tok] = r2.astype(r2_ref.dtype)


def peer_route(q, keys, *, tb):
    t = q.shape[0]
    nk = PEER_NKEYS
    out = lambda dt: jax.ShapeDtypeStruct((PEER_HEADS, nk, t), dt)
    ospec = pl.BlockSpec((None, nk, tb), lambda i, h: (h, 0, i))
    return pl.pallas_call(
        _peer_route_body,
        out_shape=(out(F32), out(F32), out(BF16), out(BF16)),
        grid=(t // tb, PEER_HEADS),
        in_specs=[pl.BlockSpec((tb, 2 * nk), lambda i, h: (i, h)),
                  pl.BlockSpec((None, 2, nk, nk), lambda i, h: (h, 0, 0, 0))],
        out_specs=(ospec,) * 4,
        compiler_params=_params(2),
        name="peer_route",
    )(q, keys)


def _gelu_tanh(x):
    c0 = 0.7978845608028654
    c1 = 0.7978845608028654 * 0.044715
    return (0.5 * x) * (1.0 + jnp.tanh(x * (c0 + c1 * (x * x))))


def _peer_dense_body(x_ref, g_ref, c1_ref, n1_ref, e2_ref, r2_ref, u_ref, vt_ref, *rest, final):
    if final:
        gf_ref, o_ref, xn_s, st_s, w_s, acc_s = rest
    else:
        o_ref, xn_s, st_s, w_s, acc_s = rest
    j = pl.program_id(1)
    nk = PEER_NKEYS
    tb = x_ref.shape[0]
    grp = u_ref.shape[0] // nk
    sub = e2_ref.shape[2]

    @pl.when(j == 0)
    def _():
        xn_s[...] = _rms(x_ref[...], g_ref[...]).astype(BF16)
        acc_s[...] = jnp.zeros_like(acc_s)

    st_s[...] = _dot_nt(u_ref[...], xn_s[...])
    rows = pl.ds(pl.multiple_of(j * grp, grp), grp)
    c1_t = [c1_ref[h, rows, :] for h in range(PEER_HEADS)]
    n1_t = [n1_ref[h, rows, :] for h in range(PEER_HEADS)]
    for ii in range(grp):
        gate = jnp.zeros((nk // sub, sub, tb), e2_ref.dtype)
        for h in range(PEER_HEADS):
            c1 = jnp.broadcast_to(c1_t[h][ii:ii + 1, :], (sub, tb)).astype(gate.dtype)[None]
            n1 = jnp.broadcast_to(n1_t[h][ii:ii + 1, :], (sub, tb)).astype(gate.dtype)[None]
            e2 = e2_ref[h]
            gate = gate + jnp.where(r2_ref[h] < n1, e2, jnp.zeros_like(e2)) * c1
        act = _gelu_tanh(st_s[ii * nk:(ii + 1) * nk, :]).astype(gate.dtype)
        w_s[ii * nk:(ii + 1) * nk, :] = act * gate.reshape(nk, tb)
    acc_s[...] += _dot(vt_ref[...], w_s[...])

    @pl.when(j == pl.num_programs(1) - 1)
    def _():
        y = x_ref[...] + acc_s[...].T
        if final:
            y = _rms(y, gf_ref[...])
        o_ref[...] = y


def peer_dense_residual(x, g, route, u, vt, g_final, *, tb, grp):
    t, d = x.shape
    nk = PEER_NKEYS
    eb = grp * nk
    c1, n1, e2, r2 = route
    sub = 32 // jnp.dtype(e2.dtype).itemsize
    e2 = e2.reshape(PEER_HEADS, nk // sub, sub, t)
    r2 = r2.reshape(PEER_HEADS, nk // sub, sub, t)
    rspec = pl.BlockSpec((PEER_HEADS, nk, tb), lambda i, j: (0, 0, i))
    pspec = pl.BlockSpec((PEER_HEADS, nk // sub, sub, tb), lambda i, j: (0, 0, 0, i))
    in_specs = [pl.BlockSpec((tb, d), lambda i, j: (i, 0)),
                pl.BlockSpec((1, d), lambda i, j: (0, 0)),
                rspec, rspec, pspec, pspec,
                pl.BlockSpec((eb, d), lambda i, j: (j, 0)),
                pl.BlockSpec((d, eb), lambda i, j: (0, j))]
    args = [x, g.reshape(1, d), c1, n1, e2, r2, u, vt]
    final = g_final is not None
    if final:
        in_specs.append(pl.BlockSpec((1, d), lambda i, j: (0, 0)))
        args.append(g_final.reshape(1, d))
    return pl.pallas_call(
        functools.partial(_peer_dense_body, final=final),
        out_shape=jax.ShapeDtypeStruct((t, d), F32),
        grid=(t // tb, (nk * nk) // eb),
        in_specs=in_specs,
        out_specs=pl.BlockSpec((tb, d), lambda i, j: (i, 0)),
        scratch_shapes=[pltpu.VMEM((tb, d), BF16),
                        pltpu.VMEM((eb, tb), F32),
                        pltpu.VMEM((eb, tb), e2.dtype),
                        pltpu.VMEM((d, tb), F32)],
        compiler_params=_params(2),
        name="peer_dense_final" if final else "peer_dense",
    )(*args)


def _pick(n, prefs):
    for p in prefs:
        if n % p == 0:
            return p
    return n


def _encoder(x3, mem3, norm_mix, w_in, ret_norm, gdn_conv, gdn_a_log, gdn_dt_bias, gdn_norm, w_out,
             norm_xa, norm_mem, w_xq, w_xkv, w_xo, norm_ffn, peer_wq, peer_keys, peer_u, peer_v,
             norm_final):
    b, s_len, d = x3.shape
    n_mem = mem3.shape[1]
    t = b * s_len
    depth = w_in.shape[0]
    x = x3.reshape(t, d)
    mem = mem3.reshape(b * n_mem, d)
    n_main = 4 * RET_HEADS * HEAD_DIM + 4 * GDN_HEADS * HEAD_DIM
    tm = _pick(t, (1024, 512, 256, 128))
    tmm = _pick(b * n_mem, (1024, 512, 256, 128))
    ts = _pick(s_len, (512, 256, 128))
    tb_route = _pick(t, (256, 128))
    tb_dense = _pick(t, (512, 256, 128))
    nh = 2 * GDN_HEADS

    for l in range(depth):
        w_main = w_in[l, :, :n_main].astype(BF16)
        w_gate_t = w_in[l, :, n_main:].T.astype(BF16)
        proj, graw = norm_matmul_side(x, norm_mix[l], w_main, w_gate_t, tm=tm, tn=1024)
        proj3 = proj.reshape(b, s_len, n_main)
        o_r = retention(proj3, ret_norm[l], b, s_len)
        gates = gdn_gates(graw, gdn_a_log[l], gdn_dt_bias[l], b, s_len)
        gates = gates.reshape(b, 3, 2, GDN_HEADS, s_len).transpose(0, 3, 2, 1, 4)
        gate_row = jnp.pad(gates.reshape(b, GDN_HEADS, 6, s_len), ((0, 0), (0, 0), (0, 2), (0, 0)))
        gate_col = jnp.swapaxes(gate_row, 2, 3)
        gate_row = gate_row.reshape(b, GDN_HEADS, 8, s_len // GDN_SUPER, GDN_SUPER).transpose(0, 1, 3, 2, 4)
        qkv = gdn_prep(proj3, gdn_conv[l], b, s_len)
        o_g = gdn_core(qkv, proj3, gate_col, gate_row, gdn_norm[l], b, s_len, heads=GDN_HEADS_PER_STEP)
        w_o = w_out[l].astype(BF16)
        n_r = RET_HEADS * HEAD_DIM
        x = out_proj_residual(x, o_r.reshape(t, -1), o_g.reshape(t, -1), w_o[:n_r], w_o[n_r:], tm=tm)

        kv = norm_matmul(mem, norm_mem[l], w_xkv[l].astype(BF16), tm=tmm, tn=1024)
        x = xattn_residual(x, norm_xa[l], w_xq[l].astype(BF16), kv.reshape(b, n_mem, 2 * d),
                           w_xo[l].astype(BF16), b, s_len, ts=ts)

        pq = norm_matmul(x, norm_ffn[l], peer_wq[l].astype(BF16), tm=tm, tn=1024)
        route = peer_route(pq, peer_keys[l].astype(BF16), tb=tb_route)
        x = peer_dense_residual(x, norm_ffn[l], route, peer_u[l].astype(BF16),
                                peer_v[l].astype(BF16).T,
                                norm_final if l == depth - 1 else None, tb=tb_dense, grp=8)
    return x.reshape(b, s_len, d)


def kernel(x_prompt, x_sample, mem_prompt, mem_sample, norm_mix, w_in, ret_norm, gdn_conv, gdn_a_log,
           gdn_dt_bias, gdn_norm, w_out, norm_xa, norm_mem, w_xq, w_xkv, w_xo, norm_ffn, peer_wq,
           peer_keys, peer_u, peer_v, norm_final):
    weights = (norm_mix, w_in, ret_norm, gdn_conv, gdn_a_log, gdn_dt_bias, gdn_norm, w_out,
               norm_xa, norm_mem, w_xq, w_xkv, w_xo, norm_ffn, peer_wq, peer_keys, peer_u, peer_v,
               norm_final)
    if x_prompt.shape[1:] == x_sample.shape[1:] and mem_prompt.shape[1:] == mem_sample.shape[1:]:
        nb = x_prompt.shape[0]
        y = _encoder(jnp.concatenate([x_prompt, x_sample], axis=0),
                     jnp.concatenate([mem_prompt, mem_sample], axis=0), *weights)
        return (y[:nb], y[nb:])
    return (_encoder(x_prompt, mem_prompt, *weights), _encoder(x_sample, mem_sample, *weights))
```

```python
import functools

import jax
import jax.numpy as jnp
from jax import lax
from jax.experimental import pallas as pl
from jax.experimental.pallas import tpu as pltpu

F32 = jnp.float32
BF16 = jnp.bfloat16

EPS = 1e-6
LANES = 128
MXU_DEPTH = 256
HEAD_DIM = 128
RET_HEADS = 4
GDN_HEADS = 4
RET_CHUNK = 128
GDN_CHUNK = 64
GDN_SUPER = 256
GDN_HEADS_PER_STEP = 2
RET_HEADS_PER_STEP = 2
ROPE_BASE = 10000.0
XA_HEADS = 4
PEER_HEADS = 8
PEER_NKEYS = 128
PEER_TOPK = 16
VMEM_LIMIT = 56 * 1024 * 1024

_NT = (((1,), (1,)), ((), ()))
_TN = (((0,), (0,)), ((), ()))


def _params(n_axes):
    return pltpu.CompilerParams(
        dimension_semantics=("arbitrary",) * n_axes, vmem_limit_bytes=VMEM_LIMIT)


def _dot(a, b):
    return jnp.dot(a, b, preferred_element_type=F32)


def _dot_nt(a, b):
    return lax.dot_general(a, b, _NT, preferred_element_type=F32)


def _dot_tn(a, b):
    return lax.dot_general(a, b, _TN, preferred_element_type=F32)


def _split(a):
    hi = a.astype(BF16)
    return hi, (a - hi.astype(F32)).astype(BF16)


def _dot3(a, b):
    (ah, al), (bh, bl) = a, b
    return _dot(jnp.concatenate([ah, ah, al], axis=1), jnp.concatenate([bh, bl, bh], axis=0))


def _rms(x, g):
    return x * lax.rsqrt(jnp.mean(x * x, axis=-1, keepdims=True) + EPS) * g


def _norm_matmul_body(x_ref, g_ref, w_ref, o_ref, xn_ref):
    @pl.when(pl.program_id(1) == 0)
    def _():
        xn_ref[...] = _rms(x_ref[...], g_ref[...]).astype(BF16)

    o_ref[...] = _dot(xn_ref[...], w_ref[...]).astype(o_ref.dtype)


def norm_matmul(x, g, w, *, tm, tn, out_dtype=F32):
    t, d = x.shape
    n = w.shape[1]
    return pl.pallas_call(
        _norm_matmul_body,
        out_shape=jax.ShapeDtypeStruct((t, n), out_dtype),
        grid=(t // tm, n // tn),
        in_specs=[pl.BlockSpec((tm, d), lambda i, j: (i, 0)),
                  pl.BlockSpec((1, d), lambda i, j: (0, 0)),
                  pl.BlockSpec((d, tn), lambda i, j: (0, j))],
        out_specs=pl.BlockSpec((tm, tn), lambda i, j: (i, j)),
        scratch_shapes=[pltpu.VMEM((tm, d), BF16)],
        compiler_params=_params(2),
        name="norm_matmul",
    )(x, g.reshape(1, d), w)


def _norm_matmul_side_body(x_ref, g_ref, w_ref, ws_ref, o_ref, os_ref, xn_ref):
    @pl.when(pl.program_id(1) == 0)
    def _():
        xn = _rms(x_ref[...], g_ref[...]).astype(BF16)
        xn_ref[...] = xn
        os_ref[...] = _dot_nt(ws_ref[...], xn)

    o_ref[...] = _dot(xn_ref[...], w_ref[...])


def norm_matmul_side(x, g, w, ws_t, *, tm, tn):
    t, d = x.shape
    n = w.shape[1]
    rows = ws_t.shape[0]
    return pl.pallas_call(
        _norm_matmul_side_body,
        out_shape=(jax.ShapeDtypeStruct((t, n), F32),
                   jax.ShapeDtypeStruct((rows, t), F32)),
        grid=(t // tm, n // tn),
        in_specs=[pl.BlockSpec((tm, d), lambda i, j: (i, 0)),
                  pl.BlockSpec((1, d), lambda i, j: (0, 0)),
                  pl.BlockSpec((d, tn), lambda i, j: (0, j)),
                  pl.BlockSpec((rows, d), lambda i, j: (0, 0))],
        out_specs=(pl.BlockSpec((tm, tn), lambda i, j: (i, j)),
                   pl.BlockSpec((rows, tm), lambda i, j: (0, i))),
        scratch_shapes=[pltpu.VMEM((tm, d), BF16)],
        compiler_params=_params(2),
        name="norm_matmul_side",
    )(x, g.reshape(1, d), w, ws_t)


def _out_proj_body(x_ref, a1_ref, a2_ref, w1_ref, w2_ref, o_ref):
    o_ref[...] = (x_ref[...] + _dot(a1_ref[...], w1_ref[...])
                  + _dot(a2_ref[...], w2_ref[...]))


def out_proj_residual(x, a1, a2, w1, w2, *, tm):
    t, d = x.shape
    k1, k2 = a1.shape[1], a2.shape[1]
    return pl.pallas_call(
        _out_proj_body,
        out_shape=jax.ShapeDtypeStruct((t, d), F32),
        grid=(t // tm,),
        in_specs=[pl.BlockSpec((tm, d), lambda i: (i, 0)),
                  pl.BlockSpec((tm, k1), lambda i: (i, 0)),
                  pl.BlockSpec((tm, k2), lambda i: (i, 0)),
                  pl.BlockSpec((k1, d), lambda i: (0, 0)),
                  pl.BlockSpec((k2, d), lambda i: (0, 0))],
        out_specs=pl.BlockSpec((tm, d), lambda i: (i, 0)),
        compiler_params=_params(1),
        name="out_proj_residual",
    )(x, a1, a2, w1, w2)


def _retention_body(q_ref, k_ref, v_ref, g_ref, cos_ref, sin_ref, dec_ref, rn_ref,
                    o_ref, qr_s, kr_s, of_s, ob_s):
    s_len = q_ref.shape[0]
    hd = HEAD_DIM
    heads = q_ref.shape[1] // hd
    c = RET_CHUNK
    n = s_len // c
    cos = cos_ref[...]
    sin = sin_ref[...]

    def rot(x):
        return x * cos + pltpu.roll(x, hd // 2, 1) * sin

    for hh in range(heads):
        lanes = slice(hh * hd, (hh + 1) * hd)
        qr_s[:, lanes] = rot(q_ref[:, lanes])
        kr_s[:, lanes] = rot(k_ref[:, lanes]) * (hd ** -0.5)

    def forward(hh, i, st):
        r = pl.ds(pl.multiple_of(i * c, c), c)
        lanes = slice(hh * hd, (hh + 1) * hd)
        qc, kc, vb = qr_s[r, lanes], kr_s[r, lanes], v_ref[r, lanes].astype(BF16)
        sc = _dot_nt(qc.astype(BF16), kc.astype(BF16)) * dec_ref[hh, 0]
        inter = _dot((qc * dec_ref[hh, 1]).astype(BF16), st.astype(BF16))
        st = st * dec_ref[hh, 5] + _dot_tn((kc * dec_ref[hh, 2]).astype(BF16), vb)
        yield
        return inter + _dot(sc.astype(BF16), vb), st

    def backward(hh, i, st):
        r = pl.ds(pl.multiple_of(i * c, c), c)
        lanes = slice(hh * hd, (hh + 1) * hd)
        qc, kc, vb = qr_s[r, lanes], kr_s[r, lanes], v_ref[r, lanes].astype(BF16)
        inter = _dot((qc * dec_ref[hh, 3]).astype(BF16), st.astype(BF16))
        st = st * dec_ref[hh, 6] + _dot_tn((kc * dec_ref[hh, 4]).astype(BF16), vb)
        yield
        return inter, st

    def step(t, carry):
        fwd_rows = pl.ds(pl.multiple_of(t * c, c), c)
        bwd_rows = pl.ds(pl.multiple_of((n - 1 - t) * c, c), c)
        gens = []
        for hh in range(heads):
            gens.append(forward(hh, t, carry[2 * hh]))
            gens.append(backward(hh, n - 1 - t, carry[2 * hh + 1]))
        res = _lockstep(gens)
        for hh in range(heads):
            lanes = slice(hh * hd, (hh + 1) * hd)
            of_s[fwd_rows, lanes] = res[2 * hh][0]
            ob_s[bwd_rows, lanes] = res[2 * hh + 1][0]
        return tuple(st for _, st in res)

    zero = jnp.zeros((hd, hd), F32)
    lax.fori_loop(0, n, step, (zero,) * (2 * heads))

    for hh in range(heads):
        lanes = slice(hh * hd, (hh + 1) * hd)
        o = of_s[:, lanes] + ob_s[:, lanes]
        mu = jnp.mean(o, axis=-1, keepdims=True)
        var = jnp.mean(jnp.square(o - mu), axis=-1, keepdims=True)
        y = (o - mu) * lax.rsqrt(var + EPS)
        y = y * rn_ref[:, lanes] * jax.nn.silu(g_ref[:, lanes])
        o_ref[:, lanes] = y.astype(o_ref.dtype)


def _deinterleave_rotary_columns(w):
    n_qk = 2 * RET_HEADS * HEAD_DIM
    qk = w[:, :n_qk].reshape(w.shape[0], 2 * RET_HEADS, HEAD_DIM // 2, 2)
    qk = jnp.swapaxes(qk, 2, 3).reshape(w.shape[0], n_qk)
    return jnp.concatenate([qk, w[:, n_qk:]], axis=1)


def _retention_tables(s_len):
    d = HEAD_DIM
    inv = ROPE_BASE ** (-jnp.arange(0, d, 2, dtype=F32) / d)
    ang = jnp.arange(s_len, dtype=F32)[:, None] * inv[None, :]
    cos = jnp.concatenate([jnp.cos(ang), jnp.cos(ang)], axis=1)
    sin = jnp.concatenate([-jnp.sin(ang), jnp.sin(ang)], axis=1)
    h = jnp.arange(RET_HEADS, dtype=F32)
    lg_f = jnp.log1p(-jnp.exp2(-5.0 - h))[:, None, None]
    lg_b = jnp.log1p(-jnp.exp2(-5.5 - h))[:, None, None]
    c = RET_CHUNK
    pos = jnp.arange(c, dtype=F32)
    diff = (pos[:, None] - pos[None, :])[None]
    dmat = jnp.where(diff >= 0, jnp.exp(lg_f * jnp.where(diff >= 0, diff, 0.0)),
                     jnp.exp(lg_b * jnp.where(diff < 0, -diff, 0.0)))
    col = lambda v: jnp.broadcast_to(v, (RET_HEADS, c, d))
    p = pos[None, :, None]
    dec = jnp.stack([
        dmat,
        col(jnp.exp(lg_f * (p + 1.0))),
        col(jnp.exp(lg_f * (c - 1.0 - p))),
        col(jnp.exp(lg_b * (c - p))),
        col(jnp.exp(lg_b * p)),
        col(jnp.exp(lg_f * c)),
        col(jnp.exp(lg_b * c)),
    ], axis=1)
    return cos, sin, dec


def retention(proj, ret_norm, b, s_len, *, heads):
    cos, sin, dec = _retention_tables(s_len)
    hd = HEAD_DIM
    w = heads * hd
    ng = RET_HEADS // heads
    head_blk = lambda off: pl.BlockSpec((None, s_len, w), lambda i, h: (i, 0, off + h))
    return pl.pallas_call(
        _retention_body,
        out_shape=jax.ShapeDtypeStruct((b, s_len, RET_HEADS * hd), BF16),
        grid=(b, ng),
        in_specs=[head_blk(0), head_blk(ng), head_blk(2 * ng), head_blk(3 * ng),
                  pl.BlockSpec((s_len, hd), lambda i, h: (0, 0)),
                  pl.BlockSpec((s_len, hd), lambda i, h: (0, 0)),
                  pl.BlockSpec((heads, 7, RET_CHUNK, hd), lambda i, h: (h, 0, 0, 0)),
                  pl.BlockSpec((1, w), lambda i, h: (0, h))],
        out_specs=pl.BlockSpec((None, s_len, w), lambda i, h: (i, 0, h)),
        scratch_shapes=[pltpu.VMEM((s_len, w), F32)] * 4,
        compiler_params=_params(2),
        name="retention",
    )(proj, proj, proj, proj, cos, sin, dec, ret_norm.reshape(1, -1))


def _gdn_gates_body(raw_ref, al_ref, dt_ref, o_ref):
    s_len = raw_ref.shape[-1]
    nh = 2 * GDN_HEADS
    gb = raw_ref[0:nh, :]
    ga = raw_ref[nh:2 * nh, :]
    beta = jax.nn.sigmoid(gb)
    g = -jnp.exp(al_ref[...]) * jax.nn.softplus(ga + dt_ref[...])
    pos = lax.broadcasted_iota(jnp.int32, (nh, s_len), 1) & (GDN_CHUNK - 1)
    fwd = g
    rev = g
    k = 1
    while k < GDN_CHUNK:
        fwd = fwd + jnp.where(pos >= k, pltpu.roll(fwd, k, 1), 0.0)
        rev = rev + jnp.where(pos < GDN_CHUNK - k, pltpu.roll(rev, s_len - k, 1), 0.0)
        k *= 2
    row = lax.broadcasted_iota(jnp.int32, (nh, s_len), 0)
    o_ref[0:nh, :] = jnp.where(row < GDN_HEADS, fwd, rev)
    o_ref[nh:2 * nh, :] = beta
    o_ref[2 * nh:3 * nh, :] = fwd + rev - g


def gdn_gates(raw, a_log, dt_bias, b, s_len):
    nh = 2 * GDN_HEADS
    return pl.pallas_call(
        _gdn_gates_body,
        out_shape=jax.ShapeDtypeStruct((b, 3 * nh, s_len), F32),
        grid=(b,),
        in_specs=[pl.BlockSpec((2 * nh, s_len), lambda i: (0, i)),
                  pl.BlockSpec((nh, 1), lambda i: (0, 0)),
                  pl.BlockSpec((nh, 1), lambda i: (0, 0))],
        out_specs=pl.BlockSpec((None, 3 * nh, s_len), lambda i: (i, 0, 0)),
        compiler_params=_params(1),
        name="gdn_gates",
    )(raw, a_log.reshape(nh, 1), dt_bias.reshape(nh, 1))


def _gdn_prep_body(x_ref, w_ref, o_ref):
    s_len = x_ref.shape[0]
    j = pl.program_id(1)
    x = x_ref[...]
    t = lax.broadcasted_iota(jnp.int32, x.shape, 0)
    prev = jnp.where(t == 0, 0.0, pltpu.roll(x, 1, 0))
    nxt = jnp.where(t == s_len - 1, 0.0, pltpu.roll(x, s_len - 1, 0))
    conv = prev * w_ref[0:1, :] + x * w_ref[1:2, :] + nxt * w_ref[2:3, :]
    act = jax.nn.silu(conv)
    inv = lax.rsqrt(jnp.sum(act * act, axis=-1, keepdims=True) + EPS)
    is_q = j < GDN_HEADS
    is_v = j >= 2 * GDN_HEADS
    scale = jnp.where(is_v, 1.0, jnp.where(is_q, inv * (HEAD_DIM ** -0.5), inv))
    o_ref[...] = act * scale


def gdn_prep(proj, conv_w, b, s_len):
    hd = HEAD_DIM
    nblk = 3 * GDN_HEADS
    first = 4 * RET_HEADS
    return pl.pallas_call(
        _gdn_prep_body,
        out_shape=jax.ShapeDtypeStruct((b, s_len, nblk * hd), F32),
        grid=(b, nblk),
        in_specs=[pl.BlockSpec((None, s_len, hd), lambda i, j: (i, 0, first + j)),
                  pl.BlockSpec((3, hd), lambda i, j: (0, j))],
        out_specs=pl.BlockSpec((None, s_len, hd), lambda i, j: (i, 0, j)),
        compiler_params=_params(2),
        name="gdn_prep",
    )(proj, conv_w)


def _lockstep(gens):
    results = [None] * len(gens)
    live = list(range(len(gens)))
    while live:
        for i in list(live):
            try:
                next(gens[i])
            except StopIteration as done:
                results[i] = done.value
                live.remove(i)
    return results


def _merge_run(a, b):
    ia = ib = 0
    while ia < len(a) or ib < len(b):
        if ib >= len(b) or (ia < len(a) and (ia + 1) * len(b) <= (ib + 1) * len(a)):
            a[ia]()
            ia += 1
        else:
            b[ib]()
            ib += 1


def _tri_inverse(m, ri, ci):
    blk16 = (ri >> 4) == (ci >> 4)
    blk32 = (ri >> 5) == (ci >> 5)
    eye = (ri == ci).astype(F32)
    nm = _split(jnp.where(blk16, -m, 0.0))
    p = eye + jnp.where(blk16, -m, 0.0)
    for _ in range(3):
        nm = _split(_dot3(nm, nm))
        yield
        p = p + _dot3(_split(p), nm)
        yield
    for off in (jnp.where(blk32 & jnp.logical_not(blk16), m, 0.0), jnp.where(blk32, 0.0, m)):
        ps = _split(p)
        pc = _split(_dot3(ps, _split(off)))
        yield
        p = p - _dot3(pc, ps)
        yield
    return p


def _gdn_super_chunk(q, k, v, gcol, grow, bcol, glcol, st, rev):
    r = GDN_SUPER
    c = GDN_CHUNK
    ri = lax.broadcasted_iota(jnp.int32, (r, r), 0)
    ci = lax.broadcasted_iota(jnp.int32, (r, r), 1)
    same = (ri >> 6) == (ci >> 6)
    if rev:
        incl = same & (ri <= ci)
        strict = same & (ri < ci)
    else:
        incl = same & (ri >= ci)
        strict = same & (ri > ci)
    decay = jnp.exp(jnp.where(incl, gcol - grow, -jnp.inf))
    kb = k * bcol
    vb = v * bcol
    k16 = k.astype(BF16)
    m = jnp.where(strict, _dot_nt(kb.astype(BF16), k16) * decay, 0.0)
    qk = jnp.where(incl, _dot_nt(q.astype(BF16), k16) * decay, 0.0).astype(BF16)
    yield
    t = yield from _tri_inverse(m, ri, ci)
    eg = jnp.exp(gcol)
    rhs = jnp.concatenate([vb, kb * eg], axis=1).astype(BF16)
    uw = _dot(t.astype(BF16), rhs)
    u = uw[:, :HEAD_DIM]
    w = uw[:, HEAD_DIM:].astype(BF16)
    qd = (q * eg).astype(BF16)
    kt = (k * jnp.exp(glcol - gcol)).astype(BF16)
    yield
    outs = [None] * (r // c)
    order = range(r // c - 1, -1, -1) if rev else range(r // c)
    for i in order:
        sl = slice(i * c, (i + 1) * c)
        s16 = st.astype(BF16)
        vn = u[sl] - _dot(w[sl], s16)
        vn16 = vn.astype(BF16)
        yield
        pair = slice((i // 2) * 2 * c, (i // 2 + 1) * 2 * c)
        zero = jnp.zeros_like(vn16)
        vpad = jnp.concatenate([zero, vn16] if i % 2 else [vn16, zero], axis=0)
        outs[i] = _dot(qd[sl], s16) + _dot(qk[sl, pair], vpad)
        st = st * jnp.exp(glcol[i * c:i * c + 1, :]) + _dot_tn(kt[sl], vn16)
        yield
    return jnp.concatenate(outs, axis=0), st


def _gdn_core_body(q_ref, k_ref, v_ref, z_ref, col_ref, row_ref, gn_ref, o_ref, of_s, ob_s):
    s_len = q_ref.shape[0]
    hd = HEAD_DIM
    heads = q_ref.shape[1] // hd
    r = GDN_SUPER
    n = s_len // r

    def one(hh, i, st, rev):
        base = 3 if rev else 0
        rows = pl.ds(pl.multiple_of(i * r, r), r)
        lanes = slice(hh * hd, (hh + 1) * hd)
        cols = col_ref[hh, rows, :]
        gcol = cols[:, base:base + 1]
        bcol = cols[:, base + 1:base + 2]
        glcol = cols[:, base + 2:base + 3]
        grow = row_ref[hh, i][base:base + 1, :]
        return _gdn_super_chunk(q_ref[rows, lanes], k_ref[rows, lanes], v_ref[rows, lanes],
                                gcol, grow, bcol, glcol, st, rev)

    def step(t, carry):
        fwd_rows = pl.ds(pl.multiple_of(t * r, r), r)
        bwd_rows = pl.ds(pl.multiple_of((n - 1 - t) * r, r), r)
        gens = []
        for hh in range(heads):
            gens.append(one(hh, t, carry[2 * hh], False))
            gens.append(one(hh, n - 1 - t, carry[2 * hh + 1], True))
        res = _lockstep(gens)
        for hh in range(heads):
            lanes = slice(hh * hd, (hh + 1) * hd)
            of_s[fwd_rows, lanes] = res[2 * hh][0]
            ob_s[bwd_rows, lanes] = res[2 * hh + 1][0]
        return tuple(st for _, st in res)

    zero = jnp.zeros((hd, hd), F32)
    lax.fori_loop(0, n, step, (zero,) * (2 * heads))
    for hh in range(heads):
        lanes = slice(hh * hd, (hh + 1) * hd)
        o = of_s[:, lanes] + ob_s[:, lanes]
        y = o * lax.rsqrt(jnp.mean(o * o, axis=-1, keepdims=True) + EPS) * gn_ref[...]
        o_ref[:, lanes] = (y * jax.nn.silu(z_ref[:, lanes])).astype(o_ref.dtype)


def gdn_core(qkv, proj, gate_col, gate_row, gdn_norm, b, s_len, *, heads):
    hd = HEAD_DIM
    w = heads * hd
    ng = GDN_HEADS // heads
    z_first = (4 * RET_HEADS + 3 * GDN_HEADS) // heads
    qkv_blk = lambda off: pl.BlockSpec((None, s_len, w), lambda i, h: (i, 0, off + h))
    return pl.pallas_call(
        _gdn_core_body,
        out_shape=jax.ShapeDtypeStruct((b, s_len, GDN_HEADS * hd), BF16),
        grid=(b, ng),
        in_specs=[qkv_blk(0), qkv_blk(ng), qkv_blk(2 * ng),
                  pl.BlockSpec((None, s_len, w), lambda i, h: (i, 0, z_first + h)),
                  pl.BlockSpec((None, heads, s_len, 8), lambda i, h: (i, h, 0, 0)),
                  pl.BlockSpec((None, heads, s_len // GDN_SUPER, 8, GDN_SUPER),
                               lambda i, h: (i, h, 0, 0, 0)),
                  pl.BlockSpec((1, hd), lambda i, h: (0, 0))],
        out_specs=pl.BlockSpec((None, s_len, w), lambda i, h: (i, 0, h)),
        scratch_shapes=[pltpu.VMEM((s_len, w), F32)] * 2,
        compiler_params=_params(2),
        name="gdn_core",
    )(qkv, qkv, qkv, proj, gate_col, gate_row, gdn_norm.reshape(1, hd))


def _xattn_body(x_ref, g_ref, wq_ref, kv_ref, wo_ref, o_ref):
    x = x_ref[...]
    d = x.shape[1]
    dh = d // XA_HEADS
    q = _dot(_rms(x, g_ref[...]).astype(BF16), wq_ref[...])
    outs = []
    for h in range(XA_HEADS):
        kh = kv_ref[:, h * dh:(h + 1) * dh].astype(BF16)
        vh = kv_ref[:, d + h * dh:d + (h + 1) * dh].astype(BF16)
        sc = _dot_nt(q[:, h * dh:(h + 1) * dh].astype(BF16), kh) * (dh ** -0.5)
        p = jax.nn.softmax(sc, axis=-1)
        outs.append(_dot(p.astype(BF16), vh).astype(BF16))
    o = jnp.concatenate(outs, axis=1)
    o_ref[...] = x + _dot(o, wo_ref[...])


def xattn_residual(x, g, wq, kv, wo, b, s_len, *, ts):
    t, d = x.shape
    m = kv.shape[1]
    nblk = s_len // ts
    return pl.pallas_call(
        _xattn_body,
        out_shape=jax.ShapeDtypeStruct((t, d), F32),
        grid=(b, nblk),
        in_specs=[pl.BlockSpec((ts, d), lambda i, j: (i * nblk + j, 0)),
                  pl.BlockSpec((1, d), lambda i, j: (0, 0)),
                  pl.BlockSpec((d, d), lambda i, j: (0, 0)),
                  pl.BlockSpec((None, m, 2 * d), lambda i, j: (i, 0, 0)),
                  pl.BlockSpec((d, d), lambda i, j: (0, 0))],
        out_specs=pl.BlockSpec((ts, d), lambda i, j: (i * nblk + j, 0)),
        compiler_params=_params(2),
        name="xattn_residual",
    )(x, g.reshape(1, d), wq, kv, wo)


def _extract_topk(s, n_rows, k):
    iota = lax.broadcasted_iota(jnp.int32, s.shape, 0).astype(F32)
    rank = jnp.full(s.shape, float(k), F32)
    vals = []
    for r in range(k):
        m = jnp.max(s, axis=0, keepdims=True)
        idx = jnp.min(jnp.where(s == m, iota, float(n_rows)), axis=0, keepdims=True)
        hit = iota == idx
        rank = jnp.where(hit, float(r), rank)
        s = jnp.where(hit, -jnp.inf, s)
        vals.append(m)
    return rank, vals


def _peer_route_body(q_ref, keys_ref, c1_ref, n1_ref, e2_ref, r2_ref):
    tb = q_ref.shape[0]
    nk = PEER_NKEYS
    kk = PEER_TOPK
    for blk in range(tb // LANES):
        tok = slice(blk * LANES, (blk + 1) * LANES)
        q = q_ref[tok, :].astype(BF16)
        s1 = _dot_nt(keys_ref[0], q[:, :nk])
        s2 = _dot_nt(keys_ref[1], q[:, nk:])
        r1, v1 = _extract_topk(s1, nk, kk)
        r2, v2 = _extract_topk(s2, nk, kk)
        v1m = jnp.concatenate(v1, axis=0)
        v2m = jnp.concatenate(v2, axis=0)
        row8 = lax.broadcasted_iota(jnp.int32, (8, LANES), 0)
        groups = [v1[0] + v2m[0:8], v1[0] + v2m[8:16]]
        for a in range(1, 8):
            groups.append(jnp.where(row8 < kk // (a + 1), v1[a] + v2m[0:8], -jnp.inf))
        groups.append(v1m[8:16] + v2[0])
        cand = jnp.concatenate(groups, axis=0)
        rc, _ = _extract_topk(cand, cand.shape[0], kk)
        sel = rc < float(kk)
        top = v1[0] + v2[0]
        z = jnp.sum(jnp.where(sel, jnp.exp(cand - top), 0.0), axis=0, keepdims=True)
        sel_f = jnp.where(sel, 1.0, 0.0)
        n1 = jnp.zeros((nk, LANES), F32)
        for a in range(kk):
            if a == 0:
                cnt = jnp.sum(sel_f[0:16], axis=0, keepdims=True)
            elif a < 8:
                cnt = jnp.sum(sel_f[8 + 8 * a:16 + 8 * a], axis=0, keepdims=True)
            else:
                cnt = sel_f[64 + a:65 + a]
            n1 = jnp.where(r1 == float(a), cnt, n1)
        c1_ref[:, tok] = jnp.where(r1 < float(kk), jnp.exp(s1 - v1[0]) / z, 0.0)
        n1_ref[:, tok] = n1
        e2_ref[:, tok] = jnp.exp(s2 - v2[0]).astype(e2_ref.dtype)
        r2_ref[:, tok] = r2.astype(r2_ref.dtype)


def peer_route(q, keys, *, tb):
    t = q.shape[0]
    nk = PEER_NKEYS
    out = lambda dt: jax.ShapeDtypeStruct((PEER_HEADS, nk, t), dt)
    ospec = pl.BlockSpec((None, nk, tb), lambda i, h: (h, 0, i))
    return pl.pallas_call(
        _peer_route_body,
        out_shape=(out(F32), out(F32), out(BF16), out(BF16)),
        grid=(t // tb, PEER_HEADS),
        in_specs=[pl.BlockSpec((tb, 2 * nk), lambda i, h: (i, h)),
                  pl.BlockSpec((None, 2, nk, nk), lambda i, h: (h, 0, 0, 0))],
        out_specs=(ospec,) * 4,
        compiler_params=_params(2),
        name="peer_route",
    )(q, keys)


def _gelu_tanh(x):
    c0 = 0.7978845608028654
    c1 = 0.7978845608028654 * 0.044715
    return (0.5 * x) * (1.0 + jnp.tanh(x * (c0 + c1 * (x * x))))


def _peer_dense_body(x_ref, g_ref, c1_ref, n1_ref, e2_ref, r2_ref, u_ref, vt_ref, *rest, final):
    if final:
        gf_ref, o_ref, xn_s, acc_s = rest
    else:
        o_ref, xn_s, acc_s = rest
    j = pl.program_id(1)
    nk = PEER_NKEYS
    tb = x_ref.shape[0]
    grp = u_ref.shape[0] // nk
    sub = e2_ref.shape[2]

    @pl.when(j == 0)
    def _():
        xn_s[...] = _rms(x_ref[...], g_ref[...]).T.astype(BF16)
        acc_s[...] = jnp.zeros_like(acc_s)

    per = MXU_DEPTH // nk
    n_chunks = grp // per
    d_model = vt_ref.shape[0]
    xn = xn_s[...]
    scores, gates, weights, chunk_w = {}, {}, {}, {}

    def score_pieces(c):
        def piece(ii):
            scores[ii] = _dot(u_ref[ii * nk:(ii + 1) * nk, :], xn)
        return [functools.partial(piece, c * per + i) for i in range(per)]

    def gate_pieces(c):
        def piece(ii, h):
            row = pl.ds(j * grp + ii, 1)
            c1 = jnp.broadcast_to(c1_ref[h, row, :], (sub, tb)).astype(e2_ref.dtype)[None]
            n1 = jnp.broadcast_to(n1_ref[h, row, :], (sub, tb)).astype(e2_ref.dtype)[None]
            e2 = e2_ref[h]
            term = jnp.where(r2_ref[h] < n1, e2, jnp.zeros_like(e2)) * c1
            gates[ii] = term if h == 0 else gates[ii] + term
        return [functools.partial(piece, c * per + i, h) for i in range(per) for h in range(PEER_HEADS)]

    def act_pieces(c):
        def piece(ii):
            act = _gelu_tanh(scores.pop(ii)).astype(e2_ref.dtype)
            weights[ii] = act * gates.pop(ii).reshape(nk, tb)
        return [functools.partial(piece, c * per + i) for i in range(per)]

    def value_pieces(c):
        ex = slice(c * MXU_DEPTH, (c + 1) * MXU_DEPTH)

        def piece(m):
            if m == 0:
                chunk_w[c] = jnp.concatenate([weights.pop(c * per + i) for i in range(per)], axis=0)
            rows = slice(m * MXU_DEPTH, (m + 1) * MXU_DEPTH)
            acc_s[rows, :] += _dot(vt_ref[rows, ex], chunk_w[c])
        return [functools.partial(piece, m) for m in range(d_model // MXU_DEPTH)]

    _merge_run(score_pieces(0), gate_pieces(0))
    for c in range(n_chunks):
        mxu = (score_pieces(c + 1) if c + 1 < n_chunks else []) + (value_pieces(c - 1) if c else [])
        vpu = act_pieces(c) + (gate_pieces(c + 1) if c + 1 < n_chunks else [])
        _merge_run(mxu, vpu)
    _merge_run(value_pieces(n_chunks - 1), [])

    @pl.when(j == pl.num_programs(1) - 1)
    def _():
        y = x_ref[...] + acc_s[...].T
        if final:
            y = _rms(y, gf_ref[...])
        o_ref[...] = y


def peer_dense_residual(x, g, route, u, vt, g_final, *, tb, grp):
    t, d = x.shape
    nk = PEER_NKEYS
    eb = grp * nk
    c1, n1, e2, r2 = route
    sub = 32 // jnp.dtype(e2.dtype).itemsize
    e2 = e2.reshape(PEER_HEADS, nk // sub, sub, t)
    r2 = r2.reshape(PEER_HEADS, nk // sub, sub, t)
    rspec = pl.BlockSpec((PEER_HEADS, nk, tb), lambda i, j: (0, 0, i))
    pspec = pl.BlockSpec((PEER_HEADS, nk // sub, sub, tb), lambda i, j: (0, 0, 0, i))
    in_specs = [pl.BlockSpec((tb, d), lambda i, j: (i, 0)),
                pl.BlockSpec((1, d), lambda i, j: (0, 0)),
                rspec, rspec, pspec, pspec,
                pl.BlockSpec((eb, d), lambda i, j: (j, 0)),
                pl.BlockSpec((d, eb), lambda i, j: (0, j))]
    args = [x, g.reshape(1, d), c1, n1, e2, r2, u, vt]
    final = g_final is not None
    if final:
        in_specs.append(pl.BlockSpec((1, d), lambda i, j: (0, 0)))
        args.append(g_final.reshape(1, d))
    return pl.pallas_call(
        functools.partial(_peer_dense_body, final=final),
        out_shape=jax.ShapeDtypeStruct((t, d), F32),
        grid=(t // tb, (nk * nk) // eb),
        in_specs=in_specs,
        out_specs=pl.BlockSpec((tb, d), lambda i, j: (i, 0)),
        scratch_shapes=[pltpu.VMEM((d, tb), BF16),
                        pltpu.VMEM((d, tb), F32)],
        compiler_params=_params(2),
        name="peer_dense_final" if final else "peer_dense",
    )(*args)


def _pick(n, prefs):
    for p in prefs:
        if n % p == 0:
            return p
    return n


def _encoder(x3, mem3, norm_mix, w_in, ret_norm, gdn_conv, gdn_a_log, gdn_dt_bias, gdn_norm, w_out,
             norm_xa, norm_mem, w_xq, w_xkv, w_xo, norm_ffn, peer_wq, peer_keys, peer_u, peer_v,
             norm_final):
    b, s_len, d = x3.shape
    n_mem = mem3.shape[1]
    t = b * s_len
    depth = w_in.shape[0]
    x = x3.reshape(t, d)
    mem = mem3.reshape(b * n_mem, d)
    n_main = 4 * RET_HEADS * HEAD_DIM + 4 * GDN_HEADS * HEAD_DIM
    tm = _pick(t, (1024, 512, 256, 128))
    tmm = _pick(b * n_mem, (1024, 512, 256, 128))
    ts = _pick(s_len, (512, 256, 128))
    tb_route = _pick(t, (256, 128))
    tb_dense = _pick(t, (512, 256, 128))
    nh = 2 * GDN_HEADS

    for l in range(depth):
        w_main = _deinterleave_rotary_columns(w_in[l, :, :n_main]).astype(BF16)
        w_gate_t = w_in[l, :, n_main:].T.astype(BF16)
        proj, graw = norm_matmul_side(x, norm_mix[l], w_main, w_gate_t, tm=tm, tn=1024)
        proj3 = proj.reshape(b, s_len, n_main)
        o_r = retention(proj3, ret_norm[l], b, s_len, heads=RET_HEADS_PER_STEP)
        gates = gdn_gates(graw, gdn_a_log[l], gdn_dt_bias[l], b, s_len)
        gates = gates.reshape(b, 3, 2, GDN_HEADS, s_len).transpose(0, 3, 2, 1, 4)
        gate_row = jnp.pad(gates.reshape(b, GDN_HEADS, 6, s_len), ((0, 0), (0, 0), (0, 2), (0, 0)))
        gate_col = jnp.swapaxes(gate_row, 2, 3)
        gate_row = gate_row.reshape(b, GDN_HEADS, 8, s_len // GDN_SUPER, GDN_SUPER).transpose(0, 1, 3, 2, 4)
        qkv = gdn_prep(proj3, gdn_conv[l], b, s_len)
        o_g = gdn_core(qkv, proj3, gate_col, gate_row, gdn_norm[l], b, s_len, heads=GDN_HEADS_PER_STEP)
        w_o = w_out[l].astype(BF16)
        n_r = RET_HEADS * HEAD_DIM
        x = out_proj_residual(x, o_r.reshape(t, -1), o_g.reshape(t, -1), w_o[:n_r], w_o[n_r:], tm=tm)

        kv = norm_matmul(mem, norm_mem[l], w_xkv[l].astype(BF16), tm=tmm, tn=1024)
        x = xattn_residual(x, norm_xa[l], w_xq[l].astype(BF16), kv.reshape(b, n_mem, 2 * d),
                           w_xo[l].astype(BF16), b, s_len, ts=ts)

        pq = norm_matmul(x, norm_ffn[l], peer_wq[l].astype(BF16), tm=tm, tn=1024)
        route = peer_route(pq, peer_keys[l].astype(BF16), tb=tb_route)
        x = peer_dense_residual(x, norm_ffn[l], route, peer_u[l].astype(BF16),
                                peer_v[l].astype(BF16).T,
                                norm_final if l == depth - 1 else None, tb=tb_dense, grp=16)
    return x.reshape(b, s_len, d)


def kernel(x_prompt, x_sample, mem_prompt, mem_sample, norm_mix, w_in, ret_norm, gdn_conv, gdn_a_log,
           gdn_dt_bias, gdn_norm, w_out, norm_xa, norm_mem, w_xq, w_xkv, w_xo, norm_ffn, peer_wq,
           peer_keys, peer_u, peer_v, norm_final):
    weights = (norm_mix, w_in, ret_norm, gdn_conv, gdn_a_log, gdn_dt_bias, gdn_norm, w_out,
               norm_xa, norm_mem, w_xq, w_xkv, w_xo, norm_ffn, peer_wq, peer_keys, peer_u, peer_v,
               norm_final)
    if x_prompt.shape[1:] == x_sample.shape[1:] and mem_prompt.shape[1:] == mem_sample.shape[1:]:
        nb = x_prompt.shape[0]
        y = _encoder(jnp.concatenate([x_prompt, x_sample], axis=0),
                     jnp.concatenate([mem_prompt, mem_sample], axis=0), *weights)
        return (y[:nb], y[nb:])
    return (_encoder(x_prompt, mem_prompt, *weights), _encoder(x_sample, mem_sample, *weights))
```

```python
import functools

import jax
import jax.numpy as jnp
from jax import lax
from jax.experimental import pallas as pl
from jax.experimental.pallas import tpu as pltpu

F32 = jnp.float32
BF16 = jnp.bfloat16

EPS = 1e-6
LANES = 128
MXU_DEPTH = 256
HEAD_DIM = 128
RET_HEADS = 4
GDN_HEADS = 4
RET_CHUNK = 128
GDN_CHUNK = 64
GDN_SUPER = 256
GDN_HEADS_PER_STEP = 2
RET_HEADS_PER_STEP = 2
ROPE_BASE = 10000.0
XA_HEADS = 4
PEER_HEADS = 8
PEER_NKEYS = 128
PEER_TOPK = 16
VMEM_LIMIT = 56 * 1024 * 1024

_NT = (((1,), (1,)), ((), ()))
_TN = (((0,), (0,)), ((), ()))


def _params(n_axes):
    return pltpu.CompilerParams(
        dimension_semantics=("arbitrary",) * n_axes, vmem_limit_bytes=VMEM_LIMIT)


def _dot(a, b):
    return jnp.dot(a, b, preferred_element_type=F32)


def _dot_nt(a, b):
    return lax.dot_general(a, b, _NT, preferred_element_type=F32)


def _dot_tn(a, b):
    return lax.dot_general(a, b, _TN, preferred_element_type=F32)


def _split(a):
    hi = a.astype(BF16)
    return hi, (a - hi.astype(F32)).astype(BF16)


def _dot3(a, b):
    (ah, al), (bh, bl) = a, b
    return _dot(jnp.concatenate([ah, ah, al], axis=1), jnp.concatenate([bh, bl, bh], axis=0))


def _rms(x, g):
    return x * lax.rsqrt(jnp.mean(x * x, axis=-1, keepdims=True) + EPS) * g


def _norm_matmul_body(x_ref, g_ref, w_ref, o_ref, xn_ref):
    @pl.when(pl.program_id(1) == 0)
    def _():
        xn_ref[...] = _rms(x_ref[...], g_ref[...]).astype(BF16)

    o_ref[...] = _dot(xn_ref[...], w_ref[...]).astype(o_ref.dtype)


def norm_matmul(x, g, w, *, tm, tn, out_dtype=F32):
    t, d = x.shape
    n = w.shape[1]
    return pl.pallas_call(
        _norm_matmul_body,
        out_shape=jax.ShapeDtypeStruct((t, n), out_dtype),
        grid=(t // tm, n // tn),
        in_specs=[pl.BlockSpec((tm, d), lambda i, j: (i, 0)),
                  pl.BlockSpec((1, d), lambda i, j: (0, 0)),
                  pl.BlockSpec((d, tn), lambda i, j: (0, j))],
        out_specs=pl.BlockSpec((tm, tn), lambda i, j: (i, j)),
        scratch_shapes=[pltpu.VMEM((tm, d), BF16)],
        compiler_params=_params(2),
        name="norm_matmul",
    )(x, g.reshape(1, d), w)


def _norm_matmul_side_body(x_ref, g_ref, w_ref, ws_ref, o_ref, os_ref, xn_ref):
    @pl.when(pl.program_id(1) == 0)
    def _():
        xn = _rms(x_ref[...], g_ref[...]).astype(BF16)
        xn_ref[...] = xn
        os_ref[...] = _dot_nt(ws_ref[...], xn)

    o_ref[...] = _dot(xn_ref[...], w_ref[...])


def norm_matmul_side(x, g, w, ws_t, *, tm, tn):
    t, d = x.shape
    n = w.shape[1]
    rows = ws_t.shape[0]
    return pl.pallas_call(
        _norm_matmul_side_body,
        out_shape=(jax.ShapeDtypeStruct((t, n), F32),
                   jax.ShapeDtypeStruct((rows, t), F32)),
        grid=(t // tm, n // tn),
        in_specs=[pl.BlockSpec((tm, d), lambda i, j: (i, 0)),
                  pl.BlockSpec((1, d), lambda i, j: (0, 0)),
                  pl.BlockSpec((d, tn), lambda i, j: (0, j)),
                  pl.BlockSpec((rows, d), lambda i, j: (0, 0))],
        out_specs=(pl.BlockSpec((tm, tn), lambda i, j: (i, j)),
                   pl.BlockSpec((rows, tm), lambda i, j: (0, i))),
        scratch_shapes=[pltpu.VMEM((tm, d), BF16)],
        compiler_params=_params(2),
        name="norm_matmul_side",
    )(x, g.reshape(1, d), w, ws_t)


def _out_proj_body(x_ref, a1_ref, a2_ref, w1_ref, w2_ref, o_ref):
    o_ref[...] = (x_ref[...] + _dot(a1_ref[...], w1_ref[...])
                  + _dot(a2_ref[...], w2_ref[...]))


def out_proj_residual(x, a1, a2, w1, w2, *, tm):
    t, d = x.shape
    k1, k2 = a1.shape[1], a2.shape[1]
    return pl.pallas_call(
        _out_proj_body,
        out_shape=jax.ShapeDtypeStruct((t, d), F32),
        grid=(t // tm,),
        in_specs=[pl.BlockSpec((tm, d), lambda i: (i, 0)),
                  pl.BlockSpec((tm, k1), lambda i: (i, 0)),
                  pl.BlockSpec((tm, k2), lambda i: (i, 0)),
                  pl.BlockSpec((k1, d), lambda i: (0, 0)),
                  pl.BlockSpec((k2, d), lambda i: (0, 0))],
        out_specs=pl.BlockSpec((tm, d), lambda i: (i, 0)),
        compiler_params=_params(1),
        name="out_proj_residual",
    )(x, a1, a2, w1, w2)


def _retention_body(q_ref, k_ref, v_ref, g_ref, cos_ref, sin_ref, dec_ref, rn_ref,
                    o_ref, qr_s, kr_s, of_s, ob_s):
    s_len = q_ref.shape[0]
    hd = HEAD_DIM
    heads = q_ref.shape[1] // hd
    c = RET_CHUNK
    n = s_len // c
    cos = cos_ref[...]
    sin = sin_ref[...]

    def rot(x):
        return x * cos + pltpu.roll(x, hd // 2, 1) * sin

    for hh in range(heads):
        lanes = slice(hh * hd, (hh + 1) * hd)
        qr_s[:, lanes] = rot(q_ref[:, lanes])
        kr_s[:, lanes] = rot(k_ref[:, lanes]) * (hd ** -0.5)

    def forward(hh, i, st):
        r = pl.ds(pl.multiple_of(i * c, c), c)
        lanes = slice(hh * hd, (hh + 1) * hd)
        qc, kc, vb = qr_s[r, lanes], kr_s[r, lanes], v_ref[r, lanes].astype(BF16)
        sc = _dot_nt(qc.astype(BF16), kc.astype(BF16)) * dec_ref[hh, 0]
        inter = _dot((qc * dec_ref[hh, 1]).astype(BF16), st.astype(BF16))
        st = st * dec_ref[hh, 5] + _dot_tn((kc * dec_ref[hh, 2]).astype(BF16), vb)
        yield
        return inter + _dot(sc.astype(BF16), vb), st

    def backward(hh, i, st):
        r = pl.ds(pl.multiple_of(i * c, c), c)
        lanes = slice(hh * hd, (hh + 1) * hd)
        qc, kc, vb = qr_s[r, lanes], kr_s[r, lanes], v_ref[r, lanes].astype(BF16)
        inter = _dot((qc * dec_ref[hh, 3]).astype(BF16), st.astype(BF16))
        st = st * dec_ref[hh, 6] + _dot_tn((kc * dec_ref[hh, 4]).astype(BF16), vb)
        yield
        return inter, st

    def step(t, carry):
        fwd_rows = pl.ds(pl.multiple_of(t * c, c), c)
        bwd_rows = pl.ds(pl.multiple_of((n - 1 - t) * c, c), c)
        gens = []
        for hh in range(heads):
            gens.append(forward(hh, t, carry[2 * hh]))
            gens.append(backward(hh, n - 1 - t, carry[2 * hh + 1]))
        res = _lockstep(gens)
        for hh in range(heads):
            lanes = slice(hh * hd, (hh + 1) * hd)
            of_s[fwd_rows, lanes] = res[2 * hh][0]
            ob_s[bwd_rows, lanes] = res[2 * hh + 1][0]
        return tuple(st for _, st in res)

    zero = jnp.zeros((hd, hd), F32)
    lax.fori_loop(0, n, step, (zero,) * (2 * heads))

    for hh in range(heads):
        lanes = slice(hh * hd, (hh + 1) * hd)
        o = of_s[:, lanes] + ob_s[:, lanes]
        mu = jnp.mean(o, axis=-1, keepdims=True)
        var = jnp.mean(jnp.square(o - mu), axis=-1, keepdims=True)
        y = (o - mu) * lax.rsqrt(var + EPS)
        y = y * rn_ref[:, lanes] * jax.nn.silu(g_ref[:, lanes])
        o_ref[:, lanes] = y.astype(o_ref.dtype)


def _deinterleave_rotary_columns(w):
    n_qk = 2 * RET_HEADS * HEAD_DIM
    qk = w[:, :n_qk].reshape(w.shape[0], 2 * RET_HEADS, HEAD_DIM // 2, 2)
    qk = jnp.swapaxes(qk, 2, 3).reshape(w.shape[0], n_qk)
    return jnp.concatenate([qk, w[:, n_qk:]], axis=1)


def _retention_tables(s_len):
    d = HEAD_DIM
    inv = ROPE_BASE ** (-jnp.arange(0, d, 2, dtype=F32) / d)
    ang = jnp.arange(s_len, dtype=F32)[:, None] * inv[None, :]
    cos = jnp.concatenate([jnp.cos(ang), jnp.cos(ang)], axis=1)
    sin = jnp.concatenate([-jnp.sin(ang), jnp.sin(ang)], axis=1)
    h = jnp.arange(RET_HEADS, dtype=F32)
    lg_f = jnp.log1p(-jnp.exp2(-5.0 - h))[:, None, None]
    lg_b = jnp.log1p(-jnp.exp2(-5.5 - h))[:, None, None]
    c = RET_CHUNK
    pos = jnp.arange(c, dtype=F32)
    diff = (pos[:, None] - pos[None, :])[None]
    dmat = jnp.where(diff >= 0, jnp.exp(lg_f * jnp.where(diff >= 0, diff, 0.0)),
                     jnp.exp(lg_b * jnp.where(diff < 0, -diff, 0.0)))
    col = lambda v: jnp.broadcast_to(v, (RET_HEADS, c, d))
    p = pos[None, :, None]
    dec = jnp.stack([
        dmat,
        col(jnp.exp(lg_f * (p + 1.0))),
        col(jnp.exp(lg_f * (c - 1.0 - p))),
        col(jnp.exp(lg_b * (c - p))),
        col(jnp.exp(lg_b * p)),
        col(jnp.exp(lg_f * c)),
        col(jnp.exp(lg_b * c)),
    ], axis=1)
    return cos, sin, dec


def retention(proj, ret_norm, b, s_len, *, heads):
    cos, sin, dec = _retention_tables(s_len)
    hd = HEAD_DIM
    w = heads * hd
    ng = RET_HEADS // heads
    head_blk = lambda off: pl.BlockSpec((None, s_len, w), lambda i, h: (i, 0, off + h))
    return pl.pallas_call(
        _retention_body,
        out_shape=jax.ShapeDtypeStruct((b, s_len, RET_HEADS * hd), BF16),
        grid=(b, ng),
        in_specs=[head_blk(0), head_blk(ng), head_blk(2 * ng), head_blk(3 * ng),
                  pl.BlockSpec((s_len, hd), lambda i, h: (0, 0)),
                  pl.BlockSpec((s_len, hd), lambda i, h: (0, 0)),
                  pl.BlockSpec((heads, 7, RET_CHUNK, hd), lambda i, h: (h, 0, 0, 0)),
                  pl.BlockSpec((1, w), lambda i, h: (0, h))],
        out_specs=pl.BlockSpec((None, s_len, w), lambda i, h: (i, 0, h)),
        scratch_shapes=[pltpu.VMEM((s_len, w), F32)] * 4,
        compiler_params=_params(2),
        name="retention",
    )(proj, proj, proj, proj, cos, sin, dec, ret_norm.reshape(1, -1))


def _gdn_gates_body(raw_ref, al_ref, dt_ref, o_ref):
    s_len = raw_ref.shape[-1]
    nh = 2 * GDN_HEADS
    gb = raw_ref[0:nh, :]
    ga = raw_ref[nh:2 * nh, :]
    beta = jax.nn.sigmoid(gb)
    g = -jnp.exp(al_ref[...]) * jax.nn.softplus(ga + dt_ref[...])
    pos = lax.broadcasted_iota(jnp.int32, (nh, s_len), 1) & (GDN_CHUNK - 1)
    fwd = g
    rev = g
    k = 1
    while k < GDN_CHUNK:
        fwd = fwd + jnp.where(pos >= k, pltpu.roll(fwd, k, 1), 0.0)
        rev = rev + jnp.where(pos < GDN_CHUNK - k, pltpu.roll(rev, s_len - k, 1), 0.0)
        k *= 2
    row = lax.broadcasted_iota(jnp.int32, (nh, s_len), 0)
    o_ref[0:nh, :] = jnp.where(row < GDN_HEADS, fwd, rev)
    o_ref[nh:2 * nh, :] = beta
    o_ref[2 * nh:3 * nh, :] = fwd + rev - g


def gdn_gates(raw, a_log, dt_bias, b, s_len):
    nh = 2 * GDN_HEADS
    return pl.pallas_call(
        _gdn_gates_body,
        out_shape=jax.ShapeDtypeStruct((b, 3 * nh, s_len), F32),
        grid=(b,),
        in_specs=[pl.BlockSpec((2 * nh, s_len), lambda i: (0, i)),
                  pl.BlockSpec((nh, 1), lambda i: (0, 0)),
                  pl.BlockSpec((nh, 1), lambda i: (0, 0))],
        out_specs=pl.BlockSpec((None, 3 * nh, s_len), lambda i: (i, 0, 0)),
        compiler_params=_params(1),
        name="gdn_gates",
    )(raw, a_log.reshape(nh, 1), dt_bias.reshape(nh, 1))


def _gdn_prep_body(x_ref, w_ref, o_ref):
    s_len = x_ref.shape[0]
    j = pl.program_id(1)
    x = x_ref[...]
    t = lax.broadcasted_iota(jnp.int32, x.shape, 0)
    prev = jnp.where(t == 0, 0.0, pltpu.roll(x, 1, 0))
    nxt = jnp.where(t == s_len - 1, 0.0, pltpu.roll(x, s_len - 1, 0))
    conv = prev * w_ref[0:1, :] + x * w_ref[1:2, :] + nxt * w_ref[2:3, :]
    act = jax.nn.silu(conv)
    inv = lax.rsqrt(jnp.sum(act * act, axis=-1, keepdims=True) + EPS)
    is_q = j < GDN_HEADS
    is_v = j >= 2 * GDN_HEADS
    scale = jnp.where(is_v, 1.0, jnp.where(is_q, inv * (HEAD_DIM ** -0.5), inv))
    o_ref[...] = act * scale


def gdn_prep(proj, conv_w, b, s_len):
    hd = HEAD_DIM
    nblk = 3 * GDN_HEADS
    first = 4 * RET_HEADS
    return pl.pallas_call(
        _gdn_prep_body,
        out_shape=jax.ShapeDtypeStruct((b, s_len, nblk * hd), F32),
        grid=(b, nblk),
        in_specs=[pl.BlockSpec((None, s_len, hd), lambda i, j: (i, 0, first + j)),
                  pl.BlockSpec((3, hd), lambda i, j: (0, j))],
        out_specs=pl.BlockSpec((None, s_len, hd), lambda i, j: (i, 0, j)),
        compiler_params=_params(2),
        name="gdn_prep",
    )(proj, conv_w)


def _lockstep(gens):
    results = [None] * len(gens)
    live = list(range(len(gens)))
    while live:
        for i in list(live):
            try:
                next(gens[i])
            except StopIteration as done:
                results[i] = done.value
                live.remove(i)
    return results


def _merge_run(a, b):
    ia = ib = 0
    while ia < len(a) or ib < len(b):
        if ib >= len(b) or (ia < len(a) and (ia + 1) * len(b) <= (ib + 1) * len(a)):
            a[ia]()
            ia += 1
        else:
            b[ib]()
            ib += 1


def _tri_inverse(m, ri, ci):
    blk16 = (ri >> 4) == (ci >> 4)
    blk32 = (ri >> 5) == (ci >> 5)
    eye = (ri == ci).astype(F32)
    nm = _split(jnp.where(blk16, -m, 0.0))
    p = eye + jnp.where(blk16, -m, 0.0)
    for _ in range(3):
        nm = _split(_dot3(nm, nm))
        yield
        p = p + _dot3(_split(p), nm)
        yield
    for off in (jnp.where(blk32 & jnp.logical_not(blk16), m, 0.0), jnp.where(blk32, 0.0, m)):
        p16 = p.astype(BF16)
        pc = _dot(p16, off.astype(BF16)).astype(BF16)
        yield
        p = p - _dot(pc, p16)
        yield
    return p


def _gdn_super_chunk(q, k, v, gcol, grow, bcol, glcol, st, rev):
    r = GDN_SUPER
    c = GDN_CHUNK
    ri = lax.broadcasted_iota(jnp.int32, (r, r), 0)
    ci = lax.broadcasted_iota(jnp.int32, (r, r), 1)
    same = (ri >> 6) == (ci >> 6)
    if rev:
        incl = same & (ri <= ci)
        strict = same & (ri < ci)
    else:
        incl = same & (ri >= ci)
        strict = same & (ri > ci)
    decay = jnp.exp(jnp.where(incl, gcol - grow, -jnp.inf))
    kb = k * bcol
    vb = v * bcol
    k16 = k.astype(BF16)
    m = jnp.where(strict, _dot_nt(kb.astype(BF16), k16) * decay, 0.0)
    qk = jnp.where(incl, _dot_nt(q.astype(BF16), k16) * decay, 0.0).astype(BF16)
    yield
    t = yield from _tri_inverse(m, ri, ci)
    eg = jnp.exp(gcol)
    rhs = jnp.concatenate([vb, kb * eg], axis=1).astype(BF16)
    uw = _dot(t.astype(BF16), rhs)
    u = uw[:, :HEAD_DIM]
    w = uw[:, HEAD_DIM:].astype(BF16)
    qd = (q * eg).astype(BF16)
    kt = (k * jnp.exp(glcol - gcol)).astype(BF16)
    yield
    outs = [None] * (r // c)
    order = range(r // c - 1, -1, -1) if rev else range(r // c)
    for i in order:
        sl = slice(i * c, (i + 1) * c)
        s16 = st.astype(BF16)
        vn = u[sl] - _dot(w[sl], s16)
        vn16 = vn.astype(BF16)
        yield
        pair = slice((i // 2) * 2 * c, (i // 2 + 1) * 2 * c)
        zero = jnp.zeros_like(vn16)
        vpad = jnp.concatenate([zero, vn16] if i % 2 else [vn16, zero], axis=0)
        outs[i] = _dot(qd[sl], s16) + _dot(qk[sl, pair], vpad)
        st = st * jnp.exp(glcol[i * c:i * c + 1, :]) + _dot_tn(kt[sl], vn16)
        yield
    return jnp.concatenate(outs, axis=0), st


def _gdn_core_body(q_ref, k_ref, v_ref, z_ref, col_ref, row_ref, gn_ref, o_ref, of_s, ob_s):
    s_len = q_ref.shape[0]
    hd = HEAD_DIM
    heads = q_ref.shape[1] // hd
    r = GDN_SUPER
    n = s_len // r

    def one(hh, i, st, rev):
        base = 3 if rev else 0
        rows = pl.ds(pl.multiple_of(i * r, r), r)
        lanes = slice(hh * hd, (hh + 1) * hd)
        cols = col_ref[hh, rows, :]
        gcol = cols[:, base:base + 1]
        bcol = cols[:, base + 1:base + 2]
        glcol = cols[:, base + 2:base + 3]
        grow = row_ref[hh, i][base:base + 1, :]
        return _gdn_super_chunk(q_ref[rows, lanes], k_ref[rows, lanes], v_ref[rows, lanes],
                                gcol, grow, bcol, glcol, st, rev)

    def step(t, carry):
        fwd_rows = pl.ds(pl.multiple_of(t * r, r), r)
        bwd_rows = pl.ds(pl.multiple_of((n - 1 - t) * r, r), r)
        gens = []
        for hh in range(heads):
            gens.append(one(hh, t, carry[2 * hh], False))
            gens.append(one(hh, n - 1 - t, carry[2 * hh + 1], True))
        res = _lockstep(gens)
        for hh in range(heads):
            lanes = slice(hh * hd, (hh + 1) * hd)
            of_s[fwd_rows, lanes] = res[2 * hh][0]
            ob_s[bwd_rows, lanes] = res[2 * hh + 1][0]
        return tuple(st for _, st in res)

    zero = jnp.zeros((hd, hd), F32)
    lax.fori_loop(0, n, step, (zero,) * (2 * heads))
    for hh in range(heads):
        lanes = slice(hh * hd, (hh + 1) * hd)
        o = of_s[:, lanes] + ob_s[:, lanes]
        y = o * lax.rsqrt(jnp.mean(o * o, axis=-1, keepdims=True) + EPS) * gn_ref[...]
        o_ref[:, lanes] = (y * jax.nn.silu(z_ref[:, lanes])).astype(o_ref.dtype)


def gdn_core(qkv, proj, gate_col, gate_row, gdn_norm, b, s_len, *, heads):
    hd = HEAD_DIM
    w = heads * hd
    ng = GDN_HEADS // heads
    z_first = (4 * RET_HEADS + 3 * GDN_HEADS) // heads
    qkv_blk = lambda off: pl.BlockSpec((None, s_len, w), lambda i, h: (i, 0, off + h))
    return pl.pallas_call(
        _gdn_core_body,
        out_shape=jax.ShapeDtypeStruct((b, s_len, GDN_HEADS * hd), BF16),
        grid=(b, ng),
        in_specs=[qkv_blk(0), qkv_blk(ng), qkv_blk(2 * ng),
                  pl.BlockSpec((None, s_len, w), lambda i, h: (i, 0, z_first + h)),
                  pl.BlockSpec((None, heads, s_len, 8), lambda i, h: (i, h, 0, 0)),
                  pl.BlockSpec((None, heads, s_len // GDN_SUPER, 8, GDN_SUPER),
                               lambda i, h: (i, h, 0, 0, 0)),
                  pl.BlockSpec((1, hd), lambda i, h: (0, 0))],
        out_specs=pl.BlockSpec((None, s_len, w), lambda i, h: (i, 0, h)),
        scratch_shapes=[pltpu.VMEM((s_len, w), F32)] * 2,
        compiler_params=_params(2),
        name="gdn_core",
    )(qkv, qkv, qkv, proj, gate_col, gate_row, gdn_norm.reshape(1, hd))


def _xattn_body(x_ref, g_ref, wq_ref, kv_ref, wo_ref, o_ref):
    x = x_ref[...]
    d = x.shape[1]
    dh = d // XA_HEADS
    q = _dot(_rms(x, g_ref[...]).astype(BF16), wq_ref[...])
    outs = []
    for h in range(XA_HEADS):
        kh = kv_ref[:, h * dh:(h + 1) * dh].astype(BF16)
        vh = kv_ref[:, d + h * dh:d + (h + 1) * dh].astype(BF16)
        sc = _dot_nt(q[:, h * dh:(h + 1) * dh].astype(BF16), kh) * (dh ** -0.5)
        p = jax.nn.softmax(sc, axis=-1)
        outs.append(_dot(p.astype(BF16), vh).astype(BF16))
    o = jnp.concatenate(outs, axis=1)
    o_ref[...] = x + _dot(o, wo_ref[...])


def xattn_residual(x, g, wq, kv, wo, b, s_len, *, ts):
    t, d = x.shape
    m = kv.shape[1]
    nblk = s_len // ts
    return pl.pallas_call(
        _xattn_body,
        out_shape=jax.ShapeDtypeStruct((t, d), F32),
        grid=(b, nblk),
        in_specs=[pl.BlockSpec((ts, d), lambda i, j: (i * nblk + j, 0)),
                  pl.BlockSpec((1, d), lambda i, j: (0, 0)),
                  pl.BlockSpec((d, d), lambda i, j: (0, 0)),
                  pl.BlockSpec((None, m, 2 * d), lambda i, j: (i, 0, 0)),
                  pl.BlockSpec((d, d), lambda i, j: (0, 0))],
        out_specs=pl.BlockSpec((ts, d), lambda i, j: (i * nblk + j, 0)),
        compiler_params=_params(2),
        name="xattn_residual",
    )(x, g.reshape(1, d), wq, kv, wo)


def _extract_topk(s, k, break_ties):
    n_rows = s.shape[0]
    iota = lax.broadcasted_iota(jnp.int32, s.shape, 0).astype(F32)
    rank = jnp.full(s.shape, float(k), F32)
    vals = []
    for r in range(k):
        m = jnp.max(s, axis=0, keepdims=True)
        hit = s == m
        if break_ties:
            idx = jnp.min(jnp.where(hit, iota, float(n_rows)), axis=0, keepdims=True)
            hit = iota == idx
        rank = jnp.where(hit, float(r), rank)
        s = jnp.where(hit, -jnp.inf, s)
        vals.append(m)
    count = jnp.sum(jnp.where(rank < float(k), 1.0, 0.0), axis=0, keepdims=True)
    return rank, vals, count


def _route_tokens(q, keys_ref, break_ties):
    nk = PEER_NKEYS
    kk = PEER_TOPK
    s1 = _dot_nt(keys_ref[0], q[:, :nk])
    s2 = _dot_nt(keys_ref[1], q[:, nk:])
    r1, v1, n_sel1 = _extract_topk(s1, kk, break_ties)
    r2, v2, n_sel2 = _extract_topk(s2, kk, break_ties)
    v1m = jnp.concatenate(v1, axis=0)
    v2m = jnp.concatenate(v2, axis=0)
    row8 = lax.broadcasted_iota(jnp.int32, (8, LANES), 0)
    groups = [v1[0] + v2m[0:8], v1[0] + v2m[8:16]]
    for a in range(1, 8):
        groups.append(jnp.where(row8 < kk // (a + 1), v1[a] + v2m[0:8], -jnp.inf))
    groups.append(v1m[8:16] + v2[0])
    cand = jnp.concatenate(groups, axis=0)
    rc, _, n_selc = _extract_topk(cand, kk, break_ties)
    sel = rc < float(kk)
    top = v1[0] + v2[0]
    z = jnp.sum(jnp.where(sel, jnp.exp(cand - top), 0.0), axis=0, keepdims=True)
    sel_f = jnp.where(sel, 1.0, 0.0)
    n1 = jnp.zeros((nk, LANES), F32)
    for a in range(kk):
        if a == 0:
            cnt = jnp.sum(sel_f[0:16], axis=0, keepdims=True)
        elif a < 8:
            cnt = jnp.sum(sel_f[8 + 8 * a:16 + 8 * a], axis=0, keepdims=True)
        else:
            cnt = sel_f[64 + a:65 + a]
        n1 = jnp.where(r1 == float(a), cnt, n1)
    c1 = jnp.where(r1 < float(kk), jnp.exp(s1 - v1[0]) / z, 0.0)
    e2 = jnp.exp(s2 - v2[0])
    exact = (n_sel1 == float(kk)) & (n_sel2 == float(kk)) & (n_selc == float(kk))
    return c1, n1, e2, r2, jnp.where(exact, 1.0, 0.0)


def _peer_route_body(q_ref, keys_ref, c1_ref, n1_ref, e2_ref, r2_ref):
    tb = q_ref.shape[0]

    def run(break_ties):
        flags = []
        for blk in range(tb // LANES):
            tok = slice(blk * LANES, (blk + 1) * LANES)
            c1, n1, e2, r2, ok = _route_tokens(q_ref[tok, :].astype(BF16), keys_ref, break_ties)
            c1_ref[:, tok] = c1
            n1_ref[:, tok] = n1
            e2_ref[:, tok] = e2.astype(e2_ref.dtype)
            r2_ref[:, tok] = r2.astype(r2_ref.dtype)
            flags.append(ok)
        return jnp.min(jnp.concatenate(flags, axis=1))

    all_exact = run(False)

    @pl.when(all_exact < 0.5)
    def _():
        run(True)


def peer_route(q, keys, *, tb):
    t = q.shape[0]
    nk = PEER_NKEYS
    out = lambda dt: jax.ShapeDtypeStruct((PEER_HEADS, nk, t), dt)
    ospec = pl.BlockSpec((None, nk, tb), lambda i, h: (h, 0, i))
    return pl.pallas_call(
        _peer_route_body,
        out_shape=(out(F32), out(F32), out(BF16), out(BF16)),
        grid=(t // tb, PEER_HEADS),
        in_specs=[pl.BlockSpec((tb, 2 * nk), lambda i, h: (i, h)),
                  pl.BlockSpec((None, 2, nk, nk), lambda i, h: (h, 0, 0, 0))],
        out_specs=(ospec,) * 4,
        compiler_params=_params(2),
        name="peer_route",
    )(q, keys)


def _gelu_tanh(x):
    c0 = 0.7978845608028654
    c1 = 0.7978845608028654 * 0.044715
    return (0.5 * x) * (1.0 + jnp.tanh(x * (c0 + c1 * (x * x))))


def _peer_dense_body(x_ref, g_ref, c1_ref, n1_ref, e2_ref, r2_ref, u_ref, vt_ref, *rest, final):
    if final:
        gf_ref, o_ref, xn_s, acc_s = rest
    else:
        o_ref, xn_s, acc_s = rest
    j = pl.program_id(1)
    nk = PEER_NKEYS
    tb = x_ref.shape[0]
    grp = u_ref.shape[0] // nk
    sub = e2_ref.shape[2]

    @pl.when(j == 0)
    def _():
        xn_s[...] = _rms(x_ref[...], g_ref[...]).T.astype(BF16)
        acc_s[...] = jnp.zeros_like(acc_s)

    per = MXU_DEPTH // nk
    n_chunks = grp // per
    d_model = vt_ref.shape[0]
    xn = xn_s[...]
    scores, gates, weights, chunk_w = {}, {}, {}, {}

    def score_pieces(c):
        def piece(ii):
            scores[ii] = _dot(u_ref[ii * nk:(ii + 1) * nk, :], xn)
        return [functools.partial(piece, c * per + i) for i in range(per)]

    def gate_pieces(c):
        def piece(ii, h):
            row = pl.ds(j * grp + ii, 1)
            c1 = jnp.broadcast_to(c1_ref[h, row, :], (sub, tb)).astype(e2_ref.dtype)[None]
            n1 = jnp.broadcast_to(n1_ref[h, row, :], (sub, tb)).astype(e2_ref.dtype)[None]
            e2 = e2_ref[h]
            term = jnp.where(r2_ref[h] < n1, e2, jnp.zeros_like(e2)) * c1
            gates[ii] = term if h == 0 else gates[ii] + term
        return [functools.partial(piece, c * per + i, h) for i in range(per) for h in range(PEER_HEADS)]

    def act_pieces(c):
        def piece(ii):
            act = _gelu_tanh(scores.pop(ii)).astype(e2_ref.dtype)
            weights[ii] = act * gates.pop(ii).reshape(nk, tb)
        return [functools.partial(piece, c * per + i) for i in range(per)]

    def value_pieces(c):
        ex = slice(c * MXU_DEPTH, (c + 1) * MXU_DEPTH)

        def piece(m):
            if m == 0:
                chunk_w[c] = jnp.concatenate([weights.pop(c * per + i) for i in range(per)], axis=0)
            rows = slice(m * MXU_DEPTH, (m + 1) * MXU_DEPTH)
            acc_s[rows, :] += _dot(vt_ref[rows, ex], chunk_w[c])
        return [functools.partial(piece, m) for m in range(d_model // MXU_DEPTH)]

    _merge_run(score_pieces(0), gate_pieces(0))
    for c in range(n_chunks):
        mxu = (score_pieces(c + 1) if c + 1 < n_chunks else []) + (value_pieces(c - 1) if c else [])
        vpu = act_pieces(c) + (gate_pieces(c + 1) if c + 1 < n_chunks else [])
        _merge_run(mxu, vpu)
    _merge_run(value_pieces(n_chunks - 1), [])

    @pl.when(j == pl.num_programs(1) - 1)
    def _():
        y = x_ref[...] + acc_s[...].T
        if final:
            y = _rms(y, gf_ref[...])
        o_ref[...] = y


def peer_dense_residual(x, g, route, u, vt, g_final, *, tb, grp):
    t, d = x.shape
    nk = PEER_NKEYS
    eb = grp * nk
    c1, n1, e2, r2 = route
    sub = 32 // jnp.dtype(e2.dtype).itemsize
    e2 = e2.reshape(PEER_HEADS, nk // sub, sub, t)
    r2 = r2.reshape(PEER_HEADS, nk // sub, sub, t)
    rspec = pl.BlockSpec((PEER_HEADS, nk, tb), lambda i, j: (0, 0, i))
    pspec = pl.BlockSpec((PEER_HEADS, nk // sub, sub, tb), lambda i, j: (0, 0, 0, i))
    in_specs = [pl.BlockSpec((tb, d), lambda i, j: (i, 0)),
                pl.BlockSpec((1, d), lambda i, j: (0, 0)),
                rspec, rspec, pspec, pspec,
                pl.BlockSpec((eb, d), lambda i, j: (j, 0)),
                pl.BlockSpec((d, eb), lambda i, j: (0, j))]
    args = [x, g.reshape(1, d), c1, n1, e2, r2, u, vt]
    final = g_final is not None
    if final:
        in_specs.append(pl.BlockSpec((1, d), lambda i, j: (0, 0)))
        args.append(g_final.reshape(1, d))
    return pl.pallas_call(
        functools.partial(_peer_dense_body, final=final),
        out_shape=jax.ShapeDtypeStruct((t, d), F32),
        grid=(t // tb, (nk * nk) // eb),
        in_specs=in_specs,
        out_specs=pl.BlockSpec((tb, d), lambda i, j: (i, 0)),
        scratch_shapes=[pltpu.VMEM((d, tb), BF16),
                        pltpu.VMEM((d, tb), F32)],
        compiler_params=_params(2),
        name="peer_dense_final" if final else "peer_dense",
    )(*args)


def _pick(n, prefs):
    for p in prefs:
        if n % p == 0:
            return p
    return n


def _encoder(x3, mem3, norm_mix, w_in, ret_norm, gdn_conv, gdn_a_log, gdn_dt_bias, gdn_norm, w_out,
             norm_xa, norm_mem, w_xq, w_xkv, w_xo, norm_ffn, peer_wq, peer_keys, peer_u, peer_v,
             norm_final):
    b, s_len, d = x3.shape
    n_mem = mem3.shape[1]
    t = b * s_len
    depth = w_in.shape[0]
    x = x3.reshape(t, d)
    mem = mem3.reshape(b * n_mem, d)
    n_main = 4 * RET_HEADS * HEAD_DIM + 4 * GDN_HEADS * HEAD_DIM
    tm = _pick(t, (1024, 512, 256, 128))
    tmm = _pick(b * n_mem, (1024, 512, 256, 128))
    ts = _pick(s_len, (512, 256, 128))
    tb_route = _pick(t, (256, 128))
    tb_dense = _pick(t, (512, 256, 128))
    nh = 2 * GDN_HEADS

    for l in range(depth):
        w_main = _deinterleave_rotary_columns(w_in[l, :, :n_main]).astype(BF16)
        w_gate_t = w_in[l, :, n_main:].T.astype(BF16)
        proj, graw = norm_matmul_side(x, norm_mix[l], w_main, w_gate_t, tm=tm, tn=1024)
        proj3 = proj.reshape(b, s_len, n_main)
        o_r = retention(proj3, ret_norm[l], b, s_len, heads=RET_HEADS_PER_STEP)
        gates = gdn_gates(graw, gdn_a_log[l], gdn_dt_bias[l], b, s_len)
        gates = gates.reshape(b, 3, 2, GDN_HEADS, s_len).transpose(0, 3, 2, 1, 4)
        gate_row = jnp.pad(gates.reshape(b, GDN_HEADS, 6, s_len), ((0, 0), (0, 0), (0, 2), (0, 0)))
        gate_col = jnp.swapaxes(gate_row, 2, 3)
        gate_row = gate_row.reshape(b, GDN_HEADS, 8, s_len // GDN_SUPER, GDN_SUPER).transpose(0, 1, 3, 2, 4)
        qkv = gdn_prep(proj3, gdn_conv[l], b, s_len)
        o_g = gdn_core(qkv, proj3, gate_col, gate_row, gdn_norm[l], b, s_len, heads=GDN_HEADS_PER_STEP)
        w_o = w_out[l].astype(BF16)
        n_r = RET_HEADS * HEAD_DIM
        x = out_proj_residual(x, o_r.reshape(t, -1), o_g.reshape(t, -1), w_o[:n_r], w_o[n_r:], tm=tm)

        kv = norm_matmul(mem, norm_mem[l], w_xkv[l].astype(BF16), tm=tmm, tn=1024)
        x = xattn_residual(x, norm_xa[l], w_xq[l].astype(BF16), kv.reshape(b, n_mem, 2 * d),
                           w_xo[l].astype(BF16), b, s_len, ts=ts)

        pq = norm_matmul(x, norm_ffn[l], peer_wq[l].astype(BF16), tm=tm, tn=1024)
        route = peer_route(pq, peer_keys[l].astype(BF16), tb=tb_route)
        x = peer_dense_residual(x, norm_ffn[l], route, peer_u[l].astype(BF16),
                                peer_v[l].astype(BF16).T,
                                norm_final if l == depth - 1 else None, tb=tb_dense, grp=16)
    return x.reshape(b, s_len, d)


def kernel(x_prompt, x_sample, mem_prompt, mem_sample, norm_mix, w_in, ret_norm, gdn_conv, gdn_a_log,
           gdn_dt_bias, gdn_norm, w_out, norm_xa, norm_mem, w_xq, w_xkv, w_xo, norm_ffn, peer_wq,
           peer_keys, peer_u, peer_v, norm_final):
    weights = (norm_mix, w_in, ret_norm, gdn_conv, gdn_a_log, gdn_dt_bias, gdn_norm, w_out,
               norm_xa, norm_mem, w_xq, w_xkv, w_xo, norm_ffn, peer_wq, peer_keys, peer_u, peer_v,
               norm_final)
    if x_prompt.shape[1:] == x_sample.shape[1:] and mem_prompt.shape[1:] == mem_sample.shape[1:]:
        nb = x_prompt.shape[0]
        y = _encoder(jnp.concatenate([x_prompt, x_sample], axis=0),
                     jnp.concatenate([mem_prompt, mem_sample], axis=0), *weights)
        return (y[:nb], y[nb:])
    return (_encoder(x_prompt, mem_prompt, *weights), _encoder(x_sample, mem_sample, *weights))
```

```python
import functools

import jax
import jax.numpy as jnp
from jax import lax
from jax.experimental import pallas as pl
from jax.experimental.pallas import tpu as pltpu

F32 = jnp.float32
BF16 = jnp.bfloat16

EPS = 1e-6
LANES = 128
MXU_DEPTH = 256
HEAD_DIM = 128
RET_HEADS = 4
GDN_HEADS = 4
RET_CHUNK = 128
GDN_CHUNK = 64
GDN_SUPER = 256
GDN_INV_BASE_LOG2 = 1
GDN_HEADS_PER_STEP = 2
RET_HEADS_PER_STEP = 2
ROPE_BASE = 10000.0
XA_HEADS = 4
PEER_HEADS = 8
PEER_NKEYS = 128
PEER_TOPK = 16
VMEM_LIMIT = 56 * 1024 * 1024

_NT = (((1,), (1,)), ((), ()))
_TN = (((0,), (0,)), ((), ()))


def _params(n_axes):
    return pltpu.CompilerParams(
        dimension_semantics=("arbitrary",) * n_axes, vmem_limit_bytes=VMEM_LIMIT)


def _dot(a, b):
    return jnp.dot(a, b, preferred_element_type=F32)


def _dot_nt(a, b):
    return lax.dot_general(a, b, _NT, preferred_element_type=F32)


def _dot_tn(a, b):
    return lax.dot_general(a, b, _TN, preferred_element_type=F32)


def _split(a):
    hi = a.astype(BF16)
    return hi, (a - hi.astype(F32)).astype(BF16)


def _dot3(a, b):
    (ah, al), (bh, bl) = a, b
    return _dot(jnp.concatenate([ah, ah, al], axis=1), jnp.concatenate([bh, bl, bh], axis=0))


def _rms(x, g):
    return x * lax.rsqrt(jnp.mean(x * x, axis=-1, keepdims=True) + EPS) * g


def _norm_matmul_body(x_ref, g_ref, w_ref, o_ref, xn_ref):
    @pl.when(pl.program_id(1) == 0)
    def _():
        xn_ref[...] = _rms(x_ref[...], g_ref[...]).astype(BF16)

    o_ref[...] = _dot(xn_ref[...], w_ref[...]).astype(o_ref.dtype)


def norm_matmul(x, g, w, *, tm, tn, out_dtype=F32):
    t, d = x.shape
    n = w.shape[1]
    return pl.pallas_call(
        _norm_matmul_body,
        out_shape=jax.ShapeDtypeStruct((t, n), out_dtype),
        grid=(t // tm, n // tn),
        in_specs=[pl.BlockSpec((tm, d), lambda i, j: (i, 0)),
                  pl.BlockSpec((1, d), lambda i, j: (0, 0)),
                  pl.BlockSpec((d, tn), lambda i, j: (0, j))],
        out_specs=pl.BlockSpec((tm, tn), lambda i, j: (i, j)),
        scratch_shapes=[pltpu.VMEM((tm, d), BF16)],
        compiler_params=_params(2),
        name="norm_matmul",
    )(x, g.reshape(1, d), w)


def _norm_matmul_side_body(x_ref, g_ref, w_ref, ws_ref, o_ref, os_ref, xn_ref):
    @pl.when(pl.program_id(1) == 0)
    def _():
        xn = _rms(x_ref[...], g_ref[...]).astype(BF16)
        xn_ref[...] = xn
        os_ref[...] = _dot_nt(ws_ref[...], xn)

    o_ref[...] = _dot(xn_ref[...], w_ref[...])


def norm_matmul_side(x, g, w, ws_t, *, tm, tn):
    t, d = x.shape
    n = w.shape[1]
    rows = ws_t.shape[0]
    return pl.pallas_call(
        _norm_matmul_side_body,
        out_shape=(jax.ShapeDtypeStruct((t, n), F32),
                   jax.ShapeDtypeStruct((rows, t), F32)),
        grid=(t // tm, n // tn),
        in_specs=[pl.BlockSpec((tm, d), lambda i, j: (i, 0)),
                  pl.BlockSpec((1, d), lambda i, j: (0, 0)),
                  pl.BlockSpec((d, tn), lambda i, j: (0, j)),
                  pl.BlockSpec((rows, d), lambda i, j: (0, 0))],
        out_specs=(pl.BlockSpec((tm, tn), lambda i, j: (i, j)),
                   pl.BlockSpec((rows, tm), lambda i, j: (0, i))),
        scratch_shapes=[pltpu.VMEM((tm, d), BF16)],
        compiler_params=_params(2),
        name="norm_matmul_side",
    )(x, g.reshape(1, d), w, ws_t)


def _retention_body(q_ref, k_ref, v_ref, g_ref, cos_ref, sin_ref, dec_ref, rn_ref,
                    o_ref, qr_s, kr_s, of_s, ob_s):
    s_len = q_ref.shape[0]
    hd = HEAD_DIM
    heads = q_ref.shape[1] // hd
    c = RET_CHUNK
    n = s_len // c
    cos = cos_ref[...]
    sin = sin_ref[...]

    def rot(x):
        return x * cos + pltpu.roll(x, hd // 2, 1) * sin

    for hh in range(heads):
        lanes = slice(hh * hd, (hh + 1) * hd)
        qr_s[:, lanes] = rot(q_ref[:, lanes])
        kr_s[:, lanes] = rot(k_ref[:, lanes]) * (hd ** -0.5)

    def forward(hh, i, st):
        r = pl.ds(pl.multiple_of(i * c, c), c)
        lanes = slice(hh * hd, (hh + 1) * hd)
        qc, kc, vb = qr_s[r, lanes], kr_s[r, lanes], v_ref[r, lanes].astype(BF16)
        sc = _dot_nt(qc.astype(BF16), kc.astype(BF16)) * dec_ref[hh, 0]
        inter = _dot((qc * dec_ref[hh, 1]).astype(BF16), st.astype(BF16))
        st = st * dec_ref[hh, 5] + _dot_tn((kc * dec_ref[hh, 2]).astype(BF16), vb)
        yield
        return inter + _dot(sc.astype(BF16), vb), st

    def backward(hh, i, st):
        r = pl.ds(pl.multiple_of(i * c, c), c)
        lanes = slice(hh * hd, (hh + 1) * hd)
        qc, kc, vb = qr_s[r, lanes], kr_s[r, lanes], v_ref[r, lanes].astype(BF16)
        inter = _dot((qc * dec_ref[hh, 3]).astype(BF16), st.astype(BF16))
        st = st * dec_ref[hh, 6] + _dot_tn((kc * dec_ref[hh, 4]).astype(BF16), vb)
        yield
        return inter, st

    def step(t, carry):
        fwd_rows = pl.ds(pl.multiple_of(t * c, c), c)
        bwd_rows = pl.ds(pl.multiple_of((n - 1 - t) * c, c), c)
        gens = []
        for hh in range(heads):
            gens.append(forward(hh, t, carry[2 * hh]))
            gens.append(backward(hh, n - 1 - t, carry[2 * hh + 1]))
        res = _lockstep(gens)
        for hh in range(heads):
            lanes = slice(hh * hd, (hh + 1) * hd)
            of_s[fwd_rows, lanes] = res[2 * hh][0]
            ob_s[bwd_rows, lanes] = res[2 * hh + 1][0]
        return tuple(st for _, st in res)

    zero = jnp.zeros((hd, hd), F32)
    lax.fori_loop(0, n, step, (zero,) * (2 * heads))

    for hh in range(heads):
        lanes = slice(hh * hd, (hh + 1) * hd)
        o = of_s[:, lanes] + ob_s[:, lanes]
        mu = jnp.mean(o, axis=-1, keepdims=True)
        var = jnp.mean(jnp.square(o - mu), axis=-1, keepdims=True)
        y = (o - mu) * lax.rsqrt(var + EPS)
        y = y * rn_ref[:, lanes] * jax.nn.silu(g_ref[:, lanes])
        o_ref[:, lanes] = y.astype(o_ref.dtype)


def _deinterleave_rotary_columns(w):
    n_qk = 2 * RET_HEADS * HEAD_DIM
    qk = w[:, :n_qk].reshape(w.shape[0], 2 * RET_HEADS, HEAD_DIM // 2, 2)
    qk = jnp.swapaxes(qk, 2, 3).reshape(w.shape[0], n_qk)
    return jnp.concatenate([qk, w[:, n_qk:]], axis=1)


def _retention_tables(s_len):
    d = HEAD_DIM
    inv = ROPE_BASE ** (-jnp.arange(0, d, 2, dtype=F32) / d)
    ang = jnp.arange(s_len, dtype=F32)[:, None] * inv[None, :]
    cos = jnp.concatenate([jnp.cos(ang), jnp.cos(ang)], axis=1)
    sin = jnp.concatenate([-jnp.sin(ang), jnp.sin(ang)], axis=1)
    h = jnp.arange(RET_HEADS, dtype=F32)
    lg_f = jnp.log1p(-jnp.exp2(-5.0 - h))[:, None, None]
    lg_b = jnp.log1p(-jnp.exp2(-5.5 - h))[:, None, None]
    c = RET_CHUNK
    pos = jnp.arange(c, dtype=F32)
    diff = (pos[:, None] - pos[None, :])[None]
    dmat = jnp.where(diff >= 0, jnp.exp(lg_f * jnp.where(diff >= 0, diff, 0.0)),
                     jnp.exp(lg_b * jnp.where(diff < 0, -diff, 0.0)))
    col = lambda v: jnp.broadcast_to(v, (RET_HEADS, c, d))
    p = pos[None, :, None]
    dec = jnp.stack([
        dmat,
        col(jnp.exp(lg_f * (p + 1.0))),
        col(jnp.exp(lg_f * (c - 1.0 - p))),
        col(jnp.exp(lg_b * (c - p))),
        col(jnp.exp(lg_b * p)),
        col(jnp.exp(lg_f * c)),
        col(jnp.exp(lg_b * c)),
    ], axis=1)
    return cos, sin, dec


def retention(proj, ret_norm, b, s_len, *, heads):
    cos, sin, dec = _retention_tables(s_len)
    hd = HEAD_DIM
    w = heads * hd
    ng = RET_HEADS // heads
    head_blk = lambda off: pl.BlockSpec((None, s_len, w), lambda i, h: (i, 0, off + h))
    return pl.pallas_call(
        _retention_body,
        out_shape=jax.ShapeDtypeStruct((b, s_len, RET_HEADS * hd), BF16),
        grid=(b, ng),
        in_specs=[head_blk(0), head_blk(ng), head_blk(2 * ng), head_blk(3 * ng),
                  pl.BlockSpec((s_len, hd), lambda i, h: (0, 0)),
                  pl.BlockSpec((s_len, hd), lambda i, h: (0, 0)),
                  pl.BlockSpec((heads, 7, RET_CHUNK, hd), lambda i, h: (h, 0, 0, 0)),
                  pl.BlockSpec((1, w), lambda i, h: (0, h))],
        out_specs=pl.BlockSpec((None, s_len, w), lambda i, h: (i, 0, h)),
        scratch_shapes=[pltpu.VMEM((s_len, w), F32)] * 4,
        compiler_params=_params(2),
        name="retention",
    )(proj, proj, proj, proj, cos, sin, dec, ret_norm.reshape(1, -1))


def _gdn_gates_body(raw_ref, al_ref, dt_ref, o_ref):
    s_len = raw_ref.shape[-1]
    nh = 2 * GDN_HEADS
    gb = raw_ref[0:nh, :]
    ga = raw_ref[nh:2 * nh, :]
    beta = jax.nn.sigmoid(gb)
    g = -jnp.exp(al_ref[...]) * jax.nn.softplus(ga + dt_ref[...])
    pos = lax.broadcasted_iota(jnp.int32, (nh, s_len), 1) & (GDN_CHUNK - 1)
    fwd = g
    rev = g
    k = 1
    while k < GDN_CHUNK:
        fwd = fwd + jnp.where(pos >= k, pltpu.roll(fwd, k, 1), 0.0)
        rev = rev + jnp.where(pos < GDN_CHUNK - k, pltpu.roll(rev, s_len - k, 1), 0.0)
        k *= 2
    row = lax.broadcasted_iota(jnp.int32, (nh, s_len), 0)
    o_ref[0:nh, :] = jnp.where(row < GDN_HEADS, fwd, rev)
    o_ref[nh:2 * nh, :] = beta
    o_ref[2 * nh:3 * nh, :] = fwd + rev - g


def gdn_gates(raw, a_log, dt_bias, b, s_len):
    nh = 2 * GDN_HEADS
    return pl.pallas_call(
        _gdn_gates_body,
        out_shape=jax.ShapeDtypeStruct((b, 3 * nh, s_len), F32),
        grid=(b,),
        in_specs=[pl.BlockSpec((2 * nh, s_len), lambda i: (0, i)),
                  pl.BlockSpec((nh, 1), lambda i: (0, 0)),
                  pl.BlockSpec((nh, 1), lambda i: (0, 0))],
        out_specs=pl.BlockSpec((None, 3 * nh, s_len), lambda i: (i, 0, 0)),
        compiler_params=_params(1),
        name="gdn_gates",
    )(raw, a_log.reshape(nh, 1), dt_bias.reshape(nh, 1))


def _gdn_prep_body(x_ref, w_ref, o_ref):
    s_len = x_ref.shape[0]
    j = pl.program_id(1)
    x = x_ref[...]
    t = lax.broadcasted_iota(jnp.int32, x.shape, 0)
    prev = jnp.where(t == 0, 0.0, pltpu.roll(x, 1, 0))
    nxt = jnp.where(t == s_len - 1, 0.0, pltpu.roll(x, s_len - 1, 0))
    conv = prev * w_ref[0:1, :] + x * w_ref[1:2, :] + nxt * w_ref[2:3, :]
    act = jax.nn.silu(conv)
    inv = lax.rsqrt(jnp.sum(act * act, axis=-1, keepdims=True) + EPS)
    is_q = j < GDN_HEADS
    is_v = j >= 2 * GDN_HEADS
    scale = jnp.where(is_v, 1.0, jnp.where(is_q, inv * (HEAD_DIM ** -0.5), inv))
    o_ref[...] = act * scale


def gdn_prep(proj, conv_w, b, s_len):
    hd = HEAD_DIM
    nblk = 3 * GDN_HEADS
    first = 4 * RET_HEADS
    return pl.pallas_call(
        _gdn_prep_body,
        out_shape=jax.ShapeDtypeStruct((b, s_len, nblk * hd), F32),
        grid=(b, nblk),
        in_specs=[pl.BlockSpec((None, s_len, hd), lambda i, j: (i, 0, first + j)),
                  pl.BlockSpec((3, hd), lambda i, j: (0, j))],
        out_specs=pl.BlockSpec((None, s_len, hd), lambda i, j: (i, 0, j)),
        compiler_params=_params(2),
        name="gdn_prep",
    )(proj, conv_w)


def _lockstep(gens):
    results = [None] * len(gens)
    live = list(range(len(gens)))
    while live:
        for i in list(live):
            try:
                next(gens[i])
            except StopIteration as done:
                results[i] = done.value
                live.remove(i)
    return results


def _merge_run(a, b):
    ia = ib = 0
    while ia < len(a) or ib < len(b):
        if ib >= len(b) or (ia < len(a) and (ia + 1) * len(b) <= (ib + 1) * len(a)):
            a[ia]()
            ia += 1
        else:
            b[ib]()
            ib += 1


def _tri_inverse(m, ri, ci):
    def same_block(log2_size):
        return (ri >> log2_size) == (ci >> log2_size)

    base = GDN_INV_BASE_LOG2
    eye = (ri == ci).astype(F32)
    diag = jnp.where(same_block(base), -m, 0.0)
    nm = _split(diag)
    p = eye + diag
    for _ in range(base - 1):
        nm = _split(_dot3(nm, nm))
        yield
        p = p + _dot3(_split(p), nm)
        yield
    for size in range(base, GDN_CHUNK.bit_length() - 1):
        off = jnp.where(same_block(size + 1) & jnp.logical_not(same_block(size)), m, 0.0)
        p16 = p.astype(BF16)
        pc = _dot(p16, off.astype(BF16)).astype(BF16)
        yield
        p = p - _dot(pc, p16)
        yield
    return p


def _gdn_super_chunk(q, k, v, gcol, grow, bcol, glcol, st, rev):
    r = GDN_SUPER
    c = GDN_CHUNK
    ri = lax.broadcasted_iota(jnp.int32, (r, r), 0)
    ci = lax.broadcasted_iota(jnp.int32, (r, r), 1)
    same = (ri >> 6) == (ci >> 6)
    if rev:
        incl = same & (ri <= ci)
        strict = same & (ri < ci)
    else:
        incl = same & (ri >= ci)
        strict = same & (ri > ci)
    decay = jnp.exp(jnp.where(incl, gcol - grow, -jnp.inf))
    kb = k * bcol
    vb = v * bcol
    k16 = k.astype(BF16)
    m = jnp.where(strict, _dot_nt(kb.astype(BF16), k16) * decay, 0.0)
    qk = jnp.where(incl, _dot_nt(q.astype(BF16), k16) * decay, 0.0).astype(BF16)
    yield
    t = yield from _tri_inverse(m, ri, ci)
    eg = jnp.exp(gcol)
    rhs = jnp.concatenate([vb, kb * eg], axis=1).astype(BF16)
    uw = _dot(t.astype(BF16), rhs)
    u = uw[:, :HEAD_DIM]
    w = uw[:, HEAD_DIM:].astype(BF16)
    qd = (q * eg).astype(BF16)
    kt = (k * jnp.exp(glcol - gcol)).astype(BF16)
    yield
    outs = [None] * (r // c)
    order = range(r // c - 1, -1, -1) if rev else range(r // c)
    for i in order:
        sl = slice(i * c, (i + 1) * c)
        s16 = st.astype(BF16)
        vn = u[sl] - _dot(w[sl], s16)
        vn16 = vn.astype(BF16)
        yield
        pair = slice((i // 2) * 2 * c, (i // 2 + 1) * 2 * c)
        zero = jnp.zeros_like(vn16)
        vpad = jnp.concatenate([zero, vn16] if i % 2 else [vn16, zero], axis=0)
        outs[i] = _dot(qd[sl], s16) + _dot(qk[sl, pair], vpad)
        st = st * jnp.exp(glcol[i * c:i * c + 1, :]) + _dot_tn(kt[sl], vn16)
        yield
    return jnp.concatenate(outs, axis=0), st


def _gdn_core_body(q_ref, k_ref, v_ref, z_ref, col_ref, row_ref, gn_ref, o_ref, of_s, ob_s):
    s_len = q_ref.shape[0]
    hd = HEAD_DIM
    heads = q_ref.shape[1] // hd
    r = GDN_SUPER
    n = s_len // r

    def one(hh, i, st, rev):
        base = 3 if rev else 0
        rows = pl.ds(pl.multiple_of(i * r, r), r)
        lanes = slice(hh * hd, (hh + 1) * hd)
        cols = col_ref[hh, rows, :]
        gcol = cols[:, base:base + 1]
        bcol = cols[:, base + 1:base + 2]
        glcol = cols[:, base + 2:base + 3]
        grow = row_ref[hh, i][base:base + 1, :]
        return _gdn_super_chunk(q_ref[rows, lanes], k_ref[rows, lanes], v_ref[rows, lanes],
                                gcol, grow, bcol, glcol, st, rev)

    def step(t, carry):
        fwd_rows = pl.ds(pl.multiple_of(t * r, r), r)
        bwd_rows = pl.ds(pl.multiple_of((n - 1 - t) * r, r), r)
        gens = []
        for hh in range(heads):
            gens.append(one(hh, t, carry[2 * hh], False))
            gens.append(one(hh, n - 1 - t, carry[2 * hh + 1], True))
        res = _lockstep(gens)
        for hh in range(heads):
            lanes = slice(hh * hd, (hh + 1) * hd)
            of_s[fwd_rows, lanes] = res[2 * hh][0]
            ob_s[bwd_rows, lanes] = res[2 * hh + 1][0]
        return tuple(st for _, st in res)

    zero = jnp.zeros((hd, hd), F32)
    lax.fori_loop(0, n, step, (zero,) * (2 * heads))
    for hh in range(heads):
        lanes = slice(hh * hd, (hh + 1) * hd)
        o = of_s[:, lanes] + ob_s[:, lanes]
        y = o * lax.rsqrt(jnp.mean(o * o, axis=-1, keepdims=True) + EPS) * gn_ref[...]
        o_ref[:, lanes] = (y * jax.nn.silu(z_ref[:, lanes])).astype(o_ref.dtype)


def gdn_core(qkv, proj, gate_col, gate_row, gdn_norm, b, s_len, *, heads):
    hd = HEAD_DIM
    w = heads * hd
    ng = GDN_HEADS // heads
    z_first = (4 * RET_HEADS + 3 * GDN_HEADS) // heads
    qkv_blk = lambda off: pl.BlockSpec((None, s_len, w), lambda i, h: (i, 0, off + h))
    return pl.pallas_call(
        _gdn_core_body,
        out_shape=jax.ShapeDtypeStruct((b, s_len, GDN_HEADS * hd), BF16),
        grid=(b, ng),
        in_specs=[qkv_blk(0), qkv_blk(ng), qkv_blk(2 * ng),
                  pl.BlockSpec((None, s_len, w), lambda i, h: (i, 0, z_first + h)),
                  pl.BlockSpec((None, heads, s_len, 8), lambda i, h: (i, h, 0, 0)),
                  pl.BlockSpec((None, heads, s_len // GDN_SUPER, 8, GDN_SUPER),
                               lambda i, h: (i, h, 0, 0, 0)),
                  pl.BlockSpec((1, hd), lambda i, h: (0, 0))],
        out_specs=pl.BlockSpec((None, s_len, w), lambda i, h: (i, 0, h)),
        scratch_shapes=[pltpu.VMEM((s_len, w), F32)] * 2,
        compiler_params=_params(2),
        name="gdn_core",
    )(qkv, qkv, qkv, proj, gate_col, gate_row, gdn_norm.reshape(1, hd))


def _mixer_xattn_body(x_ref, a1_ref, a2_ref, w1_ref, w2_ref, gxa_ref, wq_ref, kv_ref, wo_ref,
                      gffn_ref, wpq_ref, o_ref, pq_ref):
    x = x_ref[...] + _dot(a1_ref[...], w1_ref[...]) + _dot(a2_ref[...], w2_ref[...])
    d = x.shape[1]
    dh = d // XA_HEADS
    q = _dot(_rms(x, gxa_ref[...]).astype(BF16), wq_ref[...])
    outs = []
    for h in range(XA_HEADS):
        kh = kv_ref[:, h * dh:(h + 1) * dh].astype(BF16)
        vh = kv_ref[:, d + h * dh:d + (h + 1) * dh].astype(BF16)
        sc = _dot_nt(q[:, h * dh:(h + 1) * dh].astype(BF16), kh) * (dh ** -0.5)
        p = jax.nn.softmax(sc, axis=-1)
        outs.append(_dot(p.astype(BF16), vh).astype(BF16))
    x = x + _dot(jnp.concatenate(outs, axis=1), wo_ref[...])
    o_ref[...] = x
    pq_ref[...] = _dot(_rms(x, gffn_ref[...]).astype(BF16), wpq_ref[...])


def mixer_xattn_peerq(x, a1, a2, w1, w2, g_xa, wq, kv, wo, g_ffn, wpq, b, s_len, *, ts):
    t, d = x.shape
    k1, k2 = a1.shape[1], a2.shape[1]
    m = kv.shape[1]
    npq = wpq.shape[1]
    nblk = s_len // ts
    tok = lambda n: pl.BlockSpec((ts, n), lambda i, j: (i * nblk + j, 0))
    whole = lambda r, c: pl.BlockSpec((r, c), lambda i, j: (0, 0))
    return pl.pallas_call(
        _mixer_xattn_body,
        out_shape=(jax.ShapeDtypeStruct((t, d), F32), jax.ShapeDtypeStruct((t, npq), F32)),
        grid=(b, nblk),
        in_specs=[tok(d), tok(k1), tok(k2), whole(k1, d), whole(k2, d),
                  whole(1, d), whole(d, d),
                  pl.BlockSpec((None, m, 2 * d), lambda i, j: (i, 0, 0)),
                  whole(d, d), whole(1, d), whole(d, npq)],
        out_specs=(tok(d), tok(npq)),
        compiler_params=_params(2),
        name="mixer_xattn_peerq",
    )(x, a1, a2, w1, w2, g_xa.reshape(1, d), wq, kv, wo, g_ffn.reshape(1, d), wpq)


def _extract_topk(s, k, break_ties):
    n_rows = s.shape[0]
    iota = lax.broadcasted_iota(jnp.int32, s.shape, 0).astype(F32)
    rank = jnp.full(s.shape, float(k), F32)
    vals = []
    for r in range(k):
        m = jnp.max(s, axis=0, keepdims=True)
        hit = s == m
        if break_ties:
            idx = jnp.min(jnp.where(hit, iota, float(n_rows)), axis=0, keepdims=True)
            hit = iota == idx
        rank = jnp.where(hit, float(r), rank)
        s = jnp.where(hit, -jnp.inf, s)
        vals.append(m)
    count = jnp.sum(jnp.where(rank < float(k), 1.0, 0.0), axis=0, keepdims=True)
    return rank, vals, count


def _route_tokens(q, keys_ref, break_ties):
    nk = PEER_NKEYS
    kk = PEER_TOPK
    s1 = _dot_nt(keys_ref[0], q[:, :nk])
    s2 = _dot_nt(keys_ref[1], q[:, nk:])
    r1, v1, n_sel1 = _extract_topk(s1, kk, break_ties)
    r2, v2, n_sel2 = _extract_topk(s2, kk, break_ties)
    v1m = jnp.concatenate(v1, axis=0)
    v2m = jnp.concatenate(v2, axis=0)
    row8 = lax.broadcasted_iota(jnp.int32, (8, LANES), 0)
    groups = [v1[0] + v2m[0:8], v1[0] + v2m[8:16]]
    for a in range(1, 8):
        groups.append(jnp.where(row8 < kk // (a + 1), v1[a] + v2m[0:8], -jnp.inf))
    groups.append(v1m[8:16] + v2[0])
    cand = jnp.concatenate(groups, axis=0)
    rc, _, n_selc = _extract_topk(cand, kk, break_ties)
    sel = rc < float(kk)
    top = v1[0] + v2[0]
    z = jnp.sum(jnp.where(sel, jnp.exp(cand - top), 0.0), axis=0, keepdims=True)
    sel_f = jnp.where(sel, 1.0, 0.0)
    n1 = jnp.zeros((nk, LANES), F32)
    for a in range(kk):
        if a == 0:
            cnt = jnp.sum(sel_f[0:16], axis=0, keepdims=True)
        elif a < 8:
            cnt = jnp.sum(sel_f[8 + 8 * a:16 + 8 * a], axis=0, keepdims=True)
        else:
            cnt = sel_f[64 + a:65 + a]
        n1 = jnp.where(r1 == float(a), cnt, n1)
    c1 = jnp.where(r1 < float(kk), jnp.exp(s1 - v1[0]) / z, 0.0)
    e2 = jnp.exp(s2 - v2[0])
    exact = (n_sel1 == float(kk)) & (n_sel2 == float(kk)) & (n_selc == float(kk))
    return c1, n1, e2, r2, jnp.where(exact, 1.0, 0.0)


def _peer_route_body(q_ref, keys_ref, c1_ref, n1_ref, e2_ref, r2_ref):
    tb = q_ref.shape[0]

    def run(break_ties):
        flags = []
        for blk in range(tb // LANES):
            tok = slice(blk * LANES, (blk + 1) * LANES)
            c1, n1, e2, r2, ok = _route_tokens(q_ref[tok, :].astype(BF16), keys_ref, break_ties)
            c1_ref[:, tok] = c1
            n1_ref[:, tok] = n1
            e2_ref[:, tok] = e2.astype(e2_ref.dtype)
            r2_ref[:, tok] = r2.astype(r2_ref.dtype)
            flags.append(ok)
        return jnp.min(jnp.concatenate(flags, axis=1))

    all_exact = run(False)

    @pl.when(all_exact < 0.5)
    def _():
        run(True)


def peer_route(q, keys, *, tb):
    t = q.shape[0]
    nk = PEER_NKEYS
    out = lambda dt: jax.ShapeDtypeStruct((PEER_HEADS, nk, t), dt)
    ospec = pl.BlockSpec((None, nk, tb), lambda i, h: (h, 0, i))
    return pl.pallas_call(
        _peer_route_body,
        out_shape=(out(F32), out(F32), out(BF16), out(BF16)),
        grid=(t // tb, PEER_HEADS),
        in_specs=[pl.BlockSpec((tb, 2 * nk), lambda i, h: (i, h)),
                  pl.BlockSpec((None, 2, nk, nk), lambda i, h: (h, 0, 0, 0))],
        out_specs=(ospec,) * 4,
        compiler_params=_params(2),
        name="peer_route",
    )(q, keys)


def _gelu_tanh(x):
    c0 = 0.7978845608028654
    c1 = 0.7978845608028654 * 0.044715
    return (0.5 * x) * (1.0 + jnp.tanh(x * (c0 + c1 * (x * x))))


def _peer_dense_body(x_ref, g_ref, c1_ref, n1_ref, e2_ref, r2_ref, u_ref, vt_ref, *rest, final):
    if final:
        gf_ref, o_ref, xn_s, acc_s = rest
    else:
        o_ref, xn_s, acc_s = rest
    j = pl.program_id(1)
    nk = PEER_NKEYS
    tb = x_ref.shape[0]
    grp = u_ref.shape[0] // nk
    sub = e2_ref.shape[2]

    @pl.when(j == 0)
    def _():
        xn_s[...] = _rms(x_ref[...], g_ref[...]).T.astype(BF16)
        acc_s[...] = jnp.zeros_like(acc_s)

    per = MXU_DEPTH // nk
    n_chunks = grp // per
    d_model = vt_ref.shape[0]
    xn = xn_s[...]
    scores, gates, weights, chunk_w = {}, {}, {}, {}

    def score_pieces(c):
        def piece(ii):
            scores[ii] = _dot(u_ref[ii * nk:(ii + 1) * nk, :], xn)
        return [functools.partial(piece, c * per + i) for i in range(per)]

    def gate_pieces(c):
        def piece(ii, h):
            row = pl.ds(j * grp + ii, 1)
            c1 = jnp.broadcast_to(c1_ref[h, row, :], (sub, tb)).astype(e2_ref.dtype)[None]
            n1 = jnp.broadcast_to(n1_ref[h, row, :], (sub, tb)).astype(e2_ref.dtype)[None]
            e2 = e2_ref[h]
            term = jnp.where(r2_ref[h] < n1, e2, jnp.zeros_like(e2)) * c1
            gates[ii] = term if h == 0 else gates[ii] + term
        return [functools.partial(piece, c * per + i, h) for i in range(per) for h in range(PEER_HEADS)]

    def act_pieces(c):
        def piece(ii):
            act = _gelu_tanh(scores.pop(ii)).astype(e2_ref.dtype)
            weights[ii] = act * gates.pop(ii).reshape(nk, tb)
        return [functools.partial(piece, c * per + i) for i in range(per)]

    def value_pieces(c):
        ex = slice(c * MXU_DEPTH, (c + 1) * MXU_DEPTH)

        def piece(m):
            if m == 0:
                chunk_w[c] = jnp.concatenate([weights.pop(c * per + i) for i in range(per)], axis=0)
            rows = slice(m * MXU_DEPTH, (m + 1) * MXU_DEPTH)
            acc_s[rows, :] += _dot(vt_ref[rows, ex], chunk_w[c])
        return [functools.partial(piece, m) for m in range(d_model // MXU_DEPTH)]

    _merge_run(score_pieces(0), gate_pieces(0))
    for c in range(n_chunks):
        mxu = (score_pieces(c + 1) if c + 1 < n_chunks else []) + (value_pieces(c - 1) if c else [])
        vpu = act_pieces(c) + (gate_pieces(c + 1) if c + 1 < n_chunks else [])
        _merge_run(mxu, vpu)
    _merge_run(value_pieces(n_chunks - 1), [])

    @pl.when(j == pl.num_programs(1) - 1)
    def _():
        y = x_ref[...] + acc_s[...].T
        if final:
            y = _rms(y, gf_ref[...])
        o_ref[...] = y


def peer_dense_residual(x, g, route, u, vt, g_final, *, tb, grp, blocks=None):
    t, d = x.shape
    nk = PEER_NKEYS
    eb = grp * nk
    first, count = blocks if blocks is not None else (0, t // tb)
    c1, n1, e2, r2 = route
    sub = 32 // jnp.dtype(e2.dtype).itemsize
    e2 = e2.reshape(PEER_HEADS, nk // sub, sub, t)
    r2 = r2.reshape(PEER_HEADS, nk // sub, sub, t)
    rspec = pl.BlockSpec((PEER_HEADS, nk, tb), lambda i, j: (0, 0, first + i))
    pspec = pl.BlockSpec((PEER_HEADS, nk // sub, sub, tb), lambda i, j: (0, 0, 0, first + i))
    in_specs = [pl.BlockSpec((tb, d), lambda i, j: (first + i, 0)),
                pl.BlockSpec((1, d), lambda i, j: (0, 0)),
                rspec, rspec, pspec, pspec,
                pl.BlockSpec((eb, d), lambda i, j: (j, 0)),
                pl.BlockSpec((d, eb), lambda i, j: (0, j))]
    args = [x, g.reshape(1, d), c1, n1, e2, r2, u, vt]
    final = g_final is not None
    if final:
        in_specs.append(pl.BlockSpec((1, d), lambda i, j: (0, 0)))
        args.append(g_final.reshape(1, d))
    return pl.pallas_call(
        functools.partial(_peer_dense_body, final=final),
        out_shape=jax.ShapeDtypeStruct((count * tb, d), F32),
        grid=(count, (nk * nk) // eb),
        in_specs=in_specs,
        out_specs=pl.BlockSpec((tb, d), lambda i, j: (i, 0)),
        scratch_shapes=[pltpu.VMEM((d, tb), BF16),
                        pltpu.VMEM((d, tb), F32)],
        compiler_params=_params(2),
        name="peer_dense_final" if final else "peer_dense",
    )(*args)


def _pick(n, prefs):
    for p in prefs:
        if n % p == 0:
            return p
    return n


def _encoder(group_batches, x3, mem3, norm_mix, w_in, ret_norm, gdn_conv, gdn_a_log, gdn_dt_bias,
             gdn_norm, w_out, norm_xa, norm_mem, w_xq, w_xkv, w_xo, norm_ffn, peer_wq, peer_keys,
             peer_u, peer_v, norm_final):
    b, s_len, d = x3.shape
    n_mem = mem3.shape[1]
    t = b * s_len
    depth = w_in.shape[0]
    x = x3.reshape(t, d)
    mem = mem3.reshape(b * n_mem, d)
    n_main = 4 * RET_HEADS * HEAD_DIM + 4 * GDN_HEADS * HEAD_DIM
    tm = _pick(t, (1024, 512, 256, 128))
    tmm = _pick(b * n_mem, (1024, 512, 256, 128))
    ts = _pick(s_len, (512, 256, 128))
    tb_route = _pick(t, (256, 128))
    tb_dense = _pick(s_len, (512, 256, 128))
    nh = 2 * GDN_HEADS

    for l in range(depth):
        w_main = _deinterleave_rotary_columns(w_in[l, :, :n_main]).astype(BF16)
        w_gate_t = w_in[l, :, n_main:].T.astype(BF16)
        proj, graw = norm_matmul_side(x, norm_mix[l], w_main, w_gate_t, tm=tm, tn=1024)
        proj3 = proj.reshape(b, s_len, n_main)
        o_r = retention(proj3, ret_norm[l], b, s_len, heads=RET_HEADS_PER_STEP)
        gates = gdn_gates(graw, gdn_a_log[l], gdn_dt_bias[l], b, s_len)
        gates = gates.reshape(b, 3, 2, GDN_HEADS, s_len).transpose(0, 3, 2, 1, 4)
        gate_row = jnp.pad(gates.reshape(b, GDN_HEADS, 6, s_len), ((0, 0), (0, 0), (0, 2), (0, 0)))
        gate_col = jnp.swapaxes(gate_row, 2, 3)
        gate_row = gate_row.reshape(b, GDN_HEADS, 8, s_len // GDN_SUPER, GDN_SUPER).transpose(0, 1, 3, 2, 4)
        qkv = gdn_prep(proj3, gdn_conv[l], b, s_len)
        o_g = gdn_core(qkv, proj3, gate_col, gate_row, gdn_norm[l], b, s_len, heads=GDN_HEADS_PER_STEP)
        w_o = w_out[l].astype(BF16)
        n_r = RET_HEADS * HEAD_DIM

        kv = norm_matmul(mem, norm_mem[l], w_xkv[l].astype(BF16), tm=tmm, tn=1024)
        x, pq = mixer_xattn_peerq(x, o_r.reshape(t, -1), o_g.reshape(t, -1), w_o[:n_r], w_o[n_r:],
                                  norm_xa[l], w_xq[l].astype(BF16), kv.reshape(b, n_mem, 2 * d),
                                  w_xo[l].astype(BF16), norm_ffn[l], peer_wq[l].astype(BF16),
                                  b, s_len, ts=ts)

        route = peer_route(pq, peer_keys[l].astype(BF16), tb=tb_route)
        dense = functools.partial(peer_dense_residual, x, norm_ffn[l], route, peer_u[l].astype(BF16),
                                  peer_v[l].astype(BF16).T, tb=tb_dense, grp=16)
        if l < depth - 1:
            x = dense(None)
    outs, first = [], 0
    for nb in group_batches:
        n_blocks = nb * s_len // tb_dense
        outs.append(dense(norm_final, blocks=(first, n_blocks)).reshape(nb, s_len, d))
        first += n_blocks
    return tuple(outs)


def kernel(x_prompt, x_sample, mem_prompt, mem_sample, norm_mix, w_in, ret_norm, gdn_conv, gdn_a_log,
           gdn_dt_bias, gdn_norm, w_out, norm_xa, norm_mem, w_xq, w_xkv, w_xo, norm_ffn, peer_wq,
           peer_keys, peer_u, peer_v, norm_final):
    weights = (norm_mix, w_in, ret_norm, gdn_conv, gdn_a_log, gdn_dt_bias, gdn_norm, w_out,
               norm_xa, norm_mem, w_xq, w_xkv, w_xo, norm_ffn, peer_wq, peer_keys, peer_u, peer_v,
               norm_final)
    if x_prompt.shape[1:] == x_sample.shape[1:] and mem_prompt.shape[1:] == mem_sample.shape[1:]:
        return _encoder((x_prompt.shape[0], x_sample.shape[0]),
                        jnp.concatenate([x_prompt, x_sample], axis=0),
                        jnp.concatenate([mem_prompt, mem_sample], axis=0), *weights)
    return (_encoder((x_prompt.shape[0],), x_prompt, mem_prompt, *weights)[0],
            _encoder((x_sample.shape[0],), x_sample, mem_sample, *weights)[0])
```

```python
import functools

import jax
import jax.numpy as jnp
from jax import lax
from jax.experimental import pallas as pl
from jax.experimental.pallas import tpu as pltpu

F32 = jnp.float32
BF16 = jnp.bfloat16

EPS = 1e-6
LANES = 128
MXU_DEPTH = 256
HEAD_DIM = 128
RET_HEADS = 4
GDN_HEADS = 4
RET_CHUNK = 128
GDN_CHUNK = 64
GDN_SUPER = 256
GDN_INV_BASE_LOG2 = 1
GDN_HEADS_PER_STEP = 2
RET_HEADS_PER_STEP = 2
ROPE_BASE = 10000.0
XA_HEADS = 4
PEER_HEADS = 8
PEER_NKEYS = 128
PEER_TOPK = 16
TAKEN = 2.0 ** 100
TAKEN_STEP = 2.0 ** 96
FAST_PAD = -(2.0 ** 90)
FAST_SCORE_BOUND = 2.0 ** 80
VMEM_LIMIT = 56 * 1024 * 1024

_NT = (((1,), (1,)), ((), ()))
_TN = (((0,), (0,)), ((), ()))


def _params(n_axes):
    return pltpu.CompilerParams(
        dimension_semantics=("arbitrary",) * n_axes, vmem_limit_bytes=VMEM_LIMIT)


def _dot(a, b):
    return jnp.dot(a, b, preferred_element_type=F32)


def _dot_nt(a, b):
    return lax.dot_general(a, b, _NT, preferred_element_type=F32)


def _dot_tn(a, b):
    return lax.dot_general(a, b, _TN, preferred_element_type=F32)


def _split(a):
    hi = a.astype(BF16)
    return hi, (a - hi.astype(F32)).astype(BF16)


def _dot3(a, b):
    (ah, al), (bh, bl) = a, b
    return _dot(jnp.concatenate([ah, ah, al], axis=1), jnp.concatenate([bh, bl, bh], axis=0))


def _rms(x, g):
    return x * lax.rsqrt(jnp.mean(x * x, axis=-1, keepdims=True) + EPS) * g


def _norm_matmul_body(x_ref, g_ref, w_ref, o_ref, xn_ref):
    @pl.when(pl.program_id(1) == 0)
    def _():
        xn_ref[...] = _rms(x_ref[...], g_ref[...]).astype(BF16)

    o_ref[...] = _dot(xn_ref[...], w_ref[...]).astype(o_ref.dtype)


def norm_matmul(x, g, w, *, tm, tn, out_dtype=F32):
    t, d = x.shape
    n = w.shape[1]
    return pl.pallas_call(
        _norm_matmul_body,
        out_shape=jax.ShapeDtypeStruct((t, n), out_dtype),
        grid=(t // tm, n // tn),
        in_specs=[pl.BlockSpec((tm, d), lambda i, j: (i, 0)),
                  pl.BlockSpec((1, d), lambda i, j: (0, 0)),
                  pl.BlockSpec((d, tn), lambda i, j: (0, j))],
        out_specs=pl.BlockSpec((tm, tn), lambda i, j: (i, j)),
        scratch_shapes=[pltpu.VMEM((tm, d), BF16)],
        compiler_params=_params(2),
        name="norm_matmul",
    )(x, g.reshape(1, d), w)


def _norm_matmul_side_body(x_ref, g_ref, w_ref, ws_ref, o_ref, os_ref, xn_ref):
    @pl.when(pl.program_id(1) == 0)
    def _():
        xn = _rms(x_ref[...], g_ref[...]).astype(BF16)
        xn_ref[...] = xn
        os_ref[...] = _dot_nt(ws_ref[...], xn)

    o_ref[...] = _dot(xn_ref[...], w_ref[...])


def norm_matmul_side(x, g, w, ws_t, *, tm, tn):
    t, d = x.shape
    n = w.shape[1]
    rows = ws_t.shape[0]
    return pl.pallas_call(
        _norm_matmul_side_body,
        out_shape=(jax.ShapeDtypeStruct((t, n), F32),
                   jax.ShapeDtypeStruct((rows, t), F32)),
        grid=(t // tm, n // tn),
        in_specs=[pl.BlockSpec((tm, d), lambda i, j: (i, 0)),
                  pl.BlockSpec((1, d), lambda i, j: (0, 0)),
                  pl.BlockSpec((d, tn), lambda i, j: (0, j)),
                  pl.BlockSpec((rows, d), lambda i, j: (0, 0))],
        out_specs=(pl.BlockSpec((tm, tn), lambda i, j: (i, j)),
                   pl.BlockSpec((rows, tm), lambda i, j: (0, i))),
        scratch_shapes=[pltpu.VMEM((tm, d), BF16)],
        compiler_params=_params(2),
        name="norm_matmul_side",
    )(x, g.reshape(1, d), w, ws_t)


def _retention_body(q_ref, k_ref, v_ref, g_ref, cos_ref, sin_ref, dec_ref, rn_ref,
                    o_ref, qr_s, kr_s, of_s, ob_s):
    s_len = q_ref.shape[0]
    hd = HEAD_DIM
    heads = q_ref.shape[1] // hd
    c = RET_CHUNK
    n = s_len // c
    cos = cos_ref[...]
    sin = sin_ref[...]

    def rot(x):
        return x * cos + pltpu.roll(x, hd // 2, 1) * sin

    for hh in range(heads):
        lanes = slice(hh * hd, (hh + 1) * hd)
        qr_s[:, lanes] = rot(q_ref[:, lanes])
        kr_s[:, lanes] = rot(k_ref[:, lanes]) * (hd ** -0.5)

    def forward(hh, i, st):
        r = pl.ds(pl.multiple_of(i * c, c), c)
        lanes = slice(hh * hd, (hh + 1) * hd)
        qc, kc, vb = qr_s[r, lanes], kr_s[r, lanes], v_ref[r, lanes].astype(BF16)
        sc = _dot_nt(qc.astype(BF16), kc.astype(BF16)) * dec_ref[hh, 0]
        inter = _dot((qc * dec_ref[hh, 1]).astype(BF16), st.astype(BF16))
        st = st * dec_ref[hh, 5] + _dot_tn((kc * dec_ref[hh, 2]).astype(BF16), vb)
        yield
        return inter + _dot(sc.astype(BF16), vb), st

    def backward(hh, i, st):
        r = pl.ds(pl.multiple_of(i * c, c), c)
        lanes = slice(hh * hd, (hh + 1) * hd)
        qc, kc, vb = qr_s[r, lanes], kr_s[r, lanes], v_ref[r, lanes].astype(BF16)
        inter = _dot((qc * dec_ref[hh, 3]).astype(BF16), st.astype(BF16))
        st = st * dec_ref[hh, 6] + _dot_tn((kc * dec_ref[hh, 4]).astype(BF16), vb)
        yield
        return inter, st

    def step(t, carry):
        fwd_rows = pl.ds(pl.multiple_of(t * c, c), c)
        bwd_rows = pl.ds(pl.multiple_of((n - 1 - t) * c, c), c)
        gens = []
        for hh in range(heads):
            gens.append(forward(hh, t, carry[2 * hh]))
            gens.append(backward(hh, n - 1 - t, carry[2 * hh + 1]))
        res = _lockstep(gens)
        for hh in range(heads):
            lanes = slice(hh * hd, (hh + 1) * hd)
            of_s[fwd_rows, lanes] = res[2 * hh][0]
            ob_s[bwd_rows, lanes] = res[2 * hh + 1][0]
        return tuple(st for _, st in res)

    zero = jnp.zeros((hd, hd), F32)
    lax.fori_loop(0, n, step, (zero,) * (2 * heads))

    for hh in range(heads):
        lanes = slice(hh * hd, (hh + 1) * hd)
        o = of_s[:, lanes] + ob_s[:, lanes]
        mu = jnp.mean(o, axis=-1, keepdims=True)
        var = jnp.mean(jnp.square(o - mu), axis=-1, keepdims=True)
        y = (o - mu) * lax.rsqrt(var + EPS)
        y = y * rn_ref[:, lanes] * jax.nn.silu(g_ref[:, lanes])
        o_ref[:, lanes] = y.astype(o_ref.dtype)


def _deinterleave_rotary_columns(w):
    n_qk = 2 * RET_HEADS * HEAD_DIM
    qk = w[:, :n_qk].reshape(w.shape[0], 2 * RET_HEADS, HEAD_DIM // 2, 2)
    qk = jnp.swapaxes(qk, 2, 3).reshape(w.shape[0], n_qk)
    return jnp.concatenate([qk, w[:, n_qk:]], axis=1)


def _retention_tables(s_len):
    d = HEAD_DIM
    inv = ROPE_BASE ** (-jnp.arange(0, d, 2, dtype=F32) / d)
    ang = jnp.arange(s_len, dtype=F32)[:, None] * inv[None, :]
    cos = jnp.concatenate([jnp.cos(ang), jnp.cos(ang)], axis=1)
    sin = jnp.concatenate([-jnp.sin(ang), jnp.sin(ang)], axis=1)
    h = jnp.arange(RET_HEADS, dtype=F32)
    lg_f = jnp.log1p(-jnp.exp2(-5.0 - h))[:, None, None]
    lg_b = jnp.log1p(-jnp.exp2(-5.5 - h))[:, None, None]
    c = RET_CHUNK
    pos = jnp.arange(c, dtype=F32)
    diff = (pos[:, None] - pos[None, :])[None]
    dmat = jnp.where(diff >= 0, jnp.exp(lg_f * jnp.where(diff >= 0, diff, 0.0)),
                     jnp.exp(lg_b * jnp.where(diff < 0, -diff, 0.0)))
    col = lambda v: jnp.broadcast_to(v, (RET_HEADS, c, d))
    p = pos[None, :, None]
    dec = jnp.stack([
        dmat,
        col(jnp.exp(lg_f * (p + 1.0))),
        col(jnp.exp(lg_f * (c - 1.0 - p))),
        col(jnp.exp(lg_b * (c - p))),
        col(jnp.exp(lg_b * p)),
        col(jnp.exp(lg_f * c)),
        col(jnp.exp(lg_b * c)),
    ], axis=1)
    return cos, sin, dec


def retention(proj, ret_norm, b, s_len, *, heads):
    cos, sin, dec = _retention_tables(s_len)
    hd = HEAD_DIM
    w = heads * hd
    ng = RET_HEADS // heads
    head_blk = lambda off: pl.BlockSpec((None, s_len, w), lambda i, h: (i, 0, off + h))
    return pl.pallas_call(
        _retention_body,
        out_shape=jax.ShapeDtypeStruct((b, s_len, RET_HEADS * hd), BF16),
        grid=(b, ng),
        in_specs=[head_blk(0), head_blk(ng), head_blk(2 * ng), head_blk(3 * ng),
                  pl.BlockSpec((s_len, hd), lambda i, h: (0, 0)),
                  pl.BlockSpec((s_len, hd), lambda i, h: (0, 0)),
                  pl.BlockSpec((heads, 7, RET_CHUNK, hd), lambda i, h: (h, 0, 0, 0)),
                  pl.BlockSpec((1, w), lambda i, h: (0, h))],
        out_specs=pl.BlockSpec((None, s_len, w), lambda i, h: (i, 0, h)),
        scratch_shapes=[pltpu.VMEM((s_len, w), F32)] * 4,
        compiler_params=_params(2),
        name="retention",
    )(proj, proj, proj, proj, cos, sin, dec, ret_norm.reshape(1, -1))


def _gdn_gates_body(raw_ref, al_ref, dt_ref, o_ref):
    s_len = raw_ref.shape[-1]
    nh = 2 * GDN_HEADS
    gb = raw_ref[0:nh, :]
    ga = raw_ref[nh:2 * nh, :]
    beta = jax.nn.sigmoid(gb)
    g = -jnp.exp(al_ref[...]) * jax.nn.softplus(ga + dt_ref[...])
    pos = lax.broadcasted_iota(jnp.int32, (nh, s_len), 1) & (GDN_CHUNK - 1)
    fwd = g
    rev = g
    k = 1
    while k < GDN_CHUNK:
        fwd = fwd + jnp.where(pos >= k, pltpu.roll(fwd, k, 1), 0.0)
        rev = rev + jnp.where(pos < GDN_CHUNK - k, pltpu.roll(rev, s_len - k, 1), 0.0)
        k *= 2
    row = lax.broadcasted_iota(jnp.int32, (nh, s_len), 0)
    o_ref[0:nh, :] = jnp.where(row < GDN_HEADS, fwd, rev)
    o_ref[nh:2 * nh, :] = beta
    o_ref[2 * nh:3 * nh, :] = fwd + rev - g


def gdn_gates(raw, a_log, dt_bias, b, s_len):
    nh = 2 * GDN_HEADS
    return pl.pallas_call(
        _gdn_gates_body,
        out_shape=jax.ShapeDtypeStruct((b, 3 * nh, s_len), F32),
        grid=(b,),
        in_specs=[pl.BlockSpec((2 * nh, s_len), lambda i: (0, i)),
                  pl.BlockSpec((nh, 1), lambda i: (0, 0)),
                  pl.BlockSpec((nh, 1), lambda i: (0, 0))],
        out_specs=pl.BlockSpec((None, 3 * nh, s_len), lambda i: (i, 0, 0)),
        compiler_params=_params(1),
        name="gdn_gates",
    )(raw, a_log.reshape(nh, 1), dt_bias.reshape(nh, 1))


def _gdn_prep_body(x_ref, w_ref, o_ref):
    s_len = x_ref.shape[0]
    j = pl.program_id(1)
    x = x_ref[...]
    t = lax.broadcasted_iota(jnp.int32, x.shape, 0)
    prev = jnp.where(t == 0, 0.0, pltpu.roll(x, 1, 0))
    nxt = jnp.where(t == s_len - 1, 0.0, pltpu.roll(x, s_len - 1, 0))
    conv = prev * w_ref[0:1, :] + x * w_ref[1:2, :] + nxt * w_ref[2:3, :]
    act = jax.nn.silu(conv)
    inv = lax.rsqrt(jnp.sum(act * act, axis=-1, keepdims=True) + EPS)
    is_q = j < GDN_HEADS
    is_v = j >= 2 * GDN_HEADS
    scale = jnp.where(is_v, 1.0, jnp.where(is_q, inv * (HEAD_DIM ** -0.5), inv))
    o_ref[...] = act * scale


def gdn_prep(proj, conv_w, b, s_len):
    hd = HEAD_DIM
    nblk = 3 * GDN_HEADS
    first = 4 * RET_HEADS
    return pl.pallas_call(
        _gdn_prep_body,
        out_shape=jax.ShapeDtypeStruct((b, s_len, nblk * hd), F32),
        grid=(b, nblk),
        in_specs=[pl.BlockSpec((None, s_len, hd), lambda i, j: (i, 0, first + j)),
                  pl.BlockSpec((3, hd), lambda i, j: (0, j))],
        out_specs=pl.BlockSpec((None, s_len, hd), lambda i, j: (i, 0, j)),
        compiler_params=_params(2),
        name="gdn_prep",
    )(proj, conv_w)


def _lockstep(gens):
    results = [None] * len(gens)
    live = list(range(len(gens)))
    while live:
        for i in list(live):
            try:
                next(gens[i])
            except StopIteration as done:
                results[i] = done.value
                live.remove(i)
    return results


def _merge_run(a, b):
    ia = ib = 0
    while ia < len(a) or ib < len(b):
        if ib >= len(b) or (ia < len(a) and (ia + 1) * len(b) <= (ib + 1) * len(a)):
            a[ia]()
            ia += 1
        else:
            b[ib]()
            ib += 1


def _tri_inverse(m, ri, ci):
    def same_block(log2_size):
        return (ri >> log2_size) == (ci >> log2_size)

    base = GDN_INV_BASE_LOG2
    eye = (ri == ci).astype(F32)
    diag = jnp.where(same_block(base), -m, 0.0)
    nm = _split(diag)
    p = eye + diag
    for _ in range(base - 1):
        nm = _split(_dot3(nm, nm))
        yield
        p = p + _dot3(_split(p), nm)
        yield
    for size in range(base, GDN_CHUNK.bit_length() - 1):
        off = jnp.where(same_block(size + 1) & jnp.logical_not(same_block(size)), m, 0.0)
        p16 = p.astype(BF16)
        pc = _dot(p16, off.astype(BF16)).astype(BF16)
        yield
        p = p - _dot(pc, p16)
        yield
    return p


def _gdn_super_chunk(q, k, v, gcol, grow, bcol, glcol, st, rev):
    r = GDN_SUPER
    c = GDN_CHUNK
    ri = lax.broadcasted_iota(jnp.int32, (r, r), 0)
    ci = lax.broadcasted_iota(jnp.int32, (r, r), 1)
    same = (ri >> 6) == (ci >> 6)
    if rev:
        incl = same & (ri <= ci)
        strict = same & (ri < ci)
    else:
        incl = same & (ri >= ci)
        strict = same & (ri > ci)
    decay = jnp.exp(jnp.where(incl, gcol - grow, -jnp.inf))
    kb = k * bcol
    vb = v * bcol
    k16 = k.astype(BF16)
    m = jnp.where(strict, _dot_nt(kb.astype(BF16), k16) * decay, 0.0)
    qk = jnp.where(incl, _dot_nt(q.astype(BF16), k16) * decay, 0.0).astype(BF16)
    yield
    t = yield from _tri_inverse(m, ri, ci)
    eg = jnp.exp(gcol)
    rhs = jnp.concatenate([vb, kb * eg], axis=1).astype(BF16)
    uw = _dot(t.astype(BF16), rhs)
    u = uw[:, :HEAD_DIM]
    w = uw[:, HEAD_DIM:].astype(BF16)
    qd = (q * eg).astype(BF16)
    kt = (k * jnp.exp(glcol - gcol)).astype(BF16)
    yield
    outs = [None] * (r // c)
    order = range(r // c - 1, -1, -1) if rev else range(r // c)
    for i in order:
        sl = slice(i * c, (i + 1) * c)
        s16 = st.astype(BF16)
        vn = u[sl] - _dot(w[sl], s16)
        vn16 = vn.astype(BF16)
        yield
        pair = slice((i // 2) * 2 * c, (i // 2 + 1) * 2 * c)
        zero = jnp.zeros_like(vn16)
        vpad = jnp.concatenate([zero, vn16] if i % 2 else [vn16, zero], axis=0)
        outs[i] = _dot(qd[sl], s16) + _dot(qk[sl, pair], vpad)
        st = st * jnp.exp(glcol[i * c:i * c + 1, :]) + _dot_tn(kt[sl], vn16)
        yield
    return jnp.concatenate(outs, axis=0), st


def _gdn_core_body(q_ref, k_ref, v_ref, z_ref, col_ref, row_ref, gn_ref, o_ref, of_s, ob_s):
    s_len = q_ref.shape[0]
    hd = HEAD_DIM
    heads = q_ref.shape[1] // hd
    r = GDN_SUPER
    n = s_len // r

    def one(hh, i, st, rev):
        base = 3 if rev else 0
        rows = pl.ds(pl.multiple_of(i * r, r), r)
        lanes = slice(hh * hd, (hh + 1) * hd)
        cols = col_ref[hh, rows, :]
        gcol = cols[:, base:base + 1]
        bcol = cols[:, base + 1:base + 2]
        glcol = cols[:, base + 2:base + 3]
        grow = row_ref[hh, i][base:base + 1, :]
        return _gdn_super_chunk(q_ref[rows, lanes], k_ref[rows, lanes], v_ref[rows, lanes],
                                gcol, grow, bcol, glcol, st, rev)

    def step(t, carry):
        fwd_rows = pl.ds(pl.multiple_of(t * r, r), r)
        bwd_rows = pl.ds(pl.multiple_of((n - 1 - t) * r, r), r)
        gens = []
        for hh in range(heads):
            gens.append(one(hh, t, carry[2 * hh], False))
            gens.append(one(hh, n - 1 - t, carry[2 * hh + 1], True))
        res = _lockstep(gens)
        for hh in range(heads):
            lanes = slice(hh * hd, (hh + 1) * hd)
            of_s[fwd_rows, lanes] = res[2 * hh][0]
            ob_s[bwd_rows, lanes] = res[2 * hh + 1][0]
        return tuple(st for _, st in res)

    zero = jnp.zeros((hd, hd), F32)
    lax.fori_loop(0, n, step, (zero,) * (2 * heads))
    for hh in range(heads):
        lanes = slice(hh * hd, (hh + 1) * hd)
        o = of_s[:, lanes] + ob_s[:, lanes]
        y = o * lax.rsqrt(jnp.mean(o * o, axis=-1, keepdims=True) + EPS) * gn_ref[...]
        o_ref[:, lanes] = (y * jax.nn.silu(z_ref[:, lanes])).astype(o_ref.dtype)


def gdn_core(qkv, proj, gate_col, gate_row, gdn_norm, b, s_len, *, heads):
    hd = HEAD_DIM
    w = heads * hd
    ng = GDN_HEADS // heads
    z_first = (4 * RET_HEADS + 3 * GDN_HEADS) // heads
    qkv_blk = lambda off: pl.BlockSpec((None, s_len, w), lambda i, h: (i, 0, off + h))
    return pl.pallas_call(
        _gdn_core_body,
        out_shape=jax.ShapeDtypeStruct((b, s_len, GDN_HEADS * hd), BF16),
        grid=(b, ng),
        in_specs=[qkv_blk(0), qkv_blk(ng), qkv_blk(2 * ng),
                  pl.BlockSpec((None, s_len, w), lambda i, h: (i, 0, z_first + h)),
                  pl.BlockSpec((None, heads, s_len, 8), lambda i, h: (i, h, 0, 0)),
                  pl.BlockSpec((None, heads, s_len // GDN_SUPER, 8, GDN_SUPER),
                               lambda i, h: (i, h, 0, 0, 0)),
                  pl.BlockSpec((1, hd), lambda i, h: (0, 0))],
        out_specs=pl.BlockSpec((None, s_len, w), lambda i, h: (i, 0, h)),
        scratch_shapes=[pltpu.VMEM((s_len, w), F32)] * 2,
        compiler_params=_params(2),
        name="gdn_core",
    )(qkv, qkv, qkv, proj, gate_col, gate_row, gdn_norm.reshape(1, hd))


def _mixer_xattn_body(x_ref, a1_ref, a2_ref, w1_ref, w2_ref, gxa_ref, wq_ref, kv_ref, wo_ref,
                      gffn_ref, wpq_ref, o_ref, pq_ref):
    x = x_ref[...] + _dot(a1_ref[...], w1_ref[...]) + _dot(a2_ref[...], w2_ref[...])
    d = x.shape[1]
    dh = d // XA_HEADS
    q = _dot(_rms(x, gxa_ref[...]).astype(BF16), wq_ref[...])
    outs = []
    for h in range(XA_HEADS):
        kh = kv_ref[:, h * dh:(h + 1) * dh].astype(BF16)
        vh = kv_ref[:, d + h * dh:d + (h + 1) * dh].astype(BF16)
        sc = _dot_nt(q[:, h * dh:(h + 1) * dh].astype(BF16), kh) * (dh ** -0.5)
        p = jax.nn.softmax(sc, axis=-1)
        outs.append(_dot(p.astype(BF16), vh).astype(BF16))
    x = x + _dot(jnp.concatenate(outs, axis=1), wo_ref[...])
    o_ref[...] = x
    pq_ref[...] = _dot(_rms(x, gffn_ref[...]).astype(BF16), wpq_ref[...])


def mixer_xattn_peerq(x, a1, a2, w1, w2, g_xa, wq, kv, wo, g_ffn, wpq, b, s_len, *, ts):
    t, d = x.shape
    k1, k2 = a1.shape[1], a2.shape[1]
    m = kv.shape[1]
    npq = wpq.shape[1]
    nblk = s_len // ts
    tok = lambda n: pl.BlockSpec((ts, n), lambda i, j: (i * nblk + j, 0))
    whole = lambda r, c: pl.BlockSpec((r, c), lambda i, j: (0, 0))
    return pl.pallas_call(
        _mixer_xattn_body,
        out_shape=(jax.ShapeDtypeStruct((t, d), F32), jax.ShapeDtypeStruct((t, npq), F32)),
        grid=(b, nblk),
        in_specs=[tok(d), tok(k1), tok(k2), whole(k1, d), whole(k2, d),
                  whole(1, d), whole(d, d),
                  pl.BlockSpec((None, m, 2 * d), lambda i, j: (i, 0, 0)),
                  whole(d, d), whole(1, d), whole(d, npq)],
        out_specs=(tok(d), tok(npq)),
        compiler_params=_params(2),
        name="mixer_xattn_peerq",
    )(x, a1, a2, w1, w2, g_xa.reshape(1, d), wq, kv, wo, g_ffn.reshape(1, d), wpq)


def _extract_topk(s, k, break_ties):
    vals = []
    if break_ties:
        n_rows = s.shape[0]
        iota = lax.broadcasted_iota(jnp.int32, s.shape, 0).astype(F32)
        rank = jnp.full(s.shape, float(k), F32)
        for r in range(k):
            m = jnp.max(s, axis=0, keepdims=True)
            idx = jnp.min(jnp.where(s == m, iota, float(n_rows)), axis=0, keepdims=True)
            hit = iota == idx
            rank = jnp.where(hit, float(r), rank)
            s = jnp.where(hit, -jnp.inf, s)
            vals.append(m)
        taken = rank < float(k)
    else:
        for r in range(k):
            m = jnp.max(s, axis=0, keepdims=True)
            s = jnp.where(s == m, -(TAKEN + r * TAKEN_STEP), s)
            vals.append(m)
        taken = s <= -TAKEN
        rank = jnp.where(taken, s * (-1.0 / TAKEN_STEP) - TAKEN / TAKEN_STEP, float(k))
    count = jnp.sum(jnp.where(taken, 1.0, 0.0), axis=0, keepdims=True)
    return rank, vals, count


def _route_tokens(q, keys_ref, break_ties):
    nk = PEER_NKEYS
    kk = PEER_TOPK
    s1 = _dot_nt(keys_ref[0], q[:, :nk])
    s2 = _dot_nt(keys_ref[1], q[:, nk:])
    r1, v1, n_sel1 = _extract_topk(s1, kk, break_ties)
    r2, v2, n_sel2 = _extract_topk(s2, kk, break_ties)
    v1m = jnp.concatenate(v1, axis=0)
    v2m = jnp.concatenate(v2, axis=0)
    pad = -jnp.inf if break_ties else FAST_PAD
    row8 = lax.broadcasted_iota(jnp.int32, (8, LANES), 0)
    groups = [v1[0] + v2m[0:8], v1[0] + v2m[8:16]]
    for a in range(1, 8):
        groups.append(jnp.where(row8 < kk // (a + 1), v1[a] + v2m[0:8], pad))
    groups.append(v1m[8:16] + v2[0])
    cand = jnp.concatenate(groups, axis=0)
    rc, _, n_selc = _extract_topk(cand, kk, break_ties)
    sel = rc < float(kk)
    top = v1[0] + v2[0]
    z = jnp.sum(jnp.where(sel, jnp.exp(cand - top), 0.0), axis=0, keepdims=True)
    sel_f = jnp.where(sel, 1.0, 0.0)
    n1 = jnp.zeros((nk, LANES), F32)
    for a in range(kk):
        if a == 0:
            cnt = jnp.sum(sel_f[0:16], axis=0, keepdims=True)
        elif a < 8:
            cnt = jnp.sum(sel_f[8 + 8 * a:16 + 8 * a], axis=0, keepdims=True)
        else:
            cnt = sel_f[64 + a:65 + a]
        n1 = jnp.where(r1 == float(a), cnt, n1)
    c1 = jnp.where(r1 < float(kk), jnp.exp(s1 - v1[0]) / z, 0.0)
    e2 = jnp.exp(s2 - v2[0])
    exact = ((n_sel1 == float(kk)) & (n_sel2 == float(kk)) & (n_selc == float(kk))
             & (v1[kk - 1] > -FAST_SCORE_BOUND) & (v2[kk - 1] > -FAST_SCORE_BOUND))
    return c1, n1, e2, r2, jnp.where(exact, 1.0, 0.0)


def _peer_route_body(q_ref, keys_ref, c1_ref, n1_ref, e2_ref, r2_ref):
    tb = q_ref.shape[0]

    def run(break_ties):
        flags = []
        for blk in range(tb // LANES):
            tok = slice(blk * LANES, (blk + 1) * LANES)
            c1, n1, e2, r2, ok = _route_tokens(q_ref[tok, :].astype(BF16), keys_ref, break_ties)
            c1_ref[:, tok] = c1
            n1_ref[:, tok] = n1
            e2_ref[:, tok] = e2.astype(e2_ref.dtype)
            r2_ref[:, tok] = r2.astype(r2_ref.dtype)
            flags.append(ok)
        return jnp.min(jnp.concatenate(flags, axis=1))

    all_exact = run(False)

    @pl.when(all_exact < 0.5)
    def _():
        run(True)


def peer_route(q, keys, *, tb):
    t = q.shape[0]
    nk = PEER_NKEYS
    out = lambda dt: jax.ShapeDtypeStruct((PEER_HEADS, nk, t), dt)
    ospec = pl.BlockSpec((None, nk, tb), lambda i, h: (h, 0, i))
    return pl.pallas_call(
        _peer_route_body,
        out_shape=(out(F32), out(F32), out(BF16), out(BF16)),
        grid=(t // tb, PEER_HEADS),
        in_specs=[pl.BlockSpec((tb, 2 * nk), lambda i, h: (i, h)),
                  pl.BlockSpec((None, 2, nk, nk), lambda i, h: (h, 0, 0, 0))],
        out_specs=(ospec,) * 4,
        compiler_params=_params(2),
        name="peer_route",
    )(q, keys)


def _gelu_tanh(x):
    c0 = 0.7978845608028654
    c1 = 0.7978845608028654 * 0.044715
    return (0.5 * x) * (1.0 + jnp.tanh(x * (c0 + c1 * (x * x))))


def _peer_dense_body(x_ref, g_ref, c1_ref, n1_ref, e2_ref, r2_ref, u_ref, vt_ref, *rest, final):
    if final:
        gf_ref, o_ref, xn_s, acc_s = rest
    else:
        o_ref, xn_s, acc_s = rest
    j = pl.program_id(1)
    nk = PEER_NKEYS
    tb = x_ref.shape[0]
    grp = u_ref.shape[0] // nk
    sub = e2_ref.shape[2]

    @pl.when(j == 0)
    def _():
        xn_s[...] = _rms(x_ref[...], g_ref[...]).T.astype(BF16)
        acc_s[...] = jnp.zeros_like(acc_s)

    per = MXU_DEPTH // nk
    n_chunks = grp // per
    d_model = vt_ref.shape[0]
    xn = xn_s[...]
    scores, gates, weights, chunk_w = {}, {}, {}, {}

    def score_pieces(c):
        def piece(ii):
            scores[ii] = _dot(u_ref[ii * nk:(ii + 1) * nk, :], xn)
        return [functools.partial(piece, c * per + i) for i in range(per)]

    def gate_pieces(c):
        def piece(ii, h):
            row = pl.ds(j * grp + ii, 1)
            c1 = jnp.broadcast_to(c1_ref[h, row, :], (sub, tb)).astype(e2_ref.dtype)[None]
            n1 = jnp.broadcast_to(n1_ref[h, row, :], (sub, tb)).astype(e2_ref.dtype)[None]
            e2 = e2_ref[h]
            term = jnp.where(r2_ref[h] < n1, e2, jnp.zeros_like(e2)) * c1
            gates[ii] = term if h == 0 else gates[ii] + term
        return [functools.partial(piece, c * per + i, h) for i in range(per) for h in range(PEER_HEADS)]

    def act_pieces(c):
        def piece(ii):
            act = _gelu_tanh(scores.pop(ii).astype(e2_ref.dtype))
            weights[ii] = act * gates.pop(ii).reshape(nk, tb)
        return [functools.partial(piece, c * per + i) for i in range(per)]

    def value_pieces(c):
        ex = slice(c * MXU_DEPTH, (c + 1) * MXU_DEPTH)

        def piece(m):
            if m == 0:
                chunk_w[c] = jnp.concatenate([weights.pop(c * per + i) for i in range(per)], axis=0)
            rows = slice(m * MXU_DEPTH, (m + 1) * MXU_DEPTH)
            acc_s[rows, :] += _dot(vt_ref[rows, ex], chunk_w[c])
        return [functools.partial(piece, m) for m in range(d_model // MXU_DEPTH)]

    _merge_run(score_pieces(0), gate_pieces(0))
    for c in range(n_chunks):
        mxu = (score_pieces(c + 1) if c + 1 < n_chunks else []) + (value_pieces(c - 1) if c else [])
        vpu = act_pieces(c) + (gate_pieces(c + 1) if c + 1 < n_chunks else [])
        _merge_run(mxu, vpu)
    _merge_run(value_pieces(n_chunks - 1), [])

    @pl.when(j == pl.num_programs(1) - 1)
    def _():
        y = x_ref[...] + acc_s[...].T
        if final:
            y = _rms(y, gf_ref[...])
        o_ref[...] = y


def peer_dense_residual(x, g, route, u, vt, g_final, *, tb, grp, blocks=None):
    t, d = x.shape
    nk = PEER_NKEYS
    eb = grp * nk
    first, count = blocks if blocks is not None else (0, t // tb)
    c1, n1, e2, r2 = route
    sub = 32 // jnp.dtype(e2.dtype).itemsize
    e2 = e2.reshape(PEER_HEADS, nk // sub, sub, t)
    r2 = r2.reshape(PEER_HEADS, nk // sub, sub, t)
    rspec = pl.BlockSpec((PEER_HEADS, nk, tb), lambda i, j: (0, 0, first + i))
    pspec = pl.BlockSpec((PEER_HEADS, nk // sub, sub, tb), lambda i, j: (0, 0, 0, first + i))
    in_specs = [pl.BlockSpec((tb, d), lambda i, j: (first + i, 0)),
                pl.BlockSpec((1, d), lambda i, j: (0, 0)),
                rspec, rspec, pspec, pspec,
                pl.BlockSpec((eb, d), lambda i, j: (j, 0)),
                pl.BlockSpec((d, eb), lambda i, j: (0, j))]
    args = [x, g.reshape(1, d), c1, n1, e2, r2, u, vt]
    final = g_final is not None
    if final:
        in_specs.append(pl.BlockSpec((1, d), lambda i, j: (0, 0)))
        args.append(g_final.reshape(1, d))
    return pl.pallas_call(
        functools.partial(_peer_dense_body, final=final),
        out_shape=jax.ShapeDtypeStruct((count * tb, d), F32),
        grid=(count, (nk * nk) // eb),
        in_specs=in_specs,
        out_specs=pl.BlockSpec((tb, d), lambda i, j: (i, 0)),
        scratch_shapes=[pltpu.VMEM((d, tb), BF16),
                        pltpu.VMEM((d, tb), F32)],
        compiler_params=_params(2),
        name="peer_dense_final" if final else "peer_dense",
    )(*args)


def _pick(n, prefs):
    for p in prefs:
        if n % p == 0:
            return p
    return n


def _encoder(group_batches, x3, mem3, norm_mix, w_in, ret_norm, gdn_conv, gdn_a_log, gdn_dt_bias,
             gdn_norm, w_out, norm_xa, norm_mem, w_xq, w_xkv, w_xo, norm_ffn, peer_wq, peer_keys,
             peer_u, peer_v, norm_final):
    b, s_len, d = x3.shape
    n_mem = mem3.shape[1]
    t = b * s_len
    depth = w_in.shape[0]
    x = x3.reshape(t, d)
    mem = mem3.reshape(b * n_mem, d)
    n_main = 4 * RET_HEADS * HEAD_DIM + 4 * GDN_HEADS * HEAD_DIM
    tm = _pick(t, (1024, 512, 256, 128))
    tmm = _pick(b * n_mem, (1024, 512, 256, 128))
    ts = _pick(s_len, (512, 256, 128))
    tb_route = _pick(t, (256, 128))
    tb_dense = _pick(s_len, (512, 256, 128))
    nh = 2 * GDN_HEADS

    for l in range(depth):
        w_main = _deinterleave_rotary_columns(w_in[l, :, :n_main]).astype(BF16)
        w_gate_t = w_in[l, :, n_main:].T.astype(BF16)
        proj, graw = norm_matmul_side(x, norm_mix[l], w_main, w_gate_t, tm=tm, tn=1024)
        proj3 = proj.reshape(b, s_len, n_main)
        o_r = retention(proj3, ret_norm[l], b, s_len, heads=RET_HEADS_PER_STEP)
        gates = gdn_gates(graw, gdn_a_log[l], gdn_dt_bias[l], b, s_len)
        gates = gates.reshape(b, 3, 2, GDN_HEADS, s_len).transpose(0, 3, 2, 1, 4)
        gate_row = jnp.pad(gates.reshape(b, GDN_HEADS, 6, s_len), ((0, 0), (0, 0), (0, 2), (0, 0)))
        gate_col = jnp.swapaxes(gate_row, 2, 3)
        gate_row = gate_row.reshape(b, GDN_HEADS, 8, s_len // GDN_SUPER, GDN_SUPER).transpose(0, 1, 3, 2, 4)
        qkv = gdn_prep(proj3, gdn_conv[l], b, s_len)
        o_g = gdn_core(qkv, proj3, gate_col, gate_row, gdn_norm[l], b, s_len, heads=GDN_HEADS_PER_STEP)
        w_o = w_out[l].astype(BF16)
        n_r = RET_HEADS * HEAD_DIM

        kv = norm_matmul(mem, norm_mem[l], w_xkv[l].astype(BF16), tm=tmm, tn=1024)
        x, pq = mixer_xattn_peerq(x, o_r.reshape(t, -1), o_g.reshape(t, -1), w_o[:n_r], w_o[n_r:],
                                  norm_xa[l], w_xq[l].astype(BF16), kv.reshape(b, n_mem, 2 * d),
                                  w_xo[l].astype(BF16), norm_ffn[l], peer_wq[l].astype(BF16),
                                  b, s_len, ts=ts)

        route = peer_route(pq, peer_keys[l].astype(BF16), tb=tb_route)
        dense = functools.partial(peer_dense_residual, x, norm_ffn[l], route, peer_u[l].astype(BF16),
                                  peer_v[l].astype(BF16).T, tb=tb_dense, grp=16)
        if l < depth - 1:
            x = dense(None)
    outs, first = [], 0
    for nb in group_batches:
        n_blocks = nb * s_len // tb_dense
        outs.append(dense(norm_final, blocks=(first, n_blocks)).reshape(nb, s_len, d))
        first += n_blocks
    return tuple(outs)


def kernel(x_prompt, x_sample, mem_prompt, mem_sample, norm_mix, w_in, ret_norm, gdn_conv, gdn_a_log,
           gdn_dt_bias, gdn_norm, w_out, norm_xa, norm_mem, w_xq, w_xkv, w_xo, norm_ffn, peer_wq,
           peer_keys, peer_u, peer_v, norm_final):
    weights = (norm_mix, w_in, ret_norm, gdn_conv, gdn_a_log, gdn_dt_bias, gdn_norm, w_out,
               norm_xa, norm_mem, w_xq, w_xkv, w_xo, norm_ffn, peer_wq, peer_keys, peer_u, peer_v,
               norm_final)
    if x_prompt.shape[1:] == x_sample.shape[1:] and mem_prompt.shape[1:] == mem_sample.shape[1:]:
        return _encoder((x_prompt.shape[0], x_sample.shape[0]),
                        jnp.concatenate([x_prompt, x_sample], axis=0),
                        jnp.concatenate([mem_prompt, mem_sample], axis=0), *weights)
    return (_encoder((x_prompt.shape[0],), x_prompt, mem_prompt, *weights)[0],
            _encoder((x_sample.shape[0],), x_sample, mem_sample, *weights)[0])
```

```python
import functools

import jax
import jax.numpy as jnp
from jax import lax
from jax.experimental import pallas as pl
from jax.experimental.pallas import tpu as pltpu

F32 = jnp.float32
BF16 = jnp.bfloat16

EPS = 1e-6
LANES = 128
MXU_DEPTH = 256
HEAD_DIM = 128
RET_HEADS = 4
GDN_HEADS = 4
RET_CHUNK = 128
GDN_CHUNK = 64
GDN_SUPER = 256
GDN_INV_BASE_LOG2 = 1
GDN_HEADS_PER_STEP = 4
RET_HEADS_PER_STEP = 4
ROPE_BASE = 10000.0
XA_HEADS = 4
PEER_HEADS = 8
PEER_NKEYS = 128
PEER_TOPK = 16
TAKEN = 2.0 ** 100
TAKEN_STEP = 2.0 ** 96
FAST_PAD = -(2.0 ** 90)
FAST_SCORE_BOUND = 2.0 ** 80
VMEM_LIMIT = 56 * 1024 * 1024

_NT = (((1,), (1,)), ((), ()))
_TN = (((0,), (0,)), ((), ()))


def _params(n_axes):
    return pltpu.CompilerParams(
        dimension_semantics=("arbitrary",) * n_axes, vmem_limit_bytes=VMEM_LIMIT)


def _dot(a, b):
    return jnp.dot(a, b, preferred_element_type=F32)


def _dot_nt(a, b):
    return lax.dot_general(a, b, _NT, preferred_element_type=F32)


def _dot_tn(a, b):
    return lax.dot_general(a, b, _TN, preferred_element_type=F32)


def _split(a):
    hi = a.astype(BF16)
    return hi, (a - hi.astype(F32)).astype(BF16)


def _dot3(a, b):
    (ah, al), (bh, bl) = a, b
    return _dot(jnp.concatenate([ah, ah, al], axis=1), jnp.concatenate([bh, bl, bh], axis=0))


def _rms(x, g):
    return x * lax.rsqrt(jnp.mean(x * x, axis=-1, keepdims=True) + EPS) * g


def _norm_matmul_body(x_ref, g_ref, w_ref, o_ref, xn_ref):
    @pl.when(pl.program_id(1) == 0)
    def _():
        xn_ref[...] = _rms(x_ref[...], g_ref[...]).astype(BF16)

    o_ref[...] = _dot(xn_ref[...], w_ref[...]).astype(o_ref.dtype)


def norm_matmul(x, g, w, *, tm, tn, out_dtype=F32):
    t, d = x.shape
    n = w.shape[1]
    return pl.pallas_call(
        _norm_matmul_body,
        out_shape=jax.ShapeDtypeStruct((t, n), out_dtype),
        grid=(t // tm, n // tn),
        in_specs=[pl.BlockSpec((tm, d), lambda i, j: (i, 0)),
                  pl.BlockSpec((1, d), lambda i, j: (0, 0)),
                  pl.BlockSpec((d, tn), lambda i, j: (0, j))],
        out_specs=pl.BlockSpec((tm, tn), lambda i, j: (i, j)),
        scratch_shapes=[pltpu.VMEM((tm, d), BF16)],
        compiler_params=_params(2),
        name="norm_matmul",
    )(x, g.reshape(1, d), w)


def _mixer_in_proj_body(x_ref, g_ref, w_ref, ws_ref, cos_ref, sin_ref, o_ref, os_ref, xn_ref):
    j = pl.program_id(1)
    hd = HEAD_DIM

    @pl.when(j == 0)
    def _():
        xn = _rms(x_ref[...], g_ref[...]).astype(BF16)
        xn_ref[...] = xn
        os_ref[...] = _dot_nt(ws_ref[...], xn)

    o = _dot(xn_ref[...], w_ref[...])

    @pl.when(j == 0)
    def _():
        cos = cos_ref[...]
        sin = sin_ref[...]
        for head in range(2 * RET_HEADS):
            cols = slice(head * hd, (head + 1) * hd)
            x = o[:, cols]
            y = x * cos + pltpu.roll(x, hd // 2, 1) * sin
            o_ref[:, cols] = y * (hd ** -0.5) if head >= RET_HEADS else y

    @pl.when(j > 0)
    def _():
        o_ref[...] = o


def mixer_in_proj(x, g, w, ws_t, s_len, *, tm):
    t, d = x.shape
    n = w.shape[1]
    rows = ws_t.shape[0]
    tn = 2 * RET_HEADS * HEAD_DIM
    cos, sin = _rotary_tables(s_len)
    pos_blocks = s_len // tm
    return pl.pallas_call(
        _mixer_in_proj_body,
        out_shape=(jax.ShapeDtypeStruct((t, n), F32),
                   jax.ShapeDtypeStruct((rows, t), F32)),
        grid=(t // tm, n // tn),
        in_specs=[pl.BlockSpec((tm, d), lambda i, j: (i, 0)),
                  pl.BlockSpec((1, d), lambda i, j: (0, 0)),
                  pl.BlockSpec((d, tn), lambda i, j: (0, j)),
                  pl.BlockSpec((rows, d), lambda i, j: (0, 0)),
                  pl.BlockSpec((tm, HEAD_DIM), lambda i, j: (i % pos_blocks, 0)),
                  pl.BlockSpec((tm, HEAD_DIM), lambda i, j: (i % pos_blocks, 0))],
        out_specs=(pl.BlockSpec((tm, tn), lambda i, j: (i, j)),
                   pl.BlockSpec((rows, tm), lambda i, j: (0, i))),
        scratch_shapes=[pltpu.VMEM((tm, d), BF16)],
        compiler_params=_params(2),
        name="mixer_in_proj",
    )(x, g.reshape(1, d), w, ws_t, cos, sin)


def _retention_body(qr_s, kr_s, v_ref, g_ref, dec_ref, rn_ref, o_ref, of_s, ob_s):
    s_len = qr_s.shape[0]
    hd = HEAD_DIM
    heads = qr_s.shape[1] // hd
    c = RET_CHUNK
    n = s_len // c

    def forward(hh, i, st):
        r = pl.ds(pl.multiple_of(i * c, c), c)
        lanes = slice(hh * hd, (hh + 1) * hd)
        qc, kc, vb = qr_s[r, lanes], kr_s[r, lanes], v_ref[r, lanes].astype(BF16)
        sc = _dot_nt(qc.astype(BF16), kc.astype(BF16)) * dec_ref[hh, 0]
        inter = _dot((qc * dec_ref[hh, 1]).astype(BF16), st.astype(BF16))
        st = st * dec_ref[hh, 5] + _dot_tn((kc * dec_ref[hh, 2]).astype(BF16), vb)
        yield
        return inter + _dot(sc.astype(BF16), vb), st

    def backward(hh, i, st):
        r = pl.ds(pl.multiple_of(i * c, c), c)
        lanes = slice(hh * hd, (hh + 1) * hd)
        qc, kc, vb = qr_s[r, lanes], kr_s[r, lanes], v_ref[r, lanes].astype(BF16)
        inter = _dot((qc * dec_ref[hh, 3]).astype(BF16), st.astype(BF16))
        st = st * dec_ref[hh, 6] + _dot_tn((kc * dec_ref[hh, 4]).astype(BF16), vb)
        yield
        return inter, st

    def step(t, carry):
        fwd_rows = pl.ds(pl.multiple_of(t * c, c), c)
        bwd_rows = pl.ds(pl.multiple_of((n - 1 - t) * c, c), c)
        gens = []
        for hh in range(heads):
            gens.append(forward(hh, t, carry[2 * hh]))
            gens.append(backward(hh, n - 1 - t, carry[2 * hh + 1]))
        res = _lockstep(gens)
        for hh in range(heads):
            lanes = slice(hh * hd, (hh + 1) * hd)
            of_s[fwd_rows, lanes] = res[2 * hh][0]
            ob_s[bwd_rows, lanes] = res[2 * hh + 1][0]
        return tuple(st for _, st in res)

    zero = jnp.zeros((hd, hd), F32)
    lax.fori_loop(0, n, step, (zero,) * (2 * heads))

    for hh in range(heads):
        lanes = slice(hh * hd, (hh + 1) * hd)
        o = of_s[:, lanes] + ob_s[:, lanes]
        mu = jnp.mean(o, axis=-1, keepdims=True)
        var = jnp.mean(jnp.square(o - mu), axis=-1, keepdims=True)
        y = (o - mu) * lax.rsqrt(var + EPS)
        y = y * rn_ref[:, lanes] * jax.nn.silu(g_ref[:, lanes])
        o_ref[:, lanes] = y.astype(o_ref.dtype)


def _deinterleave_rotary_columns(w):
    n_qk = 2 * RET_HEADS * HEAD_DIM
    qk = w[:, :n_qk].reshape(w.shape[0], 2 * RET_HEADS, HEAD_DIM // 2, 2)
    qk = jnp.swapaxes(qk, 2, 3).reshape(w.shape[0], n_qk)
    return jnp.concatenate([qk, w[:, n_qk:]], axis=1)


def _rotary_tables(s_len):
    d = HEAD_DIM
    inv = ROPE_BASE ** (-jnp.arange(0, d, 2, dtype=F32) / d)
    ang = jnp.arange(s_len, dtype=F32)[:, None] * inv[None, :]
    cos = jnp.concatenate([jnp.cos(ang), jnp.cos(ang)], axis=1)
    sin = jnp.concatenate([-jnp.sin(ang), jnp.sin(ang)], axis=1)
    return cos, sin


def _retention_tables():
    d = HEAD_DIM
    h = jnp.arange(RET_HEADS, dtype=F32)
    lg_f = jnp.log1p(-jnp.exp2(-5.0 - h))[:, None, None]
    lg_b = jnp.log1p(-jnp.exp2(-5.5 - h))[:, None, None]
    c = RET_CHUNK
    pos = jnp.arange(c, dtype=F32)
    diff = (pos[:, None] - pos[None, :])[None]
    dmat = jnp.where(diff >= 0, jnp.exp(lg_f * jnp.where(diff >= 0, diff, 0.0)),
                     jnp.exp(lg_b * jnp.where(diff < 0, -diff, 0.0)))
    col = lambda v: jnp.broadcast_to(v, (RET_HEADS, c, d))
    p = pos[None, :, None]
    dec = jnp.stack([
        dmat,
        col(jnp.exp(lg_f * (p + 1.0))),
        col(jnp.exp(lg_f * (c - 1.0 - p))),
        col(jnp.exp(lg_b * (c - p))),
        col(jnp.exp(lg_b * p)),
        col(jnp.exp(lg_f * c)),
        col(jnp.exp(lg_b * c)),
    ], axis=1)
    return dec


def retention(proj, ret_norm, b, s_len, *, heads):
    dec = _retention_tables()
    hd = HEAD_DIM
    w = heads * hd
    ng = RET_HEADS // heads
    head_blk = lambda off: pl.BlockSpec((None, s_len, w), lambda i, h: (i, 0, off + h))
    return pl.pallas_call(
        _retention_body,
        out_shape=jax.ShapeDtypeStruct((b, s_len, RET_HEADS * hd), BF16),
        grid=(b, ng),
        in_specs=[head_blk(0), head_blk(ng), head_blk(2 * ng), head_blk(3 * ng),
                  pl.BlockSpec((heads, 7, RET_CHUNK, hd), lambda i, h: (h, 0, 0, 0)),
                  pl.BlockSpec((1, w), lambda i, h: (0, h))],
        out_specs=pl.BlockSpec((None, s_len, w), lambda i, h: (i, 0, h)),
        scratch_shapes=[pltpu.VMEM((s_len, w), F32)] * 2,
        compiler_params=_params(2),
        name="retention",
    )(proj, proj, proj, proj, dec, ret_norm.reshape(1, -1))


def _gdn_gates_body(raw_ref, al_ref, dt_ref, o_ref):
    s_len = raw_ref.shape[-1]
    nh = 2 * GDN_HEADS
    gb = raw_ref[0:nh, :]
    ga = raw_ref[nh:2 * nh, :]
    beta = jax.nn.sigmoid(gb)
    g = -jnp.exp(al_ref[...]) * jax.nn.softplus(ga + dt_ref[...])
    pos = lax.broadcasted_iota(jnp.int32, (nh, s_len), 1) & (GDN_CHUNK - 1)
    fwd = g
    rev = g
    k = 1
    while k < GDN_CHUNK:
        fwd = fwd + jnp.where(pos >= k, pltpu.roll(fwd, k, 1), 0.0)
        rev = rev + jnp.where(pos < GDN_CHUNK - k, pltpu.roll(rev, s_len - k, 1), 0.0)
        k *= 2
    row = lax.broadcasted_iota(jnp.int32, (nh, s_len), 0)
    o_ref[0:nh, :] = jnp.where(row < GDN_HEADS, fwd, rev)
    o_ref[nh:2 * nh, :] = beta
    o_ref[2 * nh:3 * nh, :] = fwd + rev - g


def gdn_gates(raw, a_log, dt_bias, b, s_len):
    nh = 2 * GDN_HEADS
    return pl.pallas_call(
        _gdn_gates_body,
        out_shape=jax.ShapeDtypeStruct((b, 3 * nh, s_len), F32),
        grid=(b,),
        in_specs=[pl.BlockSpec((2 * nh, s_len), lambda i: (0, i)),
                  pl.BlockSpec((nh, 1), lambda i: (0, 0)),
                  pl.BlockSpec((nh, 1), lambda i: (0, 0))],
        out_specs=pl.BlockSpec((None, 3 * nh, s_len), lambda i: (i, 0, 0)),
        compiler_params=_params(1),
        name="gdn_gates",
    )(raw, a_log.reshape(nh, 1), dt_bias.reshape(nh, 1))


def _gdn_prep_body(x_ref, w_ref, o_ref):
    s_len = x_ref.shape[0]
    j = pl.program_id(1)
    x = x_ref[...]
    t = lax.broadcasted_iota(jnp.int32, x.shape, 0)
    prev = jnp.where(t == 0, 0.0, pltpu.roll(x, 1, 0))
    nxt = jnp.where(t == s_len - 1, 0.0, pltpu.roll(x, s_len - 1, 0))
    conv = prev * w_ref[0:1, :] + x * w_ref[1:2, :] + nxt * w_ref[2:3, :]
    act = jax.nn.silu(conv)
    inv = lax.rsqrt(jnp.sum(act * act, axis=-1, keepdims=True) + EPS)
    is_q = j < GDN_HEADS
    is_v = j >= 2 * GDN_HEADS
    scale = jnp.where(is_v, 1.0, jnp.where(is_q, inv * (HEAD_DIM ** -0.5), inv))
    o_ref[...] = act * scale


def gdn_prep(proj, conv_w, b, s_len):
    hd = HEAD_DIM
    nblk = 3 * GDN_HEADS
    first = 4 * RET_HEADS
    return pl.pallas_call(
        _gdn_prep_body,
        out_shape=jax.ShapeDtypeStruct((b, s_len, nblk * hd), F32),
        grid=(b, nblk),
        in_specs=[pl.BlockSpec((None, s_len, hd), lambda i, j: (i, 0, first + j)),
                  pl.BlockSpec((3, hd), lambda i, j: (0, j))],
        out_specs=pl.BlockSpec((None, s_len, hd), lambda i, j: (i, 0, j)),
        compiler_params=_params(2),
        name="gdn_prep",
    )(proj, conv_w)


def _lockstep(gens):
    results = [None] * len(gens)
    live = list(range(len(gens)))
    while live:
        for i in list(live):
            try:
                next(gens[i])
            except StopIteration as done:
                results[i] = done.value
                live.remove(i)
    return results


def _merge_run(a, b):
    ia = ib = 0
    while ia < len(a) or ib < len(b):
        if ib >= len(b) or (ia < len(a) and (ia + 1) * len(b) <= (ib + 1) * len(a)):
            a[ia]()
            ia += 1
        else:
            b[ib]()
            ib += 1


def _tri_inverse(m, ri, ci):
    def same_block(log2_size):
        return (ri >> log2_size) == (ci >> log2_size)

    base = GDN_INV_BASE_LOG2
    eye = (ri == ci).astype(F32)
    diag = jnp.where(same_block(base), -m, 0.0)
    nm = _split(diag)
    p = eye + diag
    for _ in range(base - 1):
        nm = _split(_dot3(nm, nm))
        yield
        p = p + _dot3(_split(p), nm)
        yield
    for size in range(base, GDN_CHUNK.bit_length() - 1):
        off = jnp.where(same_block(size + 1) & jnp.logical_not(same_block(size)), m, 0.0)
        p16 = p.astype(BF16)
        pc = _dot(p16, off.astype(BF16)).astype(BF16)
        yield
        p = p - _dot(pc, p16)
        yield
    return p


def _gdn_super_chunk(q, k, v, gcol, grow, bcol, glcol, st, rev):
    r = GDN_SUPER
    c = GDN_CHUNK
    ri = lax.broadcasted_iota(jnp.int32, (r, r), 0)
    ci = lax.broadcasted_iota(jnp.int32, (r, r), 1)
    same = (ri >> 6) == (ci >> 6)
    if rev:
        incl = same & (ri <= ci)
        strict = same & (ri < ci)
    else:
        incl = same & (ri >= ci)
        strict = same & (ri > ci)
    decay = jnp.exp(jnp.where(incl, gcol - grow, -jnp.inf))
    kb = k * bcol
    vb = v * bcol
    k16 = k.astype(BF16)
    m = jnp.where(strict, _dot_nt(kb.astype(BF16), k16) * decay, 0.0)
    qk = jnp.where(incl, _dot_nt(q.astype(BF16), k16) * decay, 0.0).astype(BF16)
    yield
    t = yield from _tri_inverse(m, ri, ci)
    eg = jnp.exp(gcol)
    rhs = jnp.concatenate([vb, kb * eg], axis=1).astype(BF16)
    uw = _dot(t.astype(BF16), rhs)
    u = uw[:, :HEAD_DIM]
    w = uw[:, HEAD_DIM:].astype(BF16)
    qd = (q * eg).astype(BF16)
    kt = (k * jnp.exp(glcol - gcol)).astype(BF16)
    yield
    outs = [None] * (r // c)
    order = range(r // c - 1, -1, -1) if rev else range(r // c)
    for i in order:
        sl = slice(i * c, (i + 1) * c)
        s16 = st.astype(BF16)
        vn = u[sl] - _dot(w[sl], s16)
        vn16 = vn.astype(BF16)
        yield
        pair = slice((i // 2) * 2 * c, (i // 2 + 1) * 2 * c)
        zero = jnp.zeros_like(vn16)
        vpad = jnp.concatenate([zero, vn16] if i % 2 else [vn16, zero], axis=0)
        outs[i] = _dot(qd[sl], s16) + _dot(qk[sl, pair], vpad)
        st = st * jnp.exp(glcol[i * c:i * c + 1, :]) + _dot_tn(kt[sl], vn16)
        yield
    return jnp.concatenate(outs, axis=0), st


def _gdn_core_body(q_ref, k_ref, v_ref, z_ref, col_ref, row_ref, gn_ref, o_ref, of_s, ob_s):
    s_len = q_ref.shape[0]
    hd = HEAD_DIM
    heads = q_ref.shape[1] // hd
    r = GDN_SUPER
    n = s_len // r

    def one(hh, i, st, rev):
        base = 3 if rev else 0
        rows = pl.ds(pl.multiple_of(i * r, r), r)
        lanes = slice(hh * hd, (hh + 1) * hd)
        cols = col_ref[hh, rows, :]
        gcol = cols[:, base:base + 1]
        bcol = cols[:, base + 1:base + 2]
        glcol = cols[:, base + 2:base + 3]
        grow = row_ref[hh, i][base:base + 1, :]
        return _gdn_super_chunk(q_ref[rows, lanes], k_ref[rows, lanes], v_ref[rows, lanes],
                                gcol, grow, bcol, glcol, st, rev)

    def step(t, carry):
        fwd_rows = pl.ds(pl.multiple_of(t * r, r), r)
        bwd_rows = pl.ds(pl.multiple_of((n - 1 - t) * r, r), r)
        gens = []
        for hh in range(heads):
            gens.append(one(hh, t, carry[2 * hh], False))
            gens.append(one(hh, n - 1 - t, carry[2 * hh + 1], True))
        res = _lockstep(gens)
        for hh in range(heads):
            lanes = slice(hh * hd, (hh + 1) * hd)
            of_s[fwd_rows, lanes] = res[2 * hh][0]
            ob_s[bwd_rows, lanes] = res[2 * hh + 1][0]
        return tuple(st for _, st in res)

    zero = jnp.zeros((hd, hd), F32)
    lax.fori_loop(0, n, step, (zero,) * (2 * heads))
    for hh in range(heads):
        lanes = slice(hh * hd, (hh + 1) * hd)
        o = of_s[:, lanes] + ob_s[:, lanes]
        y = o * lax.rsqrt(jnp.mean(o * o, axis=-1, keepdims=True) + EPS) * gn_ref[...]
        o_ref[:, lanes] = (y * jax.nn.silu(z_ref[:, lanes])).astype(o_ref.dtype)


def gdn_core(qkv, proj, gate_col, gate_row, gdn_norm, b, s_len, *, heads):
    hd = HEAD_DIM
    w = heads * hd
    ng = GDN_HEADS // heads
    z_first = (4 * RET_HEADS + 3 * GDN_HEADS) // heads
    qkv_blk = lambda off: pl.BlockSpec((None, s_len, w), lambda i, h: (i, 0, off + h))
    return pl.pallas_call(
        _gdn_core_body,
        out_shape=jax.ShapeDtypeStruct((b, s_len, GDN_HEADS * hd), BF16),
        grid=(b, ng),
        in_specs=[qkv_blk(0), qkv_blk(ng), qkv_blk(2 * ng),
                  pl.BlockSpec((None, s_len, w), lambda i, h: (i, 0, z_first + h),
                               pipeline_mode=pl.Buffered(1)),
                  pl.BlockSpec((None, heads, s_len, 8), lambda i, h: (i, h, 0, 0),
                               pipeline_mode=pl.Buffered(1)),
                  pl.BlockSpec((None, heads, s_len // GDN_SUPER, 8, GDN_SUPER),
                               lambda i, h: (i, h, 0, 0, 0)),
                  pl.BlockSpec((1, hd), lambda i, h: (0, 0))],
        out_specs=pl.BlockSpec((None, s_len, w), lambda i, h: (i, 0, h)),
        scratch_shapes=[pltpu.VMEM((s_len, w), F32)] * 2,
        compiler_params=_params(2),
        name="gdn_core",
    )(qkv, qkv, qkv, proj, gate_col, gate_row, gdn_norm.reshape(1, hd))


def _mixer_xattn_body(x_ref, a1_ref, a2_ref, w1_ref, w2_ref, gxa_ref, wq_ref, kv_ref, wo_ref,
                      gffn_ref, wpq_ref, o_ref, pq_ref):
    x = x_ref[...] + _dot(a1_ref[...], w1_ref[...]) + _dot(a2_ref[...], w2_ref[...])
    d = x.shape[1]
    dh = d // XA_HEADS
    q = _dot(_rms(x, gxa_ref[...]).astype(BF16), wq_ref[...])
    outs = []
    for h in range(XA_HEADS):
        kh = kv_ref[:, h * dh:(h + 1) * dh].astype(BF16)
        vh = kv_ref[:, d + h * dh:d + (h + 1) * dh].astype(BF16)
        sc = _dot_nt(q[:, h * dh:(h + 1) * dh].astype(BF16), kh) * (dh ** -0.5)
        p = jax.nn.softmax(sc, axis=-1)
        outs.append(_dot(p.astype(BF16), vh).astype(BF16))
    x = x + _dot(jnp.concatenate(outs, axis=1), wo_ref[...])
    o_ref[...] = x
    pq_ref[...] = _dot(_rms(x, gffn_ref[...]).astype(BF16), wpq_ref[...])


def mixer_xattn_peerq(x, a1, a2, w1, w2, g_xa, wq, kv, wo, g_ffn, wpq, b, s_len, *, ts):
    t, d = x.shape
    k1, k2 = a1.shape[1], a2.shape[1]
    m = kv.shape[1]
    npq = wpq.shape[1]
    nblk = s_len // ts
    tok = lambda n: pl.BlockSpec((ts, n), lambda i, j: (i * nblk + j, 0))
    whole = lambda r, c: pl.BlockSpec((r, c), lambda i, j: (0, 0))
    return pl.pallas_call(
        _mixer_xattn_body,
        out_shape=(jax.ShapeDtypeStruct((t, d), F32), jax.ShapeDtypeStruct((t, npq), F32)),
        grid=(b, nblk),
        in_specs=[tok(d), tok(k1), tok(k2), whole(k1, d), whole(k2, d),
                  whole(1, d), whole(d, d),
                  pl.BlockSpec((None, m, 2 * d), lambda i, j: (i, 0, 0)),
                  whole(d, d), whole(1, d), whole(d, npq)],
        out_specs=(tok(d), tok(npq)),
        compiler_params=_params(2),
        name="mixer_xattn_peerq",
    )(x, a1, a2, w1, w2, g_xa.reshape(1, d), wq, kv, wo, g_ffn.reshape(1, d), wpq)


def _extract_topk(s, k, break_ties):
    vals = []
    if break_ties:
        n_rows = s.shape[0]
        iota = lax.broadcasted_iota(jnp.int32, s.shape, 0).astype(F32)
        rank = jnp.full(s.shape, float(k), F32)
        for r in range(k):
            m = jnp.max(s, axis=0, keepdims=True)
            idx = jnp.min(jnp.where(s == m, iota, float(n_rows)), axis=0, keepdims=True)
            hit = iota == idx
            rank = jnp.where(hit, float(r), rank)
            s = jnp.where(hit, -jnp.inf, s)
            vals.append(m)
        taken = rank < float(k)
    else:
        for r in range(k):
            m = jnp.max(s, axis=0, keepdims=True)
            s = jnp.where(s == m, -(TAKEN + r * TAKEN_STEP), s)
            vals.append(m)
        taken = s <= -TAKEN
        rank = jnp.where(taken, s * (-1.0 / TAKEN_STEP) - TAKEN / TAKEN_STEP, float(k))
    count = jnp.sum(jnp.where(taken, 1.0, 0.0), axis=0, keepdims=True)
    return rank, vals, count


def _route_tokens(q, keys_ref, break_ties):
    nk = PEER_NKEYS
    kk = PEER_TOPK
    s1 = _dot_nt(keys_ref[0], q[:, :nk])
    s2 = _dot_nt(keys_ref[1], q[:, nk:])
    r1, v1, n_sel1 = _extract_topk(s1, kk, break_ties)
    r2, v2, n_sel2 = _extract_topk(s2, kk, break_ties)
    v1m = jnp.concatenate(v1, axis=0)
    v2m = jnp.concatenate(v2, axis=0)
    pad = -jnp.inf if break_ties else FAST_PAD
    row8 = lax.broadcasted_iota(jnp.int32, (8, LANES), 0)
    groups = [v1[0] + v2m[0:8], v1[0] + v2m[8:16]]
    for a in range(1, 8):
        groups.append(jnp.where(row8 < kk // (a + 1), v1[a] + v2m[0:8], pad))
    groups.append(v1m[8:16] + v2[0])
    cand = jnp.concatenate(groups, axis=0)
    rc, _, n_selc = _extract_topk(cand, kk, break_ties)
    sel = rc < float(kk)
    top = v1[0] + v2[0]
    z = jnp.sum(jnp.where(sel, jnp.exp(cand - top), 0.0), axis=0, keepdims=True)
    sel_f = jnp.where(sel, 1.0, 0.0)
    n1 = jnp.zeros((nk, LANES), F32)
    for a in range(kk):
        if a == 0:
            cnt = jnp.sum(sel_f[0:16], axis=0, keepdims=True)
        elif a < 8:
            cnt = jnp.sum(sel_f[8 + 8 * a:16 + 8 * a], axis=0, keepdims=True)
        else:
            cnt = sel_f[64 + a:65 + a]
        n1 = jnp.where(r1 == float(a), cnt, n1)
    c1 = jnp.where(r1 < float(kk), jnp.exp(s1 - v1[0]) / z, 0.0)
    e2 = jnp.exp(s2 - v2[0])
    exact = ((n_sel1 == float(kk)) & (n_sel2 == float(kk)) & (n_selc == float(kk))
             & (v1[kk - 1] > -FAST_SCORE_BOUND) & (v2[kk - 1] > -FAST_SCORE_BOUND))
    return c1, n1, e2, r2, jnp.where(exact, 1.0, 0.0)


def _peer_route_body(q_ref, keys_ref, c1_ref, n1_ref, e2_ref, r2_ref):
    tb = q_ref.shape[0]

    def run(break_ties):
        flags = []
        for blk in range(tb // LANES):
            tok = slice(blk * LANES, (blk + 1) * LANES)
            c1, n1, e2, r2, ok = _route_tokens(q_ref[tok, :].astype(BF16), keys_ref, break_ties)
            c1_ref[:, tok] = c1
            n1_ref[:, tok] = n1
            e2_ref[:, tok] = e2.astype(e2_ref.dtype)
            r2_ref[:, tok] = r2.astype(r2_ref.dtype)
            flags.append(ok)
        return jnp.min(jnp.concatenate(flags, axis=1))

    all_exact = run(False)

    @pl.when(all_exact < 0.5)
    def _():
        run(True)


def peer_route(q, keys, *, tb):
    t = q.shape[0]
    nk = PEER_NKEYS
    out = lambda dt: jax.ShapeDtypeStruct((PEER_HEADS, nk, t), dt)
    ospec = pl.BlockSpec((None, nk, tb), lambda i, h: (h, 0, i))
    return pl.pallas_call(
        _peer_route_body,
        out_shape=(out(F32), out(F32), out(BF16), out(BF16)),
        grid=(t // tb, PEER_HEADS),
        in_specs=[pl.BlockSpec((tb, 2 * nk), lambda i, h: (i, h)),
                  pl.BlockSpec((None, 2, nk, nk), lambda i, h: (h, 0, 0, 0))],
        out_specs=(ospec,) * 4,
        compiler_params=_params(2),
        name="peer_route",
    )(q, keys)


def _gelu_tanh(x):
    c0 = 0.7978845608028654
    c1 = 0.7978845608028654 * 0.044715
    return (0.5 * x) * (1.0 + jnp.tanh(x * (c0 + c1 * (x * x))))


def _peer_dense_body(x_ref, g_ref, c1_ref, n1_ref, e2_ref, r2_ref, u_ref, vt_ref, *rest, final):
    if final:
        gf_ref, o_ref, xn_s, acc_s = rest
    else:
        o_ref, xn_s, acc_s = rest
    j = pl.program_id(1)
    nk = PEER_NKEYS
    tb = x_ref.shape[0]
    grp = u_ref.shape[0] // nk
    sub = e2_ref.shape[2]

    @pl.when(j == 0)
    def _():
        xn_s[...] = _rms(x_ref[...], g_ref[...]).T.astype(BF16)
        acc_s[...] = jnp.zeros_like(acc_s)

    per = MXU_DEPTH // nk
    n_chunks = grp // per
    d_model = vt_ref.shape[0]
    xn = xn_s[...]
    scores, gates, weights, chunk_w = {}, {}, {}, {}

    def score_pieces(c):
        def piece(ii):
            scores[ii] = _dot(u_ref[ii * nk:(ii + 1) * nk, :], xn)
        return [functools.partial(piece, c * per + i) for i in range(per)]

    def gate_pieces(c):
        def piece(ii, h):
            row = pl.ds(j * grp + ii, 1)
            c1 = jnp.broadcast_to(c1_ref[h, row, :], (sub, tb)).astype(e2_ref.dtype)[None]
            n1 = jnp.broadcast_to(n1_ref[h, row, :], (sub, tb)).astype(e2_ref.dtype)[None]
            e2 = e2_ref[h]
            term = jnp.where(r2_ref[h] < n1, e2, jnp.zeros_like(e2)) * c1
            gates[ii] = term if h == 0 else gates[ii] + term
        return [functools.partial(piece, c * per + i, h) for i in range(per) for h in range(PEER_HEADS)]

    def act_pieces(c):
        def piece(ii):
            act = _gelu_tanh(scores.pop(ii).astype(e2_ref.dtype))
            weights[ii] = act * gates.pop(ii).reshape(nk, tb)
        return [functools.partial(piece, c * per + i) for i in range(per)]

    def value_pieces(c):
        ex = slice(c * MXU_DEPTH, (c + 1) * MXU_DEPTH)

        def piece(m):
            if m == 0:
                chunk_w[c] = jnp.concatenate([weights.pop(c * per + i) for i in range(per)], axis=0)
            rows = slice(m * MXU_DEPTH, (m + 1) * MXU_DEPTH)
            acc_s[rows, :] += _dot(vt_ref[rows, ex], chunk_w[c])
        return [functools.partial(piece, m) for m in range(d_model // MXU_DEPTH)]

    _merge_run(score_pieces(0), gate_pieces(0))
    for c in range(n_chunks):
        mxu = (score_pieces(c + 1) if c + 1 < n_chunks else []) + (value_pieces(c - 1) if c else [])
        vpu = act_pieces(c) + (gate_pieces(c + 1) if c + 1 < n_chunks else [])
        _merge_run(mxu, vpu)
    _merge_run(value_pieces(n_chunks - 1), [])

    @pl.when(j == pl.num_programs(1) - 1)
    def _():
        y = x_ref[...] + acc_s[...].T
        if final:
            y = _rms(y, gf_ref[...])
        o_ref[...] = y


def peer_dense_residual(x, g, route, u, vt, g_final, *, tb, grp, blocks=None):
    t, d = x.shape
    nk = PEER_NKEYS
    eb = grp * nk
    first, count = blocks if blocks is not None else (0, t // tb)
    c1, n1, e2, r2 = route
    sub = 32 // jnp.dtype(e2.dtype).itemsize
    e2 = e2.reshape(PEER_HEADS, nk // sub, sub, t)
    r2 = r2.reshape(PEER_HEADS, nk // sub, sub, t)
    rspec = pl.BlockSpec((PEER_HEADS, nk, tb), lambda i, j: (0, 0, first + i))
    pspec = pl.BlockSpec((PEER_HEADS, nk // sub, sub, tb), lambda i, j: (0, 0, 0, first + i))
    in_specs = [pl.BlockSpec((tb, d), lambda i, j: (first + i, 0)),
                pl.BlockSpec((1, d), lambda i, j: (0, 0)),
                rspec, rspec, pspec, pspec,
                pl.BlockSpec((eb, d), lambda i, j: (j, 0)),
                pl.BlockSpec((d, eb), lambda i, j: (0, j))]
    args = [x, g.reshape(1, d), c1, n1, e2, r2, u, vt]
    final = g_final is not None
    if final:
        in_specs.append(pl.BlockSpec((1, d), lambda i, j: (0, 0)))
        args.append(g_final.reshape(1, d))
    return pl.pallas_call(
        functools.partial(_peer_dense_body, final=final),
        out_shape=jax.ShapeDtypeStruct((count * tb, d), F32),
        grid=(count, (nk * nk) // eb),
        in_specs=in_specs,
        out_specs=pl.BlockSpec((tb, d), lambda i, j: (i, 0)),
        scratch_shapes=[pltpu.VMEM((d, tb), BF16),
                        pltpu.VMEM((d, tb), F32)],
        compiler_params=_params(2),
        name="peer_dense_final" if final else "peer_dense",
    )(*args)


def _transpose_cast_body(x_ref, o_ref):
    o_ref[...] = x_ref[...].T.astype(o_ref.dtype)


def transpose_cast(tables, layer):
    _, rows, cols = tables.shape
    tr = _pick(rows, (1024, 512, 256, 128))
    return pl.pallas_call(
        _transpose_cast_body,
        out_shape=jax.ShapeDtypeStruct((cols, rows), BF16),
        grid=(rows // tr,),
        in_specs=[pl.BlockSpec((None, tr, cols), lambda i: (layer, i, 0))],
        out_specs=pl.BlockSpec((cols, tr), lambda i: (0, i)),
        compiler_params=_params(1),
        name="transpose_cast",
    )(tables)


def _pick(n, prefs):
    for p in prefs:
        if n % p == 0:
            return p
    return n


def _encoder(group_batches, x3, mem3, norm_mix, w_in, ret_norm, gdn_conv, gdn_a_log, gdn_dt_bias,
             gdn_norm, w_out, norm_xa, norm_mem, w_xq, w_xkv, w_xo, norm_ffn, peer_wq, peer_keys,
             peer_u, peer_v, norm_final):
    b, s_len, d = x3.shape
    n_mem = mem3.shape[1]
    t = b * s_len
    depth = w_in.shape[0]
    x = x3.reshape(t, d)
    mem = mem3.reshape(b * n_mem, d)
    n_main = 4 * RET_HEADS * HEAD_DIM + 4 * GDN_HEADS * HEAD_DIM
    ts_in = _pick(s_len, (1024, 512, 256, 128))
    tmm = _pick(b * n_mem, (1024, 512, 256, 128))
    ts = _pick(s_len, (512, 256, 128))
    tb_route = _pick(t, (256, 128))
    tb_dense = _pick(s_len, (512, 256, 128))
    nh = 2 * GDN_HEADS

    for l in range(depth):
        w_main = _deinterleave_rotary_columns(w_in[l, :, :n_main]).astype(BF16)
        w_gate_t = w_in[l, :, n_main:].T.astype(BF16)
        proj, graw = mixer_in_proj(x, norm_mix[l], w_main, w_gate_t, s_len, tm=ts_in)
        proj3 = proj.reshape(b, s_len, n_main)
        o_r = retention(proj3, ret_norm[l], b, s_len, heads=RET_HEADS_PER_STEP)
        gates = gdn_gates(graw, gdn_a_log[l], gdn_dt_bias[l], b, s_len)
        gates = gates.reshape(b, 3, 2, GDN_HEADS, s_len).transpose(0, 3, 2, 1, 4)
        gate_row = jnp.pad(gates.reshape(b, GDN_HEADS, 6, s_len), ((0, 0), (0, 0), (0, 2), (0, 0)))
        gate_col = jnp.swapaxes(gate_row, 2, 3)
        gate_row = gate_row.reshape(b, GDN_HEADS, 8, s_len // GDN_SUPER, GDN_SUPER).transpose(0, 1, 3, 2, 4)
        qkv = gdn_prep(proj3, gdn_conv[l], b, s_len)
        o_g = gdn_core(qkv, proj3, gate_col, gate_row, gdn_norm[l], b, s_len, heads=GDN_HEADS_PER_STEP)
        w_o = w_out[l].astype(BF16)
        n_r = RET_HEADS * HEAD_DIM

        kv = norm_matmul(mem, norm_mem[l], w_xkv[l].astype(BF16), tm=tmm, tn=1024)
        x, pq = mixer_xattn_peerq(x, o_r.reshape(t, -1), o_g.reshape(t, -1), w_o[:n_r], w_o[n_r:],
                                  norm_xa[l], w_xq[l].astype(BF16), kv.reshape(b, n_mem, 2 * d),
                                  w_xo[l].astype(BF16), norm_ffn[l], peer_wq[l].astype(BF16),
                                  b, s_len, ts=ts)

        route = peer_route(pq, peer_keys[l].astype(BF16), tb=tb_route)
        dense = functools.partial(peer_dense_residual, x, norm_ffn[l], route, peer_u[l].astype(BF16),
                                  transpose_cast(peer_v, l), tb=tb_dense, grp=16)
        if l < depth - 1:
            x = dense(None)
    outs, first = [], 0
    for nb in group_batches:
        n_blocks = nb * s_len // tb_dense
        outs.append(dense(norm_final, blocks=(first, n_blocks)).reshape(nb, s_len, d))
        first += n_blocks
    return tuple(outs)


def kernel(x_prompt, x_sample, mem_prompt, mem_sample, norm_mix, w_in, ret_norm, gdn_conv, gdn_a_log,
           gdn_dt_bias, gdn_norm, w_out, norm_xa, norm_mem, w_xq, w_xkv, w_xo, norm_ffn, peer_wq,
           peer_keys, peer_u, peer_v, norm_final):
    weights = (norm_mix, w_in, ret_norm, gdn_conv, gdn_a_log, gdn_dt_bias, gdn_norm, w_out,
               norm_xa, norm_mem, w_xq, w_xkv, w_xo, norm_ffn, peer_wq, peer_keys, peer_u, peer_v,
               norm_final)
    if x_prompt.shape[1:] == x_sample.shape[1:] and mem_prompt.shape[1:] == mem_sample.shape[1:]:
        return _encoder((x_prompt.shape[0], x_sample.shape[0]),
                        jnp.concatenate([x_prompt, x_sample], axis=0),
                        jnp.concatenate([mem_prompt, mem_sample], axis=0), *weights)
    return (_encoder((x_prompt.shape[0],), x_prompt, mem_prompt, *weights)[0],
            _encoder((x_sample.shape[0],), x_sample, mem_sample, *weights)[0])
```

```python
import functools

import jax
import jax.numpy as jnp
from jax import lax
from jax.experimental import pallas as pl
from jax.experimental.pallas import tpu as pltpu

F32 = jnp.float32
BF16 = jnp.bfloat16

EPS = 1e-6
LANES = 128
MXU_DEPTH = 256
HEAD_DIM = 128
RET_HEADS = 4
GDN_HEADS = 4
RET_CHUNK = 128
GDN_CHUNK = 64
GDN_SUPER = 256
GDN_INV_BASE_LOG2 = 1
GDN_HEADS_PER_STEP = 4
RET_HEADS_PER_STEP = 4
ROPE_BASE = 10000.0
XA_HEADS = 4
PEER_HEADS = 8
PEER_NKEYS = 128
PEER_TOPK = 16
TAKEN = 2.0 ** 100
TAKEN_STEP = 2.0 ** 96
FAST_PAD = -(2.0 ** 90)
FAST_SCORE_BOUND = 2.0 ** 80
VMEM_LIMIT = 56 * 1024 * 1024

_NT = (((1,), (1,)), ((), ()))
_TN = (((0,), (0,)), ((), ()))


def _params(n_axes):
    return pltpu.CompilerParams(
        dimension_semantics=("arbitrary",) * n_axes, vmem_limit_bytes=VMEM_LIMIT)


def _dot(a, b):
    return jnp.dot(a, b, preferred_element_type=F32)


def _dot_nt(a, b):
    return lax.dot_general(a, b, _NT, preferred_element_type=F32)


def _dot_tn(a, b):
    return lax.dot_general(a, b, _TN, preferred_element_type=F32)


def _split(a):
    hi = a.astype(BF16)
    return hi, (a - hi.astype(F32)).astype(BF16)


def _dot3(a, b):
    (ah, al), (bh, bl) = a, b
    return _dot(jnp.concatenate([ah, ah, al], axis=1), jnp.concatenate([bh, bl, bh], axis=0))


def _rms(x, g):
    return x * lax.rsqrt(jnp.mean(x * x, axis=-1, keepdims=True) + EPS) * g


def _norm_matmul_body(x_ref, g_ref, w_ref, o_ref, xn_ref):
    @pl.when(pl.program_id(1) == 0)
    def _():
        xn_ref[...] = _rms(x_ref[...], g_ref[...]).astype(BF16)

    o_ref[...] = _dot(xn_ref[...], w_ref[...]).astype(o_ref.dtype)


def norm_matmul(x, g, w, *, tm, tn, out_dtype=F32):
    t, d = x.shape
    n = w.shape[1]
    return pl.pallas_call(
        _norm_matmul_body,
        out_shape=jax.ShapeDtypeStruct((t, n), out_dtype),
        grid=(t // tm, n // tn),
        in_specs=[pl.BlockSpec((tm, d), lambda i, j: (i, 0)),
                  pl.BlockSpec((1, d), lambda i, j: (0, 0)),
                  pl.BlockSpec((d, tn), lambda i, j: (0, j))],
        out_specs=pl.BlockSpec((tm, tn), lambda i, j: (i, j)),
        scratch_shapes=[pltpu.VMEM((tm, d), BF16)],
        compiler_params=_params(2),
        name="norm_matmul",
    )(x, g.reshape(1, d), w)


def _mixer_in_proj_body(x_ref, g_ref, w_ref, ws_ref, cos_ref, sin_ref, o_ref, os_ref, xn_ref):
    j = pl.program_id(1)
    hd = HEAD_DIM

    @pl.when(j == 0)
    def _():
        xn = _rms(x_ref[...], g_ref[...]).astype(BF16)
        xn_ref[...] = xn
        os_ref[...] = _dot_nt(ws_ref[...], xn)

    o = _dot(xn_ref[...], w_ref[...])

    @pl.when(j == 0)
    def _():
        cos = cos_ref[...]
        sin = sin_ref[...]
        for head in range(2 * RET_HEADS):
            cols = slice(head * hd, (head + 1) * hd)
            x = o[:, cols]
            y = x * cos + pltpu.roll(x, hd // 2, 1) * sin
            o_ref[:, cols] = y * (hd ** -0.5) if head >= RET_HEADS else y

    @pl.when(j > 0)
    def _():
        o_ref[...] = o


def mixer_in_proj(x, g, w, ws_t, s_len, *, tm):
    t, d = x.shape
    n = w.shape[1]
    rows = ws_t.shape[0]
    tn = 2 * RET_HEADS * HEAD_DIM
    cos, sin = _rotary_tables(s_len)
    pos_blocks = s_len // tm
    return pl.pallas_call(
        _mixer_in_proj_body,
        out_shape=(jax.ShapeDtypeStruct((t, n), F32),
                   jax.ShapeDtypeStruct((rows, t), F32)),
        grid=(t // tm, n // tn),
        in_specs=[pl.BlockSpec((tm, d), lambda i, j: (i, 0)),
                  pl.BlockSpec((1, d), lambda i, j: (0, 0)),
                  pl.BlockSpec((d, tn), lambda i, j: (0, j)),
                  pl.BlockSpec((rows, d), lambda i, j: (0, 0)),
                  pl.BlockSpec((tm, HEAD_DIM), lambda i, j: (i % pos_blocks, 0)),
                  pl.BlockSpec((tm, HEAD_DIM), lambda i, j: (i % pos_blocks, 0))],
        out_specs=(pl.BlockSpec((tm, tn), lambda i, j: (i, j)),
                   pl.BlockSpec((rows, tm), lambda i, j: (0, i))),
        scratch_shapes=[pltpu.VMEM((tm, d), BF16)],
        compiler_params=_params(2),
        name="mixer_in_proj",
    )(x, g.reshape(1, d), w, ws_t, cos, sin)


def _retention_body(qr_s, kr_s, v_ref, g_ref, dec_ref, rn_ref, o_ref, of_s, ob_s):
    s_len = qr_s.shape[0]
    hd = HEAD_DIM
    heads = qr_s.shape[1] // hd
    c = RET_CHUNK
    n = s_len // c

    def forward(hh, i, st):
        r = pl.ds(pl.multiple_of(i * c, c), c)
        lanes = slice(hh * hd, (hh + 1) * hd)
        qc, kc, vb = qr_s[r, lanes], kr_s[r, lanes], v_ref[r, lanes].astype(BF16)
        sc = _dot_nt(qc.astype(BF16), kc.astype(BF16)) * dec_ref[hh, 0]
        inter = _dot((qc * dec_ref[hh, 1]).astype(BF16), st.astype(BF16))
        st = st * dec_ref[hh, 5] + _dot_tn((kc * dec_ref[hh, 2]).astype(BF16), vb)
        yield
        return inter + _dot(sc.astype(BF16), vb), st

    def backward(hh, i, st):
        r = pl.ds(pl.multiple_of(i * c, c), c)
        lanes = slice(hh * hd, (hh + 1) * hd)
        qc, kc, vb = qr_s[r, lanes], kr_s[r, lanes], v_ref[r, lanes].astype(BF16)
        inter = _dot((qc * dec_ref[hh, 3]).astype(BF16), st.astype(BF16))
        st = st * dec_ref[hh, 6] + _dot_tn((kc * dec_ref[hh, 4]).astype(BF16), vb)
        yield
        return inter, st

    def step(t, carry):
        fwd_rows = pl.ds(pl.multiple_of(t * c, c), c)
        bwd_rows = pl.ds(pl.multiple_of((n - 1 - t) * c, c), c)
        gens = []
        for hh in range(heads):
            gens.append(forward(hh, t, carry[2 * hh]))
            gens.append(backward(hh, n - 1 - t, carry[2 * hh + 1]))
        res = _lockstep(gens)
        for hh in range(heads):
            lanes = slice(hh * hd, (hh + 1) * hd)
            of_s[fwd_rows, lanes] = res[2 * hh][0]
            ob_s[bwd_rows, lanes] = res[2 * hh + 1][0]
        return tuple(st for _, st in res)

    zero = jnp.zeros((hd, hd), F32)
    lax.fori_loop(0, n, step, (zero,) * (2 * heads))

    for hh in range(heads):
        lanes = slice(hh * hd, (hh + 1) * hd)
        o = of_s[:, lanes] + ob_s[:, lanes]
        mu = jnp.mean(o, axis=-1, keepdims=True)
        var = jnp.mean(jnp.square(o - mu), axis=-1, keepdims=True)
        y = (o - mu) * lax.rsqrt(var + EPS)
        y = y * rn_ref[:, lanes] * jax.nn.silu(g_ref[:, lanes])
        o_ref[:, lanes] = y.astype(o_ref.dtype)


def _deinterleave_rotary_columns(w):
    n_qk = 2 * RET_HEADS * HEAD_DIM
    qk = w[:, :n_qk].reshape(w.shape[0], 2 * RET_HEADS, HEAD_DIM // 2, 2)
    qk = jnp.swapaxes(qk, 2, 3).reshape(w.shape[0], n_qk)
    return jnp.concatenate([qk, w[:, n_qk:]], axis=1)


def _rotary_tables(s_len):
    d = HEAD_DIM
    inv = ROPE_BASE ** (-jnp.arange(0, d, 2, dtype=F32) / d)
    ang = jnp.arange(s_len, dtype=F32)[:, None] * inv[None, :]
    cos = jnp.concatenate([jnp.cos(ang), jnp.cos(ang)], axis=1)
    sin = jnp.concatenate([-jnp.sin(ang), jnp.sin(ang)], axis=1)
    return cos, sin


def _retention_tables():
    d = HEAD_DIM
    h = jnp.arange(RET_HEADS, dtype=F32)
    lg_f = jnp.log1p(-jnp.exp2(-5.0 - h))[:, None, None]
    lg_b = jnp.log1p(-jnp.exp2(-5.5 - h))[:, None, None]
    c = RET_CHUNK
    pos = jnp.arange(c, dtype=F32)
    diff = (pos[:, None] - pos[None, :])[None]
    dmat = jnp.where(diff >= 0, jnp.exp(lg_f * jnp.where(diff >= 0, diff, 0.0)),
                     jnp.exp(lg_b * jnp.where(diff < 0, -diff, 0.0)))
    col = lambda v: jnp.broadcast_to(v, (RET_HEADS, c, d))
    p = pos[None, :, None]
    dec = jnp.stack([
        dmat,
        col(jnp.exp(lg_f * (p + 1.0))),
        col(jnp.exp(lg_f * (c - 1.0 - p))),
        col(jnp.exp(lg_b * (c - p))),
        col(jnp.exp(lg_b * p)),
        col(jnp.exp(lg_f * c)),
        col(jnp.exp(lg_b * c)),
    ], axis=1)
    return dec


def retention(proj, ret_norm, b, s_len, *, heads):
    dec = _retention_tables()
    hd = HEAD_DIM
    w = heads * hd
    ng = RET_HEADS // heads
    head_blk = lambda off: pl.BlockSpec((None, s_len, w), lambda i, h: (i, 0, off + h))
    return pl.pallas_call(
        _retention_body,
        out_shape=jax.ShapeDtypeStruct((b, s_len, RET_HEADS * hd), BF16),
        grid=(b, ng),
        in_specs=[head_blk(0), head_blk(ng), head_blk(2 * ng), head_blk(3 * ng),
                  pl.BlockSpec((heads, 7, RET_CHUNK, hd), lambda i, h: (h, 0, 0, 0)),
                  pl.BlockSpec((1, w), lambda i, h: (0, h))],
        out_specs=pl.BlockSpec((None, s_len, w), lambda i, h: (i, 0, h)),
        scratch_shapes=[pltpu.VMEM((s_len, w), F32)] * 2,
        compiler_params=_params(2),
        name="retention",
    )(proj, proj, proj, proj, dec, ret_norm.reshape(1, -1))


def _gdn_gates_body(raw_ref, al_ref, dt_ref, o_ref):
    s_len = raw_ref.shape[-1]
    nh = 2 * GDN_HEADS
    gb = raw_ref[0:nh, :]
    ga = raw_ref[nh:2 * nh, :]
    beta = jax.nn.sigmoid(gb)
    g = -jnp.exp(al_ref[...]) * jax.nn.softplus(ga + dt_ref[...])
    pos = lax.broadcasted_iota(jnp.int32, (nh, s_len), 1) & (GDN_CHUNK - 1)
    fwd = g
    rev = g
    k = 1
    while k < GDN_CHUNK:
        fwd = fwd + jnp.where(pos >= k, pltpu.roll(fwd, k, 1), 0.0)
        rev = rev + jnp.where(pos < GDN_CHUNK - k, pltpu.roll(rev, s_len - k, 1), 0.0)
        k *= 2
    total = fwd + rev - g
    for h in range(GDN_HEADS):
        for d, cum in enumerate((fwd, rev)):
            r = d * GDN_HEADS + h
            o_ref[h, 3 * d:3 * d + 1, :] = cum[r:r + 1, :]
            o_ref[h, 3 * d + 1:3 * d + 2, :] = beta[r:r + 1, :]
            o_ref[h, 3 * d + 2:3 * d + 3, :] = total[r:r + 1, :]
        o_ref[h, 6:8, :] = jnp.zeros((2, s_len), F32)


def gdn_gates(raw, a_log, dt_bias, b, s_len):
    nh = 2 * GDN_HEADS
    return pl.pallas_call(
        _gdn_gates_body,
        out_shape=jax.ShapeDtypeStruct((b, GDN_HEADS, 8, s_len), F32),
        grid=(b,),
        in_specs=[pl.BlockSpec((2 * nh, s_len), lambda i: (0, i)),
                  pl.BlockSpec((nh, 1), lambda i: (0, 0)),
                  pl.BlockSpec((nh, 1), lambda i: (0, 0))],
        out_specs=pl.BlockSpec((None, GDN_HEADS, 8, s_len), lambda i: (i, 0, 0, 0)),
        compiler_params=_params(1),
        name="gdn_gates",
    )(raw, a_log.reshape(nh, 1), dt_bias.reshape(nh, 1))


def _gdn_prep_body(x_ref, w_ref, o_ref):
    s_len = x_ref.shape[0]
    j = pl.program_id(1)
    x = x_ref[...]
    t = lax.broadcasted_iota(jnp.int32, x.shape, 0)
    prev = jnp.where(t == 0, 0.0, pltpu.roll(x, 1, 0))
    nxt = jnp.where(t == s_len - 1, 0.0, pltpu.roll(x, s_len - 1, 0))
    conv = prev * w_ref[0:1, :] + x * w_ref[1:2, :] + nxt * w_ref[2:3, :]
    act = jax.nn.silu(conv)
    inv = lax.rsqrt(jnp.sum(act * act, axis=-1, keepdims=True) + EPS)
    is_q = j < GDN_HEADS
    is_v = j >= 2 * GDN_HEADS
    scale = jnp.where(is_v, 1.0, jnp.where(is_q, inv * (HEAD_DIM ** -0.5), inv))
    o_ref[...] = act * scale


def gdn_prep(proj, conv_w, b, s_len):
    hd = HEAD_DIM
    nblk = 3 * GDN_HEADS
    first = 4 * RET_HEADS
    return pl.pallas_call(
        _gdn_prep_body,
        out_shape=jax.ShapeDtypeStruct((b, s_len, nblk * hd), F32),
        grid=(b, nblk),
        in_specs=[pl.BlockSpec((None, s_len, hd), lambda i, j: (i, 0, first + j)),
                  pl.BlockSpec((3, hd), lambda i, j: (0, j))],
        out_specs=pl.BlockSpec((None, s_len, hd), lambda i, j: (i, 0, j)),
        compiler_params=_params(2),
        name="gdn_prep",
    )(proj, conv_w)


def _lockstep(gens):
    results = [None] * len(gens)
    live = list(range(len(gens)))
    while live:
        for i in list(live):
            try:
                next(gens[i])
            except StopIteration as done:
                results[i] = done.value
                live.remove(i)
    return results


def _merge_run(a, b):
    ia = ib = 0
    while ia < len(a) or ib < len(b):
        if ib >= len(b) or (ia < len(a) and (ia + 1) * len(b) <= (ib + 1) * len(a)):
            a[ia]()
            ia += 1
        else:
            b[ib]()
            ib += 1


def _tri_inverse(m, ri, ci):
    def same_block(log2_size):
        return (ri >> log2_size) == (ci >> log2_size)

    base = GDN_INV_BASE_LOG2
    eye = (ri == ci).astype(F32)
    diag = jnp.where(same_block(base), -m, 0.0)
    nm = _split(diag)
    p = eye + diag
    for _ in range(base - 1):
        nm = _split(_dot3(nm, nm))
        yield
        p = p + _dot3(_split(p), nm)
        yield
    for size in range(base, GDN_CHUNK.bit_length() - 1):
        off = jnp.where(same_block(size + 1) & jnp.logical_not(same_block(size)), m, 0.0)
        p16 = p.astype(BF16)
        pc = _dot(p16, off.astype(BF16)).astype(BF16)
        yield
        p = p - _dot(pc, p16)
        yield
    return p


def _gdn_super_chunk(q, k, v, gcol, grow, bcol, glcol, st, rev):
    r = GDN_SUPER
    c = GDN_CHUNK
    ri = lax.broadcasted_iota(jnp.int32, (r, r), 0)
    ci = lax.broadcasted_iota(jnp.int32, (r, r), 1)
    same = (ri >> 6) == (ci >> 6)
    if rev:
        incl = same & (ri <= ci)
        strict = same & (ri < ci)
    else:
        incl = same & (ri >= ci)
        strict = same & (ri > ci)
    decay = jnp.exp(jnp.where(incl, gcol - grow, -jnp.inf))
    kb = k * bcol
    vb = v * bcol
    k16 = k.astype(BF16)
    m = jnp.where(strict, _dot_nt(kb.astype(BF16), k16) * decay, 0.0)
    qk = jnp.where(incl, _dot_nt(q.astype(BF16), k16) * decay, 0.0).astype(BF16)
    yield
    t = yield from _tri_inverse(m, ri, ci)
    eg = jnp.exp(gcol)
    rhs = jnp.concatenate([vb, kb * eg], axis=1).astype(BF16)
    uw = _dot(t.astype(BF16), rhs)
    u = uw[:, :HEAD_DIM]
    w = uw[:, HEAD_DIM:].astype(BF16)
    qd = (q * eg).astype(BF16)
    kt = (k * jnp.exp(glcol - gcol)).astype(BF16)
    yield
    outs = [None] * (r // c)
    order = range(r // c - 1, -1, -1) if rev else range(r // c)
    for i in order:
        sl = slice(i * c, (i + 1) * c)
        s16 = st.astype(BF16)
        vn = u[sl] - _dot(w[sl], s16)
        vn16 = vn.astype(BF16)
        yield
        pair = slice((i // 2) * 2 * c, (i // 2 + 1) * 2 * c)
        zero = jnp.zeros_like(vn16)
        vpad = jnp.concatenate([zero, vn16] if i % 2 else [vn16, zero], axis=0)
        outs[i] = _dot(qd[sl], s16) + _dot(qk[sl, pair], vpad)
        st = st * jnp.exp(glcol[i * c:i * c + 1, :]) + _dot_tn(kt[sl], vn16)
        yield
    return jnp.concatenate(outs, axis=0), st


def _gdn_core_body(q_ref, k_ref, v_ref, z_ref, row_ref, gn_ref, o_ref, col_s, of_s, ob_s):
    s_len = q_ref.shape[0]
    hd = HEAD_DIM
    heads = q_ref.shape[1] // hd
    r = GDN_SUPER
    n = s_len // r

    for hh in range(heads):
        rows8 = jnp.concatenate([row_ref[hh, i] for i in range(n)], axis=1)
        padded = jnp.concatenate([rows8, jnp.zeros((LANES - 8, s_len), F32)], axis=0)
        col_s[hh] = padded.T

    def one(hh, i, st, rev):
        base = 3 if rev else 0
        rows = pl.ds(pl.multiple_of(i * r, r), r)
        lanes = slice(hh * hd, (hh + 1) * hd)
        cols = col_s[hh, rows, :]
        gcol = cols[:, base:base + 1]
        bcol = cols[:, base + 1:base + 2]
        glcol = cols[:, base + 2:base + 3]
        grow = row_ref[hh, i][base:base + 1, :]
        return _gdn_super_chunk(q_ref[rows, lanes], k_ref[rows, lanes], v_ref[rows, lanes],
                                gcol, grow, bcol, glcol, st, rev)

    def step(t, carry):
        fwd_rows = pl.ds(pl.multiple_of(t * r, r), r)
        bwd_rows = pl.ds(pl.multiple_of((n - 1 - t) * r, r), r)
        gens = []
        for hh in range(heads):
            gens.append(one(hh, t, carry[2 * hh], False))
            gens.append(one(hh, n - 1 - t, carry[2 * hh + 1], True))
        res = _lockstep(gens)
        for hh in range(heads):
            lanes = slice(hh * hd, (hh + 1) * hd)
            of_s[fwd_rows, lanes] = res[2 * hh][0]
            ob_s[bwd_rows, lanes] = res[2 * hh + 1][0]
        return tuple(st for _, st in res)

    zero = jnp.zeros((hd, hd), F32)
    lax.fori_loop(0, n, step, (zero,) * (2 * heads))
    for hh in range(heads):
        lanes = slice(hh * hd, (hh + 1) * hd)
        o = of_s[:, lanes] + ob_s[:, lanes]
        y = o * lax.rsqrt(jnp.mean(o * o, axis=-1, keepdims=True) + EPS) * gn_ref[...]
        o_ref[:, lanes] = (y * jax.nn.silu(z_ref[:, lanes])).astype(o_ref.dtype)


def gdn_core(qkv, proj, gates, gdn_norm, b, s_len, *, heads):
    hd = HEAD_DIM
    n_super = s_len // GDN_SUPER
    gate_row = gates.reshape(b, GDN_HEADS, 8, n_super, GDN_SUPER).transpose(0, 1, 3, 2, 4)
    w = heads * hd
    ng = GDN_HEADS // heads
    z_first = (4 * RET_HEADS + 3 * GDN_HEADS) // heads
    qkv_blk = lambda off: pl.BlockSpec((None, s_len, w), lambda i, h: (i, 0, off + h))
    return pl.pallas_call(
        _gdn_core_body,
        out_shape=jax.ShapeDtypeStruct((b, s_len, GDN_HEADS * hd), BF16),
        grid=(b, ng),
        in_specs=[qkv_blk(0), qkv_blk(ng), qkv_blk(2 * ng),
                  pl.BlockSpec((None, s_len, w), lambda i, h: (i, 0, z_first + h),
                               pipeline_mode=pl.Buffered(1)),
                  pl.BlockSpec((None, heads, n_super, 8, GDN_SUPER), lambda i, h: (i, h, 0, 0, 0)),
                  pl.BlockSpec((1, hd), lambda i, h: (0, 0))],
        out_specs=pl.BlockSpec((None, s_len, w), lambda i, h: (i, 0, h)),
        scratch_shapes=[pltpu.VMEM((heads, s_len, LANES), F32),
                        pltpu.VMEM((s_len, w), F32), pltpu.VMEM((s_len, w), F32)],
        compiler_params=_params(2),
        name="gdn_core",
    )(qkv, qkv, qkv, proj, gate_row, gdn_norm.reshape(1, hd))


def _mixer_xattn_body(x_ref, a1_ref, a2_ref, w1_ref, w2_ref, gxa_ref, wq_ref, kv_ref, wo_ref,
                      gffn_ref, wpq_ref, o_ref, pq_ref):
    x = x_ref[...] + _dot(a1_ref[...], w1_ref[...]) + _dot(a2_ref[...], w2_ref[...])
    d = x.shape[1]
    dh = d // XA_HEADS
    q = _dot(_rms(x, gxa_ref[...]).astype(BF16), wq_ref[...])
    outs = []
    for h in range(XA_HEADS):
        kh = kv_ref[:, h * dh:(h + 1) * dh].astype(BF16)
        vh = kv_ref[:, d + h * dh:d + (h + 1) * dh].astype(BF16)
        sc = _dot_nt(q[:, h * dh:(h + 1) * dh].astype(BF16), kh) * (dh ** -0.5)
        p = jax.nn.softmax(sc, axis=-1)
        outs.append(_dot(p.astype(BF16), vh).astype(BF16))
    x = x + _dot(jnp.concatenate(outs, axis=1), wo_ref[...])
    o_ref[...] = x
    pq_ref[...] = _dot(_rms(x, gffn_ref[...]).astype(BF16), wpq_ref[...])


def mixer_xattn_peerq(x, a1, a2, w1, w2, g_xa, wq, kv, wo, g_ffn, wpq, b, s_len, *, ts):
    t, d = x.shape
    k1, k2 = a1.shape[1], a2.shape[1]
    m = kv.shape[1]
    npq = wpq.shape[1]
    nblk = s_len // ts
    tok = lambda n: pl.BlockSpec((ts, n), lambda i, j: (i * nblk + j, 0))
    whole = lambda r, c: pl.BlockSpec((r, c), lambda i, j: (0, 0))
    return pl.pallas_call(
        _mixer_xattn_body,
        out_shape=(jax.ShapeDtypeStruct((t, d), F32), jax.ShapeDtypeStruct((t, npq), F32)),
        grid=(b, nblk),
        in_specs=[tok(d), tok(k1), tok(k2), whole(k1, d), whole(k2, d),
                  whole(1, d), whole(d, d),
                  pl.BlockSpec((None, m, 2 * d), lambda i, j: (i, 0, 0)),
                  whole(d, d), whole(1, d), whole(d, npq)],
        out_specs=(tok(d), tok(npq)),
        compiler_params=_params(2),
        name="mixer_xattn_peerq",
    )(x, a1, a2, w1, w2, g_xa.reshape(1, d), wq, kv, wo, g_ffn.reshape(1, d), wpq)


def _extract_topk(s, k, break_ties):
    vals = []
    if break_ties:
        n_rows = s.shape[0]
        iota = lax.broadcasted_iota(jnp.int32, s.shape, 0).astype(F32)
        rank = jnp.full(s.shape, float(k), F32)
        for r in range(k):
            m = jnp.max(s, axis=0, keepdims=True)
            idx = jnp.min(jnp.where(s == m, iota, float(n_rows)), axis=0, keepdims=True)
            hit = iota == idx
            rank = jnp.where(hit, float(r), rank)
            s = jnp.where(hit, -jnp.inf, s)
            vals.append(m)
        taken = rank < float(k)
    else:
        for r in range(k):
            m = jnp.max(s, axis=0, keepdims=True)
            s = jnp.where(s == m, -(TAKEN + r * TAKEN_STEP), s)
            vals.append(m)
        taken = s <= -TAKEN
        rank = jnp.where(taken, s * (-1.0 / TAKEN_STEP) - TAKEN / TAKEN_STEP, float(k))
    count = jnp.sum(jnp.where(taken, 1.0, 0.0), axis=0, keepdims=True)
    return rank, vals, count


def _route_tokens(q, keys_ref, break_ties):
    nk = PEER_NKEYS
    kk = PEER_TOPK
    s1 = _dot_nt(keys_ref[0], q[:, :nk])
    s2 = _dot_nt(keys_ref[1], q[:, nk:])
    r1, v1, n_sel1 = _extract_topk(s1, kk, break_ties)
    r2, v2, n_sel2 = _extract_topk(s2, kk, break_ties)
    v1m = jnp.concatenate(v1, axis=0)
    v2m = jnp.concatenate(v2, axis=0)
    pad = -jnp.inf if break_ties else FAST_PAD
    row8 = lax.broadcasted_iota(jnp.int32, (8, LANES), 0)
    groups = [v1[0] + v2m[0:8], v1[0] + v2m[8:16]]
    for a in range(1, 8):
        groups.append(jnp.where(row8 < kk // (a + 1), v1[a] + v2m[0:8], pad))
    groups.append(v1m[8:16] + v2[0])
    cand = jnp.concatenate(groups, axis=0)
    rc, _, n_selc = _extract_topk(cand, kk, break_ties)
    sel = rc < float(kk)
    top = v1[0] + v2[0]
    z = jnp.sum(jnp.where(sel, jnp.exp(cand - top), 0.0), axis=0, keepdims=True)
    sel_f = jnp.where(sel, 1.0, 0.0)
    n1 = jnp.zeros((nk, LANES), F32)
    for a in range(kk):
        if a == 0:
            cnt = jnp.sum(sel_f[0:16], axis=0, keepdims=True)
        elif a < 8:
            cnt = jnp.sum(sel_f[8 + 8 * a:16 + 8 * a], axis=0, keepdims=True)
        else:
            cnt = sel_f[64 + a:65 + a]
        n1 = jnp.where(r1 == float(a), cnt, n1)
    c1 = jnp.where(r1 < float(kk), jnp.exp(s1 - v1[0]) / z, 0.0)
    e2 = jnp.exp(s2 - v2[0])
    exact = ((n_sel1 == float(kk)) & (n_sel2 == float(kk)) & (n_selc == float(kk))
             & (v1[kk - 1] > -FAST_SCORE_BOUND) & (v2[kk - 1] > -FAST_SCORE_BOUND))
    return c1, n1, e2, r2, jnp.where(exact, 1.0, 0.0)


def _peer_route_body(q_ref, keys_ref, c1_ref, n1_ref, e2_ref, r2_ref):
    tb = q_ref.shape[0]

    def run(break_ties):
        flags = []
        for blk in range(tb // LANES):
            tok = slice(blk * LANES, (blk + 1) * LANES)
            c1, n1, e2, r2, ok = _route_tokens(q_ref[tok, :].astype(BF16), keys_ref, break_ties)
            c1_ref[:, tok] = c1
            n1_ref[:, tok] = n1
            e2_ref[:, tok] = e2.astype(e2_ref.dtype)
            r2_ref[:, tok] = r2.astype(r2_ref.dtype)
            flags.append(ok)
        return jnp.min(jnp.concatenate(flags, axis=1))

    all_exact = run(False)

    @pl.when(all_exact < 0.5)
    def _():
        run(True)


def peer_route(q, keys, *, tb):
    t = q.shape[0]
    nk = PEER_NKEYS
    out = lambda dt: jax.ShapeDtypeStruct((PEER_HEADS, nk, t), dt)
    ospec = pl.BlockSpec((None, nk, tb), lambda i, h: (h, 0, i))
    return pl.pallas_call(
        _peer_route_body,
        out_shape=(out(F32), out(F32), out(BF16), out(BF16)),
        grid=(t // tb, PEER_HEADS),
        in_specs=[pl.BlockSpec((tb, 2 * nk), lambda i, h: (i, h)),
                  pl.BlockSpec((None, 2, nk, nk), lambda i, h: (h, 0, 0, 0))],
        out_specs=(ospec,) * 4,
        compiler_params=_params(2),
        name="peer_route",
    )(q, keys)


def _gelu_tanh(x):
    c0 = 0.7978845608028654
    c1 = 0.7978845608028654 * 0.044715
    return (0.5 * x) * (1.0 + jnp.tanh(x * (c0 + c1 * (x * x))))


def _peer_dense_body(x_ref, g_ref, c1_ref, n1_ref, e2_ref, r2_ref, u_ref, vt_ref, *rest, final):
    if final:
        gf_ref, o_ref, xn_s, acc_s = rest
    else:
        o_ref, xn_s, acc_s = rest
    j = pl.program_id(1)
    nk = PEER_NKEYS
    tb = x_ref.shape[0]
    grp = u_ref.shape[0] // nk
    sub = e2_ref.shape[2]

    @pl.when(j == 0)
    def _():
        xn_s[...] = _rms(x_ref[...], g_ref[...]).T.astype(BF16)
        acc_s[...] = jnp.zeros_like(acc_s)

    per = MXU_DEPTH // nk
    n_chunks = grp // per
    d_model = vt_ref.shape[0]
    xn = xn_s[...]
    scores, gates, weights, chunk_w = {}, {}, {}, {}

    def score_pieces(c):
        def piece(ii):
            scores[ii] = _dot(u_ref[ii * nk:(ii + 1) * nk, :], xn)
        return [functools.partial(piece, c * per + i) for i in range(per)]

    def gate_pieces(c):
        def piece(ii, h):
            row = pl.ds(j * grp + ii, 1)
            c1 = jnp.broadcast_to(c1_ref[h, row, :], (sub, tb)).astype(e2_ref.dtype)[None]
            n1 = jnp.broadcast_to(n1_ref[h, row, :], (sub, tb)).astype(e2_ref.dtype)[None]
            e2 = e2_ref[h]
            term = jnp.where(r2_ref[h] < n1, e2, jnp.zeros_like(e2)) * c1
            gates[ii] = term if h == 0 else gates[ii] + term
        return [functools.partial(piece, c * per + i, h) for i in range(per) for h in range(PEER_HEADS)]

    def act_pieces(c):
        def piece(ii):
            act = _gelu_tanh(scores.pop(ii).astype(e2_ref.dtype))
            weights[ii] = act * gates.pop(ii).reshape(nk, tb)
        return [functools.partial(piece, c * per + i) for i in range(per)]

    def value_pieces(c):
        ex = slice(c * MXU_DEPTH, (c + 1) * MXU_DEPTH)

        def piece(m):
            if m == 0:
                chunk_w[c] = jnp.concatenate([weights.pop(c * per + i) for i in range(per)], axis=0)
            rows = slice(m * MXU_DEPTH, (m + 1) * MXU_DEPTH)
            acc_s[rows, :] += _dot(vt_ref[rows, ex], chunk_w[c])
        return [functools.partial(piece, m) for m in range(d_model // MXU_DEPTH)]

    _merge_run(score_pieces(0), gate_pieces(0))
    for c in range(n_chunks):
        mxu = (score_pieces(c + 1) if c + 1 < n_chunks else []) + (value_pieces(c - 1) if c else [])
        vpu = act_pieces(c) + (gate_pieces(c + 1) if c + 1 < n_chunks else [])
        _merge_run(mxu, vpu)
    _merge_run(value_pieces(n_chunks - 1), [])

    @pl.when(j == pl.num_programs(1) - 1)
    def _():
        y = x_ref[...] + acc_s[...].T
        if final:
            y = _rms(y, gf_ref[...])
        o_ref[...] = y


def peer_dense_residual(x, g, route, u, layer, vt, g_final, *, tb, grp, blocks=None):
    t, d = x.shape
    nk = PEER_NKEYS
    eb = grp * nk
    first, count = blocks if blocks is not None else (0, t // tb)
    c1, n1, e2, r2 = route
    sub = 32 // jnp.dtype(e2.dtype).itemsize
    e2 = e2.reshape(PEER_HEADS, nk // sub, sub, t)
    r2 = r2.reshape(PEER_HEADS, nk // sub, sub, t)
    rspec = pl.BlockSpec((PEER_HEADS, nk, tb), lambda i, j: (0, 0, first + i))
    pspec = pl.BlockSpec((PEER_HEADS, nk // sub, sub, tb), lambda i, j: (0, 0, 0, first + i))
    in_specs = [pl.BlockSpec((tb, d), lambda i, j: (first + i, 0)),
                pl.BlockSpec((1, d), lambda i, j: (0, 0)),
                rspec, rspec, pspec, pspec,
                pl.BlockSpec((None, eb, d), lambda i, j: (layer, j, 0)),
                pl.BlockSpec((d, eb), lambda i, j: (0, j))]
    args = [x, g.reshape(1, d), c1, n1, e2, r2, u, vt]
    final = g_final is not None
    if final:
        in_specs.append(pl.BlockSpec((1, d), lambda i, j: (0, 0)))
        args.append(g_final.reshape(1, d))
    return pl.pallas_call(
        functools.partial(_peer_dense_body, final=final),
        out_shape=jax.ShapeDtypeStruct((count * tb, d), F32),
        grid=(count, (nk * nk) // eb),
        in_specs=in_specs,
        out_specs=pl.BlockSpec((tb, d), lambda i, j: (i, 0)),
        scratch_shapes=[pltpu.VMEM((d, tb), BF16),
                        pltpu.VMEM((d, tb), F32)],
        compiler_params=_params(2),
        name="peer_dense_final" if final else "peer_dense",
    )(*args)


def _transpose_cast_body(x_ref, o_ref):
    o_ref[...] = x_ref[...].T.astype(o_ref.dtype)


def transpose_cast(tables, layer):
    _, rows, cols = tables.shape
    tr = _pick(rows, (1024, 512, 256, 128))
    return pl.pallas_call(
        _transpose_cast_body,
        out_shape=jax.ShapeDtypeStruct((cols, rows), BF16),
        grid=(rows // tr,),
        in_specs=[pl.BlockSpec((None, tr, cols), lambda i: (layer, i, 0))],
        out_specs=pl.BlockSpec((cols, tr), lambda i: (0, i)),
        compiler_params=_params(1),
        name="transpose_cast",
    )(tables)


def _pick(n, prefs):
    for p in prefs:
        if n % p == 0:
            return p
    return n


def _encoder(group_batches, x3, mem3, norm_mix, w_in, ret_norm, gdn_conv, gdn_a_log, gdn_dt_bias,
             gdn_norm, w_out, norm_xa, norm_mem, w_xq, w_xkv, w_xo, norm_ffn, peer_wq, peer_keys,
             peer_u, peer_v, norm_final):
    b, s_len, d = x3.shape
    n_mem = mem3.shape[1]
    t = b * s_len
    depth = w_in.shape[0]
    x = x3.reshape(t, d)
    mem = mem3.reshape(b * n_mem, d)
    n_main = 4 * RET_HEADS * HEAD_DIM + 4 * GDN_HEADS * HEAD_DIM
    ts_in = _pick(s_len, (1024, 512, 256, 128))
    tmm = _pick(b * n_mem, (1024, 512, 256, 128))
    ts = _pick(s_len, (512, 256, 128))
    tb_route = _pick(t, (256, 128))
    tb_dense = _pick(s_len, (512, 256, 128))
    peer_u16 = peer_u.astype(BF16)

    for l in range(depth):
        w_main = _deinterleave_rotary_columns(w_in[l, :, :n_main]).astype(BF16)
        w_gate_t = w_in[l, :, n_main:].T.astype(BF16)
        proj, graw = mixer_in_proj(x, norm_mix[l], w_main, w_gate_t, s_len, tm=ts_in)
        proj3 = proj.reshape(b, s_len, n_main)
        o_r = retention(proj3, ret_norm[l], b, s_len, heads=RET_HEADS_PER_STEP)
        gates = gdn_gates(graw, gdn_a_log[l], gdn_dt_bias[l], b, s_len)
        qkv = gdn_prep(proj3, gdn_conv[l], b, s_len)
        o_g = gdn_core(qkv, proj3, gates, gdn_norm[l], b, s_len, heads=GDN_HEADS_PER_STEP)
        w_o = w_out[l].astype(BF16)
        n_r = RET_HEADS * HEAD_DIM

        kv = norm_matmul(mem, norm_mem[l], w_xkv[l].astype(BF16), tm=tmm, tn=1024)
        x, pq = mixer_xattn_peerq(x, o_r.reshape(t, -1), o_g.reshape(t, -1), w_o[:n_r], w_o[n_r:],
                                  norm_xa[l], w_xq[l].astype(BF16), kv.reshape(b, n_mem, 2 * d),
                                  w_xo[l].astype(BF16), norm_ffn[l], peer_wq[l].astype(BF16),
                                  b, s_len, ts=ts)

        route = peer_route(pq, peer_keys[l].astype(BF16), tb=tb_route)
        dense = functools.partial(peer_dense_residual, x, norm_ffn[l], route, peer_u16, l,
                                  transpose_cast(peer_v, l), tb=tb_dense, grp=16)
        if l < depth - 1:
            x = dense(None)
    outs, first = [], 0
    for nb in group_batches:
        n_blocks = nb * s_len // tb_dense
        outs.append(dense(norm_final, blocks=(first, n_blocks)).reshape(nb, s_len, d))
        first += n_blocks
    return tuple(outs)


def kernel(x_prompt, x_sample, mem_prompt, mem_sample, norm_mix, w_in, ret_norm, gdn_conv, gdn_a_log,
           gdn_dt_bias, gdn_norm, w_out, norm_xa, norm_mem, w_xq, w_xkv, w_xo, norm_ffn, peer_wq,
           peer_keys, peer_u, peer_v, norm_final):
    weights = (norm_mix, w_in, ret_norm, gdn_conv, gdn_a_log, gdn_dt_bias, gdn_norm, w_out,
               norm_xa, norm_mem, w_xq, w_xkv, w_xo, norm_ffn, peer_wq, peer_keys, peer_u, peer_v,
               norm_final)
    if x_prompt.shape[1:] == x_sample.shape[1:] and mem_prompt.shape[1:] == mem_sample.shape[1:]:
        return _encoder((x_prompt.shape[0], x_sample.shape[0]),
                        jnp.concatenate([x_prompt, x_sample], axis=0),
                        jnp.concatenate([mem_prompt, mem_sample], axis=0), *weights)
    return (_encoder((x_prompt.shape[0],), x_prompt, mem_prompt, *weights)[0],
            _encoder((x_sample.shape[0],), x_sample, mem_sample, *weights)[0])
```

```python
import functools

import jax
import jax.numpy as jnp
from jax import lax
from jax.experimental import pallas as pl
from jax.experimental.pallas import tpu as pltpu

F32 = jnp.float32
BF16 = jnp.bfloat16

EPS = 1e-6
LANES = 128
MXU_DEPTH = 256
HEAD_DIM = 128
RET_HEADS = 4
GDN_HEADS = 4
RET_CHUNK = 128
GDN_CHUNK = 64
GDN_SUPER = 256
GDN_INV_BASE_LOG2 = 1
GDN_HEADS_PER_STEP = 4
RET_HEADS_PER_STEP = 4
ROPE_BASE = 10000.0
XA_HEADS = 4
PEER_HEADS = 8
PEER_NKEYS = 128
PEER_TOPK = 16
TAKEN = 2.0 ** 100
TAKEN_STEP = 2.0 ** 96
FAST_PAD = -(2.0 ** 90)
FAST_SCORE_BOUND = 2.0 ** 80
VMEM_LIMIT = 56 * 1024 * 1024

_NT = (((1,), (1,)), ((), ()))
_TN = (((0,), (0,)), ((), ()))


def _params(n_axes):
    return pltpu.CompilerParams(
        dimension_semantics=("arbitrary",) * n_axes, vmem_limit_bytes=VMEM_LIMIT)


def _dot(a, b):
    return jnp.dot(a, b, preferred_element_type=F32)


def _dot_nt(a, b):
    return lax.dot_general(a, b, _NT, preferred_element_type=F32)


def _dot_tn(a, b):
    return lax.dot_general(a, b, _TN, preferred_element_type=F32)


def _split(a):
    hi = a.astype(BF16)
    return hi, (a - hi.astype(F32)).astype(BF16)


def _dot3(a, b):
    (ah, al), (bh, bl) = a, b
    return _dot(jnp.concatenate([ah, ah, al], axis=1), jnp.concatenate([bh, bl, bh], axis=0))


def _rms(x, g):
    return x * lax.rsqrt(jnp.mean(x * x, axis=-1, keepdims=True) + EPS) * g


def _norm_matmul_body(x_ref, g_ref, w_ref, o_ref, xn_ref):
    @pl.when(pl.program_id(1) == 0)
    def _():
        xn_ref[...] = _rms(x_ref[...], g_ref[...]).astype(BF16)

    o_ref[...] = _dot(xn_ref[...], w_ref[...]).astype(o_ref.dtype)


def norm_matmul(x, g, w, *, tm, tn, out_dtype=F32):
    t, d = x.shape
    n = w.shape[1]
    return pl.pallas_call(
        _norm_matmul_body,
        out_shape=jax.ShapeDtypeStruct((t, n), out_dtype),
        grid=(t // tm, n // tn),
        in_specs=[pl.BlockSpec((tm, d), lambda i, j: (i, 0)),
                  pl.BlockSpec((1, d), lambda i, j: (0, 0)),
                  pl.BlockSpec((d, tn), lambda i, j: (0, j))],
        out_specs=pl.BlockSpec((tm, tn), lambda i, j: (i, j)),
        scratch_shapes=[pltpu.VMEM((tm, d), BF16)],
        compiler_params=_params(2),
        name="norm_matmul",
    )(x, g.reshape(1, d), w)


def _mixer_in_proj_body(x_ref, g_ref, w_ref, ws_ref, cos_ref, sin_ref, cw_ref, o_ref, os_ref, xn_ref):
    j = pl.program_id(1)
    hd = HEAD_DIM
    s_len = x_ref.shape[0]

    @pl.when(j == 0)
    def _():
        xn = _rms(x_ref[...], g_ref[...]).astype(BF16)
        xn_ref[...] = xn
        os_ref[...] = _dot_nt(ws_ref[...], xn)

    o_ref[...] = _dot(xn_ref[...], w_ref[...])

    @pl.when(j == 0)
    def _():
        cos = cos_ref[...]
        sin = sin_ref[...]
        for head in range(2 * RET_HEADS):
            cols = slice(head * hd, (head + 1) * hd)
            x = o_ref[:, cols]
            y = x * cos + pltpu.roll(x, hd // 2, 1) * sin
            o_ref[:, cols] = y * (hd ** -0.5) if head >= RET_HEADS else y

    def conv_silu(cols):
        x = o_ref[:, cols]
        t = lax.broadcasted_iota(jnp.int32, x.shape, 0)
        prev = jnp.where(t == 0, 0.0, pltpu.roll(x, 1, 0))
        nxt = jnp.where(t == s_len - 1, 0.0, pltpu.roll(x, s_len - 1, 0))
        return jax.nn.silu(prev * cw_ref[0:1, cols] + x * cw_ref[1:2, cols] + nxt * cw_ref[2:3, cols])

    @pl.when(j == 2)
    def _():
        for head in range(2 * GDN_HEADS):
            cols = slice(head * hd, (head + 1) * hd)
            act = conv_silu(cols)
            inv = lax.rsqrt(jnp.sum(act * act, axis=-1, keepdims=True) + EPS)
            o_ref[:, cols] = act * (inv * (hd ** -0.5) if head < GDN_HEADS else inv)

    @pl.when(j == 3)
    def _():
        for head in range(GDN_HEADS):
            cols = slice(head * hd, (head + 1) * hd)
            o_ref[:, cols] = conv_silu(cols)


def mixer_in_proj(x, g, w, ws_t, conv_w, s_len):
    t, d = x.shape
    n = w.shape[1]
    rows = ws_t.shape[0]
    tn = 2 * RET_HEADS * HEAD_DIM
    assert tn == 2 * GDN_HEADS * HEAD_DIM and n == 4 * tn
    cos, sin = _rotary_tables(s_len)
    conv_w = jnp.pad(conv_w, ((0, 0), (0, 2 * tn - conv_w.shape[1])))
    return pl.pallas_call(
        _mixer_in_proj_body,
        out_shape=(jax.ShapeDtypeStruct((t, n), F32),
                   jax.ShapeDtypeStruct((rows, t), F32)),
        grid=(t // s_len, n // tn),
        in_specs=[pl.BlockSpec((s_len, d), lambda i, j: (i, 0)),
                  pl.BlockSpec((1, d), lambda i, j: (0, 0)),
                  pl.BlockSpec((d, tn), lambda i, j: (0, j)),
                  pl.BlockSpec((rows, d), lambda i, j: (0, 0)),
                  pl.BlockSpec((s_len, HEAD_DIM), lambda i, j: (0, 0)),
                  pl.BlockSpec((s_len, HEAD_DIM), lambda i, j: (0, 0)),
                  pl.BlockSpec((3, tn), lambda i, j: (0, jnp.maximum(j - 2, 0)))],
        out_specs=(pl.BlockSpec((s_len, tn), lambda i, j: (i, j)),
                   pl.BlockSpec((rows, s_len), lambda i, j: (0, i))),
        scratch_shapes=[pltpu.VMEM((s_len, d), BF16)],
        compiler_params=_params(2),
        name="mixer_in_proj",
    )(x, g.reshape(1, d), w, ws_t, cos, sin, conv_w)


def _retention_body(qr_s, kr_s, v_ref, g_ref, dec_ref, rn_ref, o_ref, of_s, ob_s):
    s_len = qr_s.shape[0]
    hd = HEAD_DIM
    heads = qr_s.shape[1] // hd
    c = RET_CHUNK
    n = s_len // c

    def forward(hh, i, st):
        r = pl.ds(pl.multiple_of(i * c, c), c)
        lanes = slice(hh * hd, (hh + 1) * hd)
        qc, kc, vb = qr_s[r, lanes], kr_s[r, lanes], v_ref[r, lanes].astype(BF16)
        sc = _dot_nt(qc.astype(BF16), kc.astype(BF16)) * dec_ref[hh, 0]
        inter = _dot((qc * dec_ref[hh, 1]).astype(BF16), st.astype(BF16))
        st = st * dec_ref[hh, 5] + _dot_tn((kc * dec_ref[hh, 2]).astype(BF16), vb)
        yield
        return inter + _dot(sc.astype(BF16), vb), st

    def backward(hh, i, st):
        r = pl.ds(pl.multiple_of(i * c, c), c)
        lanes = slice(hh * hd, (hh + 1) * hd)
        qc, kc, vb = qr_s[r, lanes], kr_s[r, lanes], v_ref[r, lanes].astype(BF16)
        inter = _dot((qc * dec_ref[hh, 3]).astype(BF16), st.astype(BF16))
        st = st * dec_ref[hh, 6] + _dot_tn((kc * dec_ref[hh, 4]).astype(BF16), vb)
        yield
        return inter, st

    def step(t, carry):
        fwd_rows = pl.ds(pl.multiple_of(t * c, c), c)
        bwd_rows = pl.ds(pl.multiple_of((n - 1 - t) * c, c), c)
        gens = []
        for hh in range(heads):
            gens.append(forward(hh, t, carry[2 * hh]))
            gens.append(backward(hh, n - 1 - t, carry[2 * hh + 1]))
        res = _lockstep(gens)
        for hh in range(heads):
            lanes = slice(hh * hd, (hh + 1) * hd)
            of_s[fwd_rows, lanes] = res[2 * hh][0]
            ob_s[bwd_rows, lanes] = res[2 * hh + 1][0]
        return tuple(st for _, st in res)

    zero = jnp.zeros((hd, hd), F32)
    lax.fori_loop(0, n, step, (zero,) * (2 * heads))

    for hh in range(heads):
        lanes = slice(hh * hd, (hh + 1) * hd)
        o = of_s[:, lanes] + ob_s[:, lanes]
        mu = jnp.mean(o, axis=-1, keepdims=True)
        var = jnp.mean(jnp.square(o - mu), axis=-1, keepdims=True)
        y = (o - mu) * lax.rsqrt(var + EPS)
        y = y * rn_ref[:, lanes] * jax.nn.silu(g_ref[:, lanes])
        o_ref[:, lanes] = y.astype(o_ref.dtype)


def _deinterleave_rotary_columns(w):
    n_qk = 2 * RET_HEADS * HEAD_DIM
    qk = w[:, :n_qk].reshape(w.shape[0], 2 * RET_HEADS, HEAD_DIM // 2, 2)
    qk = jnp.swapaxes(qk, 2, 3).reshape(w.shape[0], n_qk)
    return jnp.concatenate([qk, w[:, n_qk:]], axis=1)


def _rotary_tables(s_len):
    d = HEAD_DIM
    inv = ROPE_BASE ** (-jnp.arange(0, d, 2, dtype=F32) / d)
    ang = jnp.arange(s_len, dtype=F32)[:, None] * inv[None, :]
    cos = jnp.concatenate([jnp.cos(ang), jnp.cos(ang)], axis=1)
    sin = jnp.concatenate([-jnp.sin(ang), jnp.sin(ang)], axis=1)
    return cos, sin


def _retention_tables():
    d = HEAD_DIM
    h = jnp.arange(RET_HEADS, dtype=F32)
    lg_f = jnp.log1p(-jnp.exp2(-5.0 - h))[:, None, None]
    lg_b = jnp.log1p(-jnp.exp2(-5.5 - h))[:, None, None]
    c = RET_CHUNK
    pos = jnp.arange(c, dtype=F32)
    diff = (pos[:, None] - pos[None, :])[None]
    dmat = jnp.where(diff >= 0, jnp.exp(lg_f * jnp.where(diff >= 0, diff, 0.0)),
                     jnp.exp(lg_b * jnp.where(diff < 0, -diff, 0.0)))
    col = lambda v: jnp.broadcast_to(v, (RET_HEADS, c, d))
    p = pos[None, :, None]
    dec = jnp.stack([
        dmat,
        col(jnp.exp(lg_f * (p + 1.0))),
        col(jnp.exp(lg_f * (c - 1.0 - p))),
        col(jnp.exp(lg_b * (c - p))),
        col(jnp.exp(lg_b * p)),
        col(jnp.exp(lg_f * c)),
        col(jnp.exp(lg_b * c)),
    ], axis=1)
    return dec


def retention(proj, ret_norm, b, s_len, *, heads):
    dec = _retention_tables()
    hd = HEAD_DIM
    w = heads * hd
    ng = RET_HEADS // heads
    head_blk = lambda off: pl.BlockSpec((None, s_len, w), lambda i, h: (i, 0, off + h))
    return pl.pallas_call(
        _retention_body,
        out_shape=jax.ShapeDtypeStruct((b, s_len, RET_HEADS * hd), BF16),
        grid=(b, ng),
        in_specs=[head_blk(0), head_blk(ng), head_blk(2 * ng), head_blk(3 * ng),
                  pl.BlockSpec((heads, 7, RET_CHUNK, hd), lambda i, h: (h, 0, 0, 0)),
                  pl.BlockSpec((1, w), lambda i, h: (0, h))],
        out_specs=pl.BlockSpec((None, s_len, w), lambda i, h: (i, 0, h)),
        scratch_shapes=[pltpu.VMEM((s_len, w), F32)] * 2,
        compiler_params=_params(2),
        name="retention",
    )(proj, proj, proj, proj, dec, ret_norm.reshape(1, -1))


def _gdn_gates_body(raw_ref, al_ref, dt_ref, o_ref):
    s_len = raw_ref.shape[-1]
    nh = 2 * GDN_HEADS
    gb = raw_ref[0:nh, :]
    ga = raw_ref[nh:2 * nh, :]
    beta = jax.nn.sigmoid(gb)
    g = -jnp.exp(al_ref[...]) * jax.nn.softplus(ga + dt_ref[...])
    pos = lax.broadcasted_iota(jnp.int32, (nh, s_len), 1) & (GDN_CHUNK - 1)
    fwd = g
    rev = g
    k = 1
    while k < GDN_CHUNK:
        fwd = fwd + jnp.where(pos >= k, pltpu.roll(fwd, k, 1), 0.0)
        rev = rev + jnp.where(pos < GDN_CHUNK - k, pltpu.roll(rev, s_len - k, 1), 0.0)
        k *= 2
    total = fwd + rev - g
    for h in range(GDN_HEADS):
        for d, cum in enumerate((fwd, rev)):
            r = d * GDN_HEADS + h
            o_ref[h, 3 * d:3 * d + 1, :] = cum[r:r + 1, :]
            o_ref[h, 3 * d + 1:3 * d + 2, :] = beta[r:r + 1, :]
            o_ref[h, 3 * d + 2:3 * d + 3, :] = total[r:r + 1, :]
        o_ref[h, 6:8, :] = jnp.zeros((2, s_len), F32)


def gdn_gates(raw, a_log, dt_bias, b, s_len):
    nh = 2 * GDN_HEADS
    return pl.pallas_call(
        _gdn_gates_body,
        out_shape=jax.ShapeDtypeStruct((b, GDN_HEADS, 8, s_len), F32),
        grid=(b,),
        in_specs=[pl.BlockSpec((2 * nh, s_len), lambda i: (0, i)),
                  pl.BlockSpec((nh, 1), lambda i: (0, 0)),
                  pl.BlockSpec((nh, 1), lambda i: (0, 0))],
        out_specs=pl.BlockSpec((None, GDN_HEADS, 8, s_len), lambda i: (i, 0, 0, 0)),
        compiler_params=_params(1),
        name="gdn_gates",
    )(raw, a_log.reshape(nh, 1), dt_bias.reshape(nh, 1))


def _lockstep(gens):
    results = [None] * len(gens)
    live = list(range(len(gens)))
    while live:
        for i in list(live):
            try:
                next(gens[i])
            except StopIteration as done:
                results[i] = done.value
                live.remove(i)
    return results


def _merge_run(a, b):
    ia = ib = 0
    while ia < len(a) or ib < len(b):
        if ib >= len(b) or (ia < len(a) and (ia + 1) * len(b) <= (ib + 1) * len(a)):
            a[ia]()
            ia += 1
        else:
            b[ib]()
            ib += 1


def _tri_inverse(m, ri, ci):
    def same_block(log2_size):
        return (ri >> log2_size) == (ci >> log2_size)

    base = GDN_INV_BASE_LOG2
    eye = (ri == ci).astype(F32)
    diag = jnp.where(same_block(base), -m, 0.0)
    nm = _split(diag)
    p = eye + diag
    for _ in range(base - 1):
        nm = _split(_dot3(nm, nm))
        yield
        p = p + _dot3(_split(p), nm)
        yield
    for size in range(base, GDN_CHUNK.bit_length() - 1):
        off = jnp.where(same_block(size + 1) & jnp.logical_not(same_block(size)), m, 0.0)
        p16 = p.astype(BF16)
        pc = _dot(p16, off.astype(BF16)).astype(BF16)
        yield
        p = p - _dot(pc, p16)
        yield
    return p


def _gdn_super_chunk(q, k, v, gcol, grow, bcol, glcol, st, rev):
    r = GDN_SUPER
    c = GDN_CHUNK
    ri = lax.broadcasted_iota(jnp.int32, (r, r), 0)
    ci = lax.broadcasted_iota(jnp.int32, (r, r), 1)
    same = (ri >> 6) == (ci >> 6)
    if rev:
        incl = same & (ri <= ci)
        strict = same & (ri < ci)
    else:
        incl = same & (ri >= ci)
        strict = same & (ri > ci)
    decay = jnp.exp(jnp.where(incl, gcol - grow, -jnp.inf))
    kb = k * bcol
    vb = v * bcol
    k16 = k.astype(BF16)
    m = jnp.where(strict, _dot_nt(kb.astype(BF16), k16) * decay, 0.0)
    qk = jnp.where(incl, _dot_nt(q.astype(BF16), k16) * decay, 0.0).astype(BF16)
    yield
    t = yield from _tri_inverse(m, ri, ci)
    eg = jnp.exp(gcol)
    rhs = jnp.concatenate([vb, kb * eg], axis=1).astype(BF16)
    uw = _dot(t.astype(BF16), rhs)
    u = uw[:, :HEAD_DIM]
    w = uw[:, HEAD_DIM:].astype(BF16)
    qd = (q * eg).astype(BF16)
    kt = (k * jnp.exp(glcol - gcol)).astype(BF16)
    yield
    outs = [None] * (r // c)
    order = range(r // c - 1, -1, -1) if rev else range(r // c)
    for i in order:
        sl = slice(i * c, (i + 1) * c)
        s16 = st.astype(BF16)
        vn = u[sl] - _dot(w[sl], s16)
        vn16 = vn.astype(BF16)
        yield
        pair = slice((i // 2) * 2 * c, (i // 2 + 1) * 2 * c)
        zero = jnp.zeros_like(vn16)
        vpad = jnp.concatenate([zero, vn16] if i % 2 else [vn16, zero], axis=0)
        outs[i] = _dot(qd[sl], s16) + _dot(qk[sl, pair], vpad)
        st = st * jnp.exp(glcol[i * c:i * c + 1, :]) + _dot_tn(kt[sl], vn16)
        yield
    return jnp.concatenate(outs, axis=0), st


def _gdn_core_body(q_ref, k_ref, v_ref, z_ref, row_ref, gn_ref, o_ref, col_s, of_s, ob_s):
    s_len = q_ref.shape[0]
    hd = HEAD_DIM
    heads = q_ref.shape[1] // hd
    r = GDN_SUPER
    n = s_len // r

    for hh in range(heads):
        rows8 = jnp.concatenate([row_ref[hh, i] for i in range(n)], axis=1)
        padded = jnp.concatenate([rows8, jnp.zeros((LANES - 8, s_len), F32)], axis=0)
        col_s[hh] = padded.T

    def one(hh, i, st, rev):
        base = 3 if rev else 0
        rows = pl.ds(pl.multiple_of(i * r, r), r)
        lanes = slice(hh * hd, (hh + 1) * hd)
        cols = col_s[hh, rows, :]
        gcol = cols[:, base:base + 1]
        bcol = cols[:, base + 1:base + 2]
        glcol = cols[:, base + 2:base + 3]
        grow = row_ref[hh, i][base:base + 1, :]
        return _gdn_super_chunk(q_ref[rows, lanes], k_ref[rows, lanes], v_ref[rows, lanes],
                                gcol, grow, bcol, glcol, st, rev)

    def step(t, carry):
        fwd_rows = pl.ds(pl.multiple_of(t * r, r), r)
        bwd_rows = pl.ds(pl.multiple_of((n - 1 - t) * r, r), r)
        gens = []
        for hh in range(heads):
            gens.append(one(hh, t, carry[2 * hh], False))
            gens.append(one(hh, n - 1 - t, carry[2 * hh + 1], True))
        res = _lockstep(gens)
        for hh in range(heads):
            lanes = slice(hh * hd, (hh + 1) * hd)
            of_s[fwd_rows, lanes] = res[2 * hh][0]
            ob_s[bwd_rows, lanes] = res[2 * hh + 1][0]
        return tuple(st for _, st in res)

    zero = jnp.zeros((hd, hd), F32)
    lax.fori_loop(0, n, step, (zero,) * (2 * heads))
    for hh in range(heads):
        lanes = slice(hh * hd, (hh + 1) * hd)
        o = of_s[:, lanes] + ob_s[:, lanes]
        y = o * lax.rsqrt(jnp.mean(o * o, axis=-1, keepdims=True) + EPS) * gn_ref[...]
        o_ref[:, lanes] = (y * jax.nn.silu(z_ref[:, lanes])).astype(o_ref.dtype)


def gdn_core(proj, gates, gdn_norm, b, s_len, *, heads):
    hd = HEAD_DIM
    n_super = s_len // GDN_SUPER
    gate_row = gates.reshape(b, GDN_HEADS, 8, n_super, GDN_SUPER).transpose(0, 1, 3, 2, 4)
    w = heads * hd
    ng = GDN_HEADS // heads
    first = 4 * RET_HEADS // heads
    qkv_blk = lambda off: pl.BlockSpec((None, s_len, w), lambda i, h: (i, 0, first + off + h))
    return pl.pallas_call(
        _gdn_core_body,
        out_shape=jax.ShapeDtypeStruct((b, s_len, GDN_HEADS * hd), BF16),
        grid=(b, ng),
        in_specs=[qkv_blk(0), qkv_blk(ng), qkv_blk(2 * ng),
                  pl.BlockSpec((None, s_len, w), lambda i, h: (i, 0, first + 3 * ng + h),
                               pipeline_mode=pl.Buffered(1)),
                  pl.BlockSpec((None, heads, n_super, 8, GDN_SUPER), lambda i, h: (i, h, 0, 0, 0)),
                  pl.BlockSpec((1, hd), lambda i, h: (0, 0))],
        out_specs=pl.BlockSpec((None, s_len, w), lambda i, h: (i, 0, h)),
        scratch_shapes=[pltpu.VMEM((heads, s_len, LANES), F32),
                        pltpu.VMEM((s_len, w), F32), pltpu.VMEM((s_len, w), F32)],
        compiler_params=_params(2),
        name="gdn_core",
    )(proj, proj, proj, proj, gate_row, gdn_norm.reshape(1, hd))


def _mixer_xattn_body(x_ref, a1_ref, a2_ref, w1_ref, w2_ref, gxa_ref, wq_ref, kv_ref, wo_ref,
                      gffn_ref, wpq_ref, o_ref, pq_ref):
    x = x_ref[...] + _dot(a1_ref[...], w1_ref[...]) + _dot(a2_ref[...], w2_ref[...])
    d = x.shape[1]
    dh = d // XA_HEADS
    q = _dot(_rms(x, gxa_ref[...]).astype(BF16), wq_ref[...])
    outs = []
    for h in range(XA_HEADS):
        kh = kv_ref[:, h * dh:(h + 1) * dh].astype(BF16)
        vh = kv_ref[:, d + h * dh:d + (h + 1) * dh].astype(BF16)
        sc = _dot_nt(q[:, h * dh:(h + 1) * dh].astype(BF16), kh) * (dh ** -0.5)
        p = jax.nn.softmax(sc, axis=-1)
        outs.append(_dot(p.astype(BF16), vh).astype(BF16))
    x = x + _dot(jnp.concatenate(outs, axis=1), wo_ref[...])
    o_ref[...] = x
    pq_ref[...] = _dot(_rms(x, gffn_ref[...]).astype(BF16), wpq_ref[...])


def mixer_xattn_peerq(x, a1, a2, w1, w2, g_xa, wq, kv, wo, g_ffn, wpq, b, s_len, *, ts):
    t, d = x.shape
    k1, k2 = a1.shape[1], a2.shape[1]
    m = kv.shape[1]
    npq = wpq.shape[1]
    nblk = s_len // ts
    tok = lambda n: pl.BlockSpec((ts, n), lambda i, j: (i * nblk + j, 0))
    whole = lambda r, c: pl.BlockSpec((r, c), lambda i, j: (0, 0))
    return pl.pallas_call(
        _mixer_xattn_body,
        out_shape=(jax.ShapeDtypeStruct((t, d), F32), jax.ShapeDtypeStruct((t, npq), F32)),
        grid=(b, nblk),
        in_specs=[tok(d), tok(k1), tok(k2), whole(k1, d), whole(k2, d),
                  whole(1, d), whole(d, d),
                  pl.BlockSpec((None, m, 2 * d), lambda i, j: (i, 0, 0)),
                  whole(d, d), whole(1, d), whole(d, npq)],
        out_specs=(tok(d), tok(npq)),
        compiler_params=_params(2),
        name="mixer_xattn_peerq",
    )(x, a1, a2, w1, w2, g_xa.reshape(1, d), wq, kv, wo, g_ffn.reshape(1, d), wpq)


def _extract_topk(s, k, break_ties):
    vals = []
    if break_ties:
        n_rows = s.shape[0]
        iota = lax.broadcasted_iota(jnp.int32, s.shape, 0).astype(F32)
        rank = jnp.full(s.shape, float(k), F32)
        for r in range(k):
            m = jnp.max(s, axis=0, keepdims=True)
            idx = jnp.min(jnp.where(s == m, iota, float(n_rows)), axis=0, keepdims=True)
            hit = iota == idx
            rank = jnp.where(hit, float(r), rank)
            s = jnp.where(hit, -jnp.inf, s)
            vals.append(m)
        taken = rank < float(k)
    else:
        for r in range(k):
            m = jnp.max(s, axis=0, keepdims=True)
            s = jnp.where(s == m, -(TAKEN + r * TAKEN_STEP), s)
            vals.append(m)
        taken = s <= -TAKEN
        rank = jnp.where(taken, s * (-1.0 / TAKEN_STEP) - TAKEN / TAKEN_STEP, float(k))
    count = jnp.sum(jnp.where(taken, 1.0, 0.0), axis=0, keepdims=True)
    return rank, vals, count


def _route_tokens(q, keys_ref, break_ties):
    nk = PEER_NKEYS
    kk = PEER_TOPK
    s1 = _dot_nt(keys_ref[0], q[:, :nk])
    s2 = _dot_nt(keys_ref[1], q[:, nk:])
    r1, v1, n_sel1 = _extract_topk(s1, kk, break_ties)
    r2, v2, n_sel2 = _extract_topk(s2, kk, break_ties)
    v1m = jnp.concatenate(v1, axis=0)
    v2m = jnp.concatenate(v2, axis=0)
    pad = -jnp.inf if break_ties else FAST_PAD
    row8 = lax.broadcasted_iota(jnp.int32, (8, LANES), 0)
    groups = [v1[0] + v2m[0:8], v1[0] + v2m[8:16]]
    for a in range(1, 8):
        groups.append(jnp.where(row8 < kk // (a + 1), v1[a] + v2m[0:8], pad))
    groups.append(v1m[8:16] + v2[0])
    cand = jnp.concatenate(groups, axis=0)
    rc, _, n_selc = _extract_topk(cand, kk, break_ties)
    sel = rc < float(kk)
    top = v1[0] + v2[0]
    z = jnp.sum(jnp.where(sel, jnp.exp(cand - top), 0.0), axis=0, keepdims=True)
    sel_f = jnp.where(sel, 1.0, 0.0)
    n1 = jnp.zeros((nk, LANES), F32)
    for a in range(kk):
        if a == 0:
            cnt = jnp.sum(sel_f[0:16], axis=0, keepdims=True)
        elif a < 8:
            cnt = jnp.sum(sel_f[8 + 8 * a:16 + 8 * a], axis=0, keepdims=True)
        else:
            cnt = sel_f[64 + a:65 + a]
        n1 = jnp.where(r1 == float(a), cnt, n1)
    c1 = jnp.where(r1 < float(kk), jnp.exp(s1 - v1[0]) / z, 0.0)
    e2 = jnp.exp(s2 - v2[0])
    exact = ((n_sel1 == float(kk)) & (n_sel2 == float(kk)) & (n_selc == float(kk))
             & (v1[kk - 1] > -FAST_SCORE_BOUND) & (v2[kk - 1] > -FAST_SCORE_BOUND))
    return c1, n1, e2, r2, jnp.where(exact, 1.0, 0.0)


def _peer_route_body(q_ref, keys_ref, c1_ref, n1_ref, e2_ref, r2_ref):
    tb = q_ref.shape[0]

    def run(break_ties):
        flags = []
        for blk in range(tb // LANES):
            tok = slice(blk * LANES, (blk + 1) * LANES)
            c1, n1, e2, r2, ok = _route_tokens(q_ref[tok, :].astype(BF16), keys_ref, break_ties)
            c1_ref[:, tok] = c1
            n1_ref[:, tok] = n1
            e2_ref[:, tok] = e2.astype(e2_ref.dtype)
            r2_ref[:, tok] = r2.astype(r2_ref.dtype)
            flags.append(ok)
        return jnp.min(jnp.concatenate(flags, axis=1))

    all_exact = run(False)

    @pl.when(all_exact < 0.5)
    def _():
        run(True)


def peer_route(q, keys, *, tb):
    t = q.shape[0]
    nk = PEER_NKEYS
    out = lambda dt: jax.ShapeDtypeStruct((PEER_HEADS, nk, t), dt)
    ospec = pl.BlockSpec((None, nk, tb), lambda i, h: (h, 0, i))
    return pl.pallas_call(
        _peer_route_body,
        out_shape=(out(F32), out(F32), out(BF16), out(BF16)),
        grid=(t // tb, PEER_HEADS),
        in_specs=[pl.BlockSpec((tb, 2 * nk), lambda i, h: (i, h)),
                  pl.BlockSpec((None, 2, nk, nk), lambda i, h: (h, 0, 0, 0))],
        out_specs=(ospec,) * 4,
        compiler_params=_params(2),
        name="peer_route",
    )(q, keys)


def _gelu_tanh(x):
    c0 = 0.7978845608028654
    c1 = 0.7978845608028654 * 0.044715
    return (0.5 * x) * (1.0 + jnp.tanh(x * (c0 + c1 * (x * x))))


def _peer_dense_body(x_ref, g_ref, c1_ref, n1_ref, e2_ref, r2_ref, u_ref, vt_ref, *rest, final):
    if final:
        gf_ref, o_ref, xn_s, acc_s = rest
    else:
        o_ref, xn_s, acc_s = rest
    j = pl.program_id(1)
    nk = PEER_NKEYS
    tb = x_ref.shape[0]
    grp = u_ref.shape[0] // nk
    sub = e2_ref.shape[2]

    @pl.when(j == 0)
    def _():
        xn_s[...] = _rms(x_ref[...], g_ref[...]).T.astype(BF16)
        acc_s[...] = jnp.zeros_like(acc_s)

    per = MXU_DEPTH // nk
    n_chunks = grp // per
    d_model = vt_ref.shape[0]
    xn = xn_s[...]
    scores, gates, weights, chunk_w = {}, {}, {}, {}

    def score_pieces(c):
        def piece(ii):
            scores[ii] = _dot(u_ref[ii * nk:(ii + 1) * nk, :], xn)
        return [functools.partial(piece, c * per + i) for i in range(per)]

    def gate_pieces(c):
        def piece(ii, h):
            row = pl.ds(j * grp + ii, 1)
            c1 = jnp.broadcast_to(c1_ref[h, row, :], (sub, tb)).astype(e2_ref.dtype)[None]
            n1 = jnp.broadcast_to(n1_ref[h, row, :], (sub, tb)).astype(e2_ref.dtype)[None]
            e2 = e2_ref[h]
            term = jnp.where(r2_ref[h] < n1, e2, jnp.zeros_like(e2)) * c1
            gates[ii] = term if h == 0 else gates[ii] + term
        return [functools.partial(piece, c * per + i, h) for i in range(per) for h in range(PEER_HEADS)]

    def act_pieces(c):
        def piece(ii):
            act = _gelu_tanh(scores.pop(ii).astype(e2_ref.dtype))
            weights[ii] = act * gates.pop(ii).reshape(nk, tb)
        return [functools.partial(piece, c * per + i) for i in range(per)]

    def value_pieces(c):
        ex = slice(c * MXU_DEPTH, (c + 1) * MXU_DEPTH)

        def piece(m):
            if m == 0:
                chunk_w[c] = jnp.concatenate([weights.pop(c * per + i) for i in range(per)], axis=0)
            rows = slice(m * MXU_DEPTH, (m + 1) * MXU_DEPTH)
            acc_s[rows, :] += _dot(vt_ref[rows, ex], chunk_w[c])
        return [functools.partial(piece, m) for m in range(d_model // MXU_DEPTH)]

    _merge_run(score_pieces(0), gate_pieces(0))
    for c in range(n_chunks):
        mxu = (score_pieces(c + 1) if c + 1 < n_chunks else []) + (value_pieces(c - 1) if c else [])
        vpu = act_pieces(c) + (gate_pieces(c + 1) if c + 1 < n_chunks else [])
        _merge_run(mxu, vpu)
    _merge_run(value_pieces(n_chunks - 1), [])

    @pl.when(j == pl.num_programs(1) - 1)
    def _():
        y = x_ref[...] + acc_s[...].T
        if final:
            y = _rms(y, gf_ref[...])
        o_ref[...] = y


def peer_dense_residual(x, g, route, u, layer, vt, g_final, *, tb, grp, blocks=None):
    t, d = x.shape
    nk = PEER_NKEYS
    eb = grp * nk
    first, count = blocks if blocks is not None else (0, t // tb)
    c1, n1, e2, r2 = route
    sub = 32 // jnp.dtype(e2.dtype).itemsize
    e2 = e2.reshape(PEER_HEADS, nk // sub, sub, t)
    r2 = r2.reshape(PEER_HEADS, nk // sub, sub, t)
    rspec = pl.BlockSpec((PEER_HEADS, nk, tb), lambda i, j: (0, 0, first + i))
    pspec = pl.BlockSpec((PEER_HEADS, nk // sub, sub, tb), lambda i, j: (0, 0, 0, first + i))
    in_specs = [pl.BlockSpec((tb, d), lambda i, j: (first + i, 0)),
                pl.BlockSpec((1, d), lambda i, j: (0, 0)),
                rspec, rspec, pspec, pspec,
                pl.BlockSpec((None, eb, d), lambda i, j: (layer, j, 0)),
                pl.BlockSpec((d, eb), lambda i, j: (0, j))]
    args = [x, g.reshape(1, d), c1, n1, e2, r2, u, vt]
    final = g_final is not None
    if final:
        in_specs.append(pl.BlockSpec((1, d), lambda i, j: (0, 0)))
        args.append(g_final.reshape(1, d))
    return pl.pallas_call(
        functools.partial(_peer_dense_body, final=final),
        out_shape=jax.ShapeDtypeStruct((count * tb, d), F32),
        grid=(count, (nk * nk) // eb),
        in_specs=in_specs,
        out_specs=pl.BlockSpec((tb, d), lambda i, j: (i, 0)),
        scratch_shapes=[pltpu.VMEM((d, tb), BF16),
                        pltpu.VMEM((d, tb), F32)],
        compiler_params=_params(2),
        name="peer_dense_final" if final else "peer_dense",
    )(*args)


def _transpose_cast_body(x_ref, o_ref):
    o_ref[...] = x_ref[...].T.astype(o_ref.dtype)


def transpose_cast(tables, layer):
    _, rows, cols = tables.shape
    tr = _pick(rows, (1024, 512, 256, 128))
    return pl.pallas_call(
        _transpose_cast_body,
        out_shape=jax.ShapeDtypeStruct((cols, rows), BF16),
        grid=(rows // tr,),
        in_specs=[pl.BlockSpec((None, tr, cols), lambda i: (layer, i, 0))],
        out_specs=pl.BlockSpec((cols, tr), lambda i: (0, i)),
        compiler_params=_params(1),
        name="transpose_cast",
    )(tables)


def _pick(n, prefs):
    for p in prefs:
        if n % p == 0:
            return p
    return n


def _encoder(group_batches, x3, mem3, norm_mix, w_in, ret_norm, gdn_conv, gdn_a_log, gdn_dt_bias,
             gdn_norm, w_out, norm_xa, norm_mem, w_xq, w_xkv, w_xo, norm_ffn, peer_wq, peer_keys,
             peer_u, peer_v, norm_final):
    b, s_len, d = x3.shape
    n_mem = mem3.shape[1]
    t = b * s_len
    depth = w_in.shape[0]
    x = x3.reshape(t, d)
    mem = mem3.reshape(b * n_mem, d)
    n_main = 4 * RET_HEADS * HEAD_DIM + 4 * GDN_HEADS * HEAD_DIM
    tmm = _pick(b * n_mem, (1024, 512, 256, 128))
    ts = _pick(s_len, (512, 256, 128))
    tb_route = _pick(t, (256, 128))
    tb_dense = _pick(s_len, (512, 256, 128))
    peer_u16 = peer_u.astype(BF16)

    for l in range(depth):
        w_main = _deinterleave_rotary_columns(w_in[l, :, :n_main]).astype(BF16)
        w_gate_t = w_in[l, :, n_main:].T.astype(BF16)
        proj, graw = mixer_in_proj(x, norm_mix[l], w_main, w_gate_t, gdn_conv[l], s_len)
        proj3 = proj.reshape(b, s_len, n_main)
        o_r = retention(proj3, ret_norm[l], b, s_len, heads=RET_HEADS_PER_STEP)
        gates = gdn_gates(graw, gdn_a_log[l], gdn_dt_bias[l], b, s_len)
        o_g = gdn_core(proj3, gates, gdn_norm[l], b, s_len, heads=GDN_HEADS_PER_STEP)
        w_o = w_out[l].astype(BF16)
        n_r = RET_HEADS * HEAD_DIM

        kv = norm_matmul(mem, norm_mem[l], w_xkv[l].astype(BF16), tm=tmm, tn=1024)
        x, pq = mixer_xattn_peerq(x, o_r.reshape(t, -1), o_g.reshape(t, -1), w_o[:n_r], w_o[n_r:],
                                  norm_xa[l], w_xq[l].astype(BF16), kv.reshape(b, n_mem, 2 * d),
                                  w_xo[l].astype(BF16), norm_ffn[l], peer_wq[l].astype(BF16),
                                  b, s_len, ts=ts)

        route = peer_route(pq, peer_keys[l].astype(BF16), tb=tb_route)
        dense = functools.partial(peer_dense_residual, x, norm_ffn[l], route, peer_u16, l,
                                  transpose_cast(peer_v, l), tb=tb_dense, grp=16)
        if l < depth - 1:
            x = dense(None)
    outs, first = [], 0
    for nb in group_batches:
        n_blocks = nb * s_len // tb_dense
        outs.append(dense(norm_final, blocks=(first, n_blocks)).reshape(nb, s_len, d))
        first += n_blocks
    return tuple(outs)


def kernel(x_prompt, x_sample, mem_prompt, mem_sample, norm_mix, w_in, ret_norm, gdn_conv, gdn_a_log,
           gdn_dt_bias, gdn_norm, w_out, norm_xa, norm_mem, w_xq, w_xkv, w_xo, norm_ffn, peer_wq,
           peer_keys, peer_u, peer_v, norm_final):
    weights = (norm_mix, w_in, ret_norm, gdn_conv, gdn_a_log, gdn_dt_bias, gdn_norm, w_out,
               norm_xa, norm_mem, w_xq, w_xkv, w_xo, norm_ffn, peer_wq, peer_keys, peer_u, peer_v,
               norm_final)
    if x_prompt.shape[1:] == x_sample.shape[1:] and mem_prompt.shape[1:] == mem_sample.shape[1:]:
        return _encoder((x_prompt.shape[0], x_sample.shape[0]),
                        jnp.concatenate([x_prompt, x_sample], axis=0),
                        jnp.concatenate([mem_prompt, mem_sample], axis=0), *weights)
    return (_encoder((x_prompt.shape[0],), x_prompt, mem_prompt, *weights)[0],
            _encoder((x_sample.shape[0],), x_sample, mem_sample, *weights)[0])
```

```python
import functools

import jax
import jax.numpy as jnp
from jax import lax
from jax.experimental import pallas as pl
from jax.experimental.pallas import tpu as pltpu

F32 = jnp.float32
BF16 = jnp.bfloat16

EPS = 1e-6
LANES = 128
MXU_DEPTH = 256
HEAD_DIM = 128
RET_HEADS = 4
GDN_HEADS = 4
RET_CHUNK = 128
GDN_CHUNK = 64
GDN_SUPER = 256
GDN_INV_BASE_LOG2 = 1
GDN_HEADS_PER_STEP = 4
RET_HEADS_PER_STEP = 4
ROPE_BASE = 10000.0
XA_HEADS = 4
PEER_HEADS = 8
PEER_NKEYS = 128
PEER_TOPK = 16
TAKEN = 2.0 ** 100
TAKEN_STEP = 2.0 ** 96
FAST_PAD = -(2.0 ** 90)
FAST_SCORE_BOUND = 2.0 ** 80
VMEM_LIMIT = 56 * 1024 * 1024

_NT = (((1,), (1,)), ((), ()))
_TN = (((0,), (0,)), ((), ()))


def _params(n_axes):
    return pltpu.CompilerParams(
        dimension_semantics=("arbitrary",) * n_axes, vmem_limit_bytes=VMEM_LIMIT)


def _dot(a, b):
    return jnp.dot(a, b, preferred_element_type=F32)


def _dot_nt(a, b):
    return lax.dot_general(a, b, _NT, preferred_element_type=F32)


def _dot_tn(a, b):
    return lax.dot_general(a, b, _TN, preferred_element_type=F32)


def _split(a):
    hi = a.astype(BF16)
    return hi, (a - hi.astype(F32)).astype(BF16)


def _dot3(a, b):
    (ah, al), (bh, bl) = a, b
    return _dot(jnp.concatenate([ah, ah, al], axis=1), jnp.concatenate([bh, bl, bh], axis=0))


def _rms(x, g):
    return x * lax.rsqrt(jnp.mean(x * x, axis=-1, keepdims=True) + EPS) * g


def _norm_matmul_body(x_ref, g_ref, w_ref, o_ref, xn_ref):
    @pl.when(pl.program_id(1) == 0)
    def _():
        xn_ref[...] = _rms(x_ref[...], g_ref[...]).astype(BF16)

    o_ref[...] = _dot(xn_ref[...], w_ref[...]).astype(o_ref.dtype)


def norm_matmul(x, g, w, *, tm, tn, out_dtype=F32):
    t, d = x.shape
    n = w.shape[1]
    return pl.pallas_call(
        _norm_matmul_body,
        out_shape=jax.ShapeDtypeStruct((t, n), out_dtype),
        grid=(t // tm, n // tn),
        in_specs=[pl.BlockSpec((tm, d), lambda i, j: (i, 0)),
                  pl.BlockSpec((1, d), lambda i, j: (0, 0)),
                  pl.BlockSpec((d, tn), lambda i, j: (0, j))],
        out_specs=pl.BlockSpec((tm, tn), lambda i, j: (i, j)),
        scratch_shapes=[pltpu.VMEM((tm, d), BF16)],
        compiler_params=_params(2),
        name="norm_matmul",
    )(x, g.reshape(1, d), w)


def _mixer_in_proj_body(x_ref, g_ref, w_ref, ws_ref, cos_ref, sin_ref, cw_ref, o_ref, os_ref, xn_ref):
    j = pl.program_id(1)
    hd = HEAD_DIM
    s_len = x_ref.shape[0]

    @pl.when(j == 0)
    def _():
        xn = _rms(x_ref[...], g_ref[...]).astype(BF16)
        xn_ref[...] = xn
        os_ref[...] = _dot_nt(ws_ref[...], xn)

    o_ref[...] = _dot(xn_ref[...], w_ref[...])

    @pl.when(j == 0)
    def _():
        cos = cos_ref[...]
        sin = sin_ref[...]
        for head in range(2 * RET_HEADS):
            cols = slice(head * hd, (head + 1) * hd)
            x = o_ref[:, cols]
            y = x * cos + pltpu.roll(x, hd // 2, 1) * sin
            o_ref[:, cols] = y * (hd ** -0.5) if head >= RET_HEADS else y

    def conv_silu(cols):
        x = o_ref[:, cols]
        t = lax.broadcasted_iota(jnp.int32, x.shape, 0)
        prev = jnp.where(t == 0, 0.0, pltpu.roll(x, 1, 0))
        nxt = jnp.where(t == s_len - 1, 0.0, pltpu.roll(x, s_len - 1, 0))
        return jax.nn.silu(prev * cw_ref[0:1, cols] + x * cw_ref[1:2, cols] + nxt * cw_ref[2:3, cols])

    @pl.when(j == 2)
    def _():
        for head in range(2 * GDN_HEADS):
            cols = slice(head * hd, (head + 1) * hd)
            act = conv_silu(cols)
            inv = lax.rsqrt(jnp.sum(act * act, axis=-1, keepdims=True) + EPS)
            o_ref[:, cols] = act * (inv * (hd ** -0.5) if head < GDN_HEADS else inv)

    @pl.when(j == 3)
    def _():
        for head in range(GDN_HEADS):
            cols = slice(head * hd, (head + 1) * hd)
            o_ref[:, cols] = conv_silu(cols)


def mixer_in_proj(x, g, w, ws_t, conv_w, s_len):
    t, d = x.shape
    n = w.shape[1]
    rows = ws_t.shape[0]
    tn = 2 * RET_HEADS * HEAD_DIM
    assert tn == 2 * GDN_HEADS * HEAD_DIM and n == 4 * tn
    cos, sin = _rotary_tables(s_len)
    conv_w = jnp.pad(conv_w, ((0, 0), (0, 2 * tn - conv_w.shape[1])))
    return pl.pallas_call(
        _mixer_in_proj_body,
        out_shape=(jax.ShapeDtypeStruct((t, n), F32),
                   jax.ShapeDtypeStruct((rows, t), F32)),
        grid=(t // s_len, n // tn),
        in_specs=[pl.BlockSpec((s_len, d), lambda i, j: (i, 0)),
                  pl.BlockSpec((1, d), lambda i, j: (0, 0)),
                  pl.BlockSpec((d, tn), lambda i, j: (0, j)),
                  pl.BlockSpec((rows, d), lambda i, j: (0, 0)),
                  pl.BlockSpec((s_len, HEAD_DIM), lambda i, j: (0, 0)),
                  pl.BlockSpec((s_len, HEAD_DIM), lambda i, j: (0, 0)),
                  pl.BlockSpec((3, tn), lambda i, j: (0, jnp.maximum(j - 2, 0)))],
        out_specs=(pl.BlockSpec((s_len, tn), lambda i, j: (i, j)),
                   pl.BlockSpec((rows, s_len), lambda i, j: (0, i))),
        scratch_shapes=[pltpu.VMEM((s_len, d), BF16)],
        compiler_params=_params(2),
        name="mixer_in_proj",
    )(x, g.reshape(1, d), w, ws_t, cos, sin, conv_w)


def _retention_body(qr_s, kr_s, v_ref, g_ref, dec_ref, rn_ref, o_ref, of_s, ob_s):
    s_len = qr_s.shape[0]
    hd = HEAD_DIM
    heads = qr_s.shape[1] // hd
    c = RET_CHUNK
    n = s_len // c

    def forward(hh, i, st):
        r = pl.ds(pl.multiple_of(i * c, c), c)
        lanes = slice(hh * hd, (hh + 1) * hd)
        qc, kc, vb = qr_s[r, lanes], kr_s[r, lanes], v_ref[r, lanes].astype(BF16)
        sc = _dot_nt(qc.astype(BF16), kc.astype(BF16)) * dec_ref[hh, 0]
        inter = _dot((qc * dec_ref[hh, 1]).astype(BF16), st.astype(BF16))
        st = st * dec_ref[hh, 5] + _dot_tn((kc * dec_ref[hh, 2]).astype(BF16), vb)
        yield
        return inter + _dot(sc.astype(BF16), vb), st

    def backward(hh, i, st):
        r = pl.ds(pl.multiple_of(i * c, c), c)
        lanes = slice(hh * hd, (hh + 1) * hd)
        qc, kc, vb = qr_s[r, lanes], kr_s[r, lanes], v_ref[r, lanes].astype(BF16)
        inter = _dot((qc * dec_ref[hh, 3]).astype(BF16), st.astype(BF16))
        st = st * dec_ref[hh, 6] + _dot_tn((kc * dec_ref[hh, 4]).astype(BF16), vb)
        yield
        return inter, st

    def step(t, carry):
        fwd_rows = pl.ds(pl.multiple_of(t * c, c), c)
        bwd_rows = pl.ds(pl.multiple_of((n - 1 - t) * c, c), c)
        gens = []
        for hh in range(heads):
            gens.append(forward(hh, t, carry[2 * hh]))
            gens.append(backward(hh, n - 1 - t, carry[2 * hh + 1]))
        res = _lockstep(gens)
        for hh in range(heads):
            lanes = slice(hh * hd, (hh + 1) * hd)
            of_s[fwd_rows, lanes] = res[2 * hh][0]
            ob_s[bwd_rows, lanes] = res[2 * hh + 1][0]
        return tuple(st for _, st in res)

    zero = jnp.zeros((hd, hd), F32)
    lax.fori_loop(0, n, step, (zero,) * (2 * heads))

    for hh in range(heads):
        lanes = slice(hh * hd, (hh + 1) * hd)
        o = of_s[:, lanes] + ob_s[:, lanes]
        mu = jnp.mean(o, axis=-1, keepdims=True)
        var = jnp.mean(jnp.square(o - mu), axis=-1, keepdims=True)
        y = (o - mu) * lax.rsqrt(var + EPS)
        y = y * rn_ref[:, lanes] * jax.nn.silu(g_ref[:, lanes])
        o_ref[:, lanes] = y.astype(o_ref.dtype)


def _deinterleave_rotary_columns(w):
    n_qk = 2 * RET_HEADS * HEAD_DIM
    qk = w[:, :n_qk].reshape(w.shape[0], 2 * RET_HEADS, HEAD_DIM // 2, 2)
    qk = jnp.swapaxes(qk, 2, 3).reshape(w.shape[0], n_qk)
    return jnp.concatenate([qk, w[:, n_qk:]], axis=1)


def _rotary_tables(s_len):
    d = HEAD_DIM
    inv = ROPE_BASE ** (-jnp.arange(0, d, 2, dtype=F32) / d)
    ang = jnp.arange(s_len, dtype=F32)[:, None] * inv[None, :]
    cos = jnp.concatenate([jnp.cos(ang), jnp.cos(ang)], axis=1)
    sin = jnp.concatenate([-jnp.sin(ang), jnp.sin(ang)], axis=1)
    return cos, sin


def _retention_tables():
    d = HEAD_DIM
    h = jnp.arange(RET_HEADS, dtype=F32)
    lg_f = jnp.log1p(-jnp.exp2(-5.0 - h))[:, None, None]
    lg_b = jnp.log1p(-jnp.exp2(-5.5 - h))[:, None, None]
    c = RET_CHUNK
    pos = jnp.arange(c, dtype=F32)
    diff = (pos[:, None] - pos[None, :])[None]
    dmat = jnp.where(diff >= 0, jnp.exp(lg_f * jnp.where(diff >= 0, diff, 0.0)),
                     jnp.exp(lg_b * jnp.where(diff < 0, -diff, 0.0)))
    col = lambda v: jnp.broadcast_to(v, (RET_HEADS, c, d))
    p = pos[None, :, None]
    dec = jnp.stack([
        dmat,
        col(jnp.exp(lg_f * (p + 1.0))),
        col(jnp.exp(lg_f * (c - 1.0 - p))),
        col(jnp.exp(lg_b * (c - p))),
        col(jnp.exp(lg_b * p)),
        col(jnp.exp(lg_f * c)),
        col(jnp.exp(lg_b * c)),
    ], axis=1)
    return dec


def retention(proj, ret_norm, b, s_len, *, heads):
    dec = _retention_tables()
    hd = HEAD_DIM
    w = heads * hd
    ng = RET_HEADS // heads
    head_blk = lambda off: pl.BlockSpec((None, s_len, w), lambda i, h: (i, 0, off + h))
    return pl.pallas_call(
        _retention_body,
        out_shape=jax.ShapeDtypeStruct((b, s_len, RET_HEADS * hd), BF16),
        grid=(b, ng),
        in_specs=[head_blk(0), head_blk(ng), head_blk(2 * ng), head_blk(3 * ng),
                  pl.BlockSpec((heads, 7, RET_CHUNK, hd), lambda i, h: (h, 0, 0, 0)),
                  pl.BlockSpec((1, w), lambda i, h: (0, h))],
        out_specs=pl.BlockSpec((None, s_len, w), lambda i, h: (i, 0, h)),
        scratch_shapes=[pltpu.VMEM((s_len, w), F32)] * 2,
        compiler_params=_params(2),
        name="retention",
    )(proj, proj, proj, proj, dec, ret_norm.reshape(1, -1))


def _gdn_gates_body(raw_ref, al_ref, dt_ref, o_ref):
    s_len = raw_ref.shape[-1]
    nh = 2 * GDN_HEADS
    gb = raw_ref[0:nh, :]
    ga = raw_ref[nh:2 * nh, :]
    beta = jax.nn.sigmoid(gb)
    g = -jnp.exp(al_ref[...]) * jax.nn.softplus(ga + dt_ref[...])
    pos = lax.broadcasted_iota(jnp.int32, (nh, s_len), 1) & (GDN_CHUNK - 1)
    fwd = g
    rev = g
    k = 1
    while k < GDN_CHUNK:
        fwd = fwd + jnp.where(pos >= k, pltpu.roll(fwd, k, 1), 0.0)
        rev = rev + jnp.where(pos < GDN_CHUNK - k, pltpu.roll(rev, s_len - k, 1), 0.0)
        k *= 2
    total = fwd + rev - g
    for h in range(GDN_HEADS):
        for d, cum in enumerate((fwd, rev)):
            r = d * GDN_HEADS + h
            o_ref[h, 3 * d:3 * d + 1, :] = cum[r:r + 1, :]
            o_ref[h, 3 * d + 1:3 * d + 2, :] = beta[r:r + 1, :]
            o_ref[h, 3 * d + 2:3 * d + 3, :] = total[r:r + 1, :]
        o_ref[h, 6:8, :] = jnp.zeros((2, s_len), F32)


def gdn_gates(raw, a_log, dt_bias, b, s_len):
    nh = 2 * GDN_HEADS
    return pl.pallas_call(
        _gdn_gates_body,
        out_shape=jax.ShapeDtypeStruct((b, GDN_HEADS, 8, s_len), F32),
        grid=(b,),
        in_specs=[pl.BlockSpec((2 * nh, s_len), lambda i: (0, i)),
                  pl.BlockSpec((nh, 1), lambda i: (0, 0)),
                  pl.BlockSpec((nh, 1), lambda i: (0, 0))],
        out_specs=pl.BlockSpec((None, GDN_HEADS, 8, s_len), lambda i: (i, 0, 0, 0)),
        compiler_params=_params(1),
        name="gdn_gates",
    )(raw, a_log.reshape(nh, 1), dt_bias.reshape(nh, 1))


def _lockstep_stages(gens):
    results = [None] * len(gens)
    live = list(range(len(gens)))
    while live:
        for i in list(live):
            try:
                next(gens[i])
            except StopIteration as done:
                results[i] = done.value
                live.remove(i)
        yield
    return results


def _lockstep(gens):
    stages = _lockstep_stages(gens)
    while True:
        try:
            next(stages)
        except StopIteration as done:
            return done.value


def _merge_run(a, b):
    ia = ib = 0
    while ia < len(a) or ib < len(b):
        if ib >= len(b) or (ia < len(a) and (ia + 1) * len(b) <= (ib + 1) * len(a)):
            a[ia]()
            ia += 1
        else:
            b[ib]()
            ib += 1


def _tri_inverse(m, ri, ci):
    def same_block(log2_size):
        return (ri >> log2_size) == (ci >> log2_size)

    base = GDN_INV_BASE_LOG2
    eye = (ri == ci).astype(F32)
    diag = jnp.where(same_block(base), -m, 0.0)
    nm = _split(diag)
    p = eye + diag
    for _ in range(base - 1):
        nm = _split(_dot3(nm, nm))
        yield
        p = p + _dot3(_split(p), nm)
        yield
    for size in range(base, GDN_CHUNK.bit_length() - 1):
        off = jnp.where(same_block(size + 1) & jnp.logical_not(same_block(size)), m, 0.0)
        p16 = p.astype(BF16)
        pc = _dot(p16, off.astype(BF16)).astype(BF16)
        yield
        p = p - _dot(pc, p16)
        yield
    return p


def _gdn_super_chunk(q, k, v, gcol, grow, bcol, glcol, st, rev):
    r = GDN_SUPER
    c = GDN_CHUNK
    ri = lax.broadcasted_iota(jnp.int32, (r, r), 0)
    ci = lax.broadcasted_iota(jnp.int32, (r, r), 1)
    same = (ri >> 6) == (ci >> 6)
    if rev:
        incl = same & (ri <= ci)
        strict = same & (ri < ci)
    else:
        incl = same & (ri >= ci)
        strict = same & (ri > ci)
    decay = jnp.exp(jnp.where(incl, gcol - grow, -jnp.inf))
    kb = k * bcol
    vb = v * bcol
    k16 = k.astype(BF16)
    m = jnp.where(strict, _dot_nt(kb.astype(BF16), k16) * decay, 0.0)
    qk = jnp.where(incl, _dot_nt(q.astype(BF16), k16) * decay, 0.0).astype(BF16)
    yield
    t = yield from _tri_inverse(m, ri, ci)
    eg = jnp.exp(gcol)
    rhs = jnp.concatenate([vb, kb * eg], axis=1).astype(BF16)
    uw = _dot(t.astype(BF16), rhs)
    u = uw[:, :HEAD_DIM]
    w = uw[:, HEAD_DIM:].astype(BF16)
    qd = (q * eg).astype(BF16)
    kt = (k * jnp.exp(glcol - gcol)).astype(BF16)
    yield
    outs = [None] * (r // c)
    order = range(r // c - 1, -1, -1) if rev else range(r // c)
    for i in order:
        sl = slice(i * c, (i + 1) * c)
        s16 = st.astype(BF16)
        vn = u[sl] - _dot(w[sl], s16)
        vn16 = vn.astype(BF16)
        yield
        pair = slice((i // 2) * 2 * c, (i // 2 + 1) * 2 * c)
        zero = jnp.zeros_like(vn16)
        vpad = jnp.concatenate([zero, vn16] if i % 2 else [vn16, zero], axis=0)
        outs[i] = _dot(qd[sl], s16) + _dot(qk[sl, pair], vpad)
        st = st * jnp.exp(glcol[i * c:i * c + 1, :]) + _dot_tn(kt[sl], vn16)
        yield
    return jnp.concatenate(outs, axis=0), st


def _gdn_core_body(q_ref, k_ref, v_ref, z_ref, row_ref, gn_ref, o_ref, col_s, of_s, ob_s):
    s_len = q_ref.shape[0]
    hd = HEAD_DIM
    heads = q_ref.shape[1] // hd
    r = GDN_SUPER
    n = s_len // r

    for hh in range(heads):
        rows8 = jnp.concatenate([row_ref[hh, i] for i in range(n)], axis=1)
        padded = jnp.concatenate([rows8, jnp.zeros((LANES - 8, s_len), F32)], axis=0)
        col_s[hh] = padded.T

    def one(hh, i, st, rev):
        base = 3 if rev else 0
        rows = pl.ds(pl.multiple_of(i * r, r), r)
        lanes = slice(hh * hd, (hh + 1) * hd)
        cols = col_s[hh, rows, :]
        gcol = cols[:, base:base + 1]
        bcol = cols[:, base + 1:base + 2]
        glcol = cols[:, base + 2:base + 3]
        grow = row_ref[hh, i][base:base + 1, :]
        return _gdn_super_chunk(q_ref[rows, lanes], k_ref[rows, lanes], v_ref[rows, lanes],
                                gcol, grow, bcol, glcol, st, rev)

    def step(t, carry):
        fwd_rows = pl.ds(pl.multiple_of(t * r, r), r)
        bwd_rows = pl.ds(pl.multiple_of((n - 1 - t) * r, r), r)
        gens = []
        for hh in range(heads):
            gens.append(one(hh, t, carry[2 * hh], False))
            gens.append(one(hh, n - 1 - t, carry[2 * hh + 1], True))
        res = _lockstep(gens)
        for hh in range(heads):
            lanes = slice(hh * hd, (hh + 1) * hd)
            of_s[fwd_rows, lanes] = res[2 * hh][0]
            ob_s[bwd_rows, lanes] = res[2 * hh + 1][0]
        return tuple(st for _, st in res)

    zero = jnp.zeros((hd, hd), F32)
    lax.fori_loop(0, n, step, (zero,) * (2 * heads))
    for hh in range(heads):
        lanes = slice(hh * hd, (hh + 1) * hd)
        o = of_s[:, lanes] + ob_s[:, lanes]
        y = o * lax.rsqrt(jnp.mean(o * o, axis=-1, keepdims=True) + EPS) * gn_ref[...]
        o_ref[:, lanes] = (y * jax.nn.silu(z_ref[:, lanes])).astype(o_ref.dtype)


def gdn_core(proj, gates, gdn_norm, b, s_len, *, heads):
    hd = HEAD_DIM
    n_super = s_len // GDN_SUPER
    gate_row = gates.reshape(b, GDN_HEADS, 8, n_super, GDN_SUPER).transpose(0, 1, 3, 2, 4)
    w = heads * hd
    ng = GDN_HEADS // heads
    first = 4 * RET_HEADS // heads
    qkv_blk = lambda off: pl.BlockSpec((None, s_len, w), lambda i, h: (i, 0, first + off + h))
    return pl.pallas_call(
        _gdn_core_body,
        out_shape=jax.ShapeDtypeStruct((b, s_len, GDN_HEADS * hd), BF16),
        grid=(b, ng),
        in_specs=[qkv_blk(0), qkv_blk(ng), qkv_blk(2 * ng),
                  pl.BlockSpec((None, s_len, w), lambda i, h: (i, 0, first + 3 * ng + h),
                               pipeline_mode=pl.Buffered(1)),
                  pl.BlockSpec((None, heads, n_super, 8, GDN_SUPER), lambda i, h: (i, h, 0, 0, 0)),
                  pl.BlockSpec((1, hd), lambda i, h: (0, 0))],
        out_specs=pl.BlockSpec((None, s_len, w), lambda i, h: (i, 0, h)),
        scratch_shapes=[pltpu.VMEM((heads, s_len, LANES), F32),
                        pltpu.VMEM((s_len, w), F32), pltpu.VMEM((s_len, w), F32)],
        compiler_params=_params(2),
        name="gdn_core",
    )(proj, proj, proj, proj, gate_row, gdn_norm.reshape(1, hd))


def _mixer_xattn_body(x_ref, a1_ref, a2_ref, w1_ref, w2_ref, gxa_ref, wq_ref, kv_ref, wo_ref,
                      gffn_ref, wpq_ref, o_ref, pq_ref):
    x = x_ref[...] + _dot(a1_ref[...], w1_ref[...]) + _dot(a2_ref[...], w2_ref[...])
    d = x.shape[1]
    dh = d // XA_HEADS
    q = _dot(_rms(x, gxa_ref[...]).astype(BF16), wq_ref[...])
    outs = []
    for h in range(XA_HEADS):
        kh = kv_ref[:, h * dh:(h + 1) * dh].astype(BF16)
        vh = kv_ref[:, d + h * dh:d + (h + 1) * dh].astype(BF16)
        sc = _dot_nt(q[:, h * dh:(h + 1) * dh].astype(BF16), kh) * (dh ** -0.5)
        p = jax.nn.softmax(sc, axis=-1)
        outs.append(_dot(p.astype(BF16), vh).astype(BF16))
    x = x + _dot(jnp.concatenate(outs, axis=1), wo_ref[...])
    o_ref[...] = x
    pq_ref[...] = _dot(_rms(x, gffn_ref[...]).astype(BF16), wpq_ref[...])


def mixer_xattn_peerq(x, a1, a2, w1, w2, g_xa, wq, kv, wo, g_ffn, wpq, b, s_len, *, ts):
    t, d = x.shape
    k1, k2 = a1.shape[1], a2.shape[1]
    m = kv.shape[1]
    npq = wpq.shape[1]
    nblk = s_len // ts
    tok = lambda n: pl.BlockSpec((ts, n), lambda i, j: (i * nblk + j, 0))
    whole = lambda r, c: pl.BlockSpec((r, c), lambda i, j: (0, 0))
    return pl.pallas_call(
        _mixer_xattn_body,
        out_shape=(jax.ShapeDtypeStruct((t, d), F32), jax.ShapeDtypeStruct((t, npq), F32)),
        grid=(b, nblk),
        in_specs=[tok(d), tok(k1), tok(k2), whole(k1, d), whole(k2, d),
                  whole(1, d), whole(d, d),
                  pl.BlockSpec((None, m, 2 * d), lambda i, j: (i, 0, 0)),
                  whole(d, d), whole(1, d), whole(d, npq)],
        out_specs=(tok(d), tok(npq)),
        compiler_params=_params(2),
        name="mixer_xattn_peerq",
    )(x, a1, a2, w1, w2, g_xa.reshape(1, d), wq, kv, wo, g_ffn.reshape(1, d), wpq)


def _extract_topk(s, k, break_ties):
    vals = []
    if break_ties:
        n_rows = s.shape[0]
        iota = lax.broadcasted_iota(jnp.int32, s.shape, 0).astype(F32)
        rank = jnp.full(s.shape, float(k), F32)
        for r in range(k):
            m = jnp.max(s, axis=0, keepdims=True)
            idx = jnp.min(jnp.where(s == m, iota, float(n_rows)), axis=0, keepdims=True)
            hit = iota == idx
            rank = jnp.where(hit, float(r), rank)
            s = jnp.where(hit, -jnp.inf, s)
            vals.append(m)
            yield
        taken = rank < float(k)
    else:
        for r in range(k):
            m = jnp.max(s, axis=0, keepdims=True)
            s = jnp.where(s == m, -(TAKEN + r * TAKEN_STEP), s)
            vals.append(m)
            yield
        taken = s <= -TAKEN
        rank = jnp.where(taken, s * (-1.0 / TAKEN_STEP) - TAKEN / TAKEN_STEP, float(k))
    count = jnp.sum(jnp.where(taken, 1.0, 0.0), axis=0, keepdims=True)
    return rank, vals, count


def _route_tokens(q, keys_ref, break_ties):
    nk = PEER_NKEYS
    kk = PEER_TOPK
    s1 = _dot_nt(keys_ref[0], q[:, :nk])
    s2 = _dot_nt(keys_ref[1], q[:, nk:])
    (r1, v1, n_sel1), (r2, v2, n_sel2) = yield from _lockstep_stages(
        [_extract_topk(s1, kk, break_ties), _extract_topk(s2, kk, break_ties)])
    v1m = jnp.concatenate(v1, axis=0)
    v2m = jnp.concatenate(v2, axis=0)
    pad = -jnp.inf if break_ties else FAST_PAD
    row8 = lax.broadcasted_iota(jnp.int32, (8, LANES), 0)
    groups = [v1[0] + v2m[0:8], v1[0] + v2m[8:16]]
    for a in range(1, 8):
        groups.append(jnp.where(row8 < kk // (a + 1), v1[a] + v2m[0:8], pad))
    groups.append(v1m[8:16] + v2[0])
    cand = jnp.concatenate(groups, axis=0)
    rc, _, n_selc = yield from _extract_topk(cand, kk, break_ties)
    sel = rc < float(kk)
    top = v1[0] + v2[0]
    z = jnp.sum(jnp.where(sel, jnp.exp(cand - top), 0.0), axis=0, keepdims=True)
    sel_f = jnp.where(sel, 1.0, 0.0)
    n1 = jnp.zeros((nk, LANES), F32)
    for a in range(kk):
        if a == 0:
            cnt = jnp.sum(sel_f[0:16], axis=0, keepdims=True)
        elif a < 8:
            cnt = jnp.sum(sel_f[8 + 8 * a:16 + 8 * a], axis=0, keepdims=True)
        else:
            cnt = sel_f[64 + a:65 + a]
        n1 = jnp.where(r1 == float(a), cnt, n1)
    c1 = jnp.where(r1 < float(kk), jnp.exp(s1 - v1[0]) / z, 0.0)
    e2 = jnp.exp(s2 - v2[0])
    exact = ((n_sel1 == float(kk)) & (n_sel2 == float(kk)) & (n_selc == float(kk))
             & (v1[kk - 1] > -FAST_SCORE_BOUND) & (v2[kk - 1] > -FAST_SCORE_BOUND))
    return c1, n1, e2, r2, jnp.where(exact, 1.0, 0.0)


def _peer_route_body(q_ref, keys_ref, c1_ref, n1_ref, e2_ref, r2_ref):
    tb = q_ref.shape[0]

    def run(break_ties):
        groups = [slice(blk * LANES, (blk + 1) * LANES) for blk in range(tb // LANES)]
        results = _lockstep([_route_tokens(q_ref[tok, :].astype(BF16), keys_ref, break_ties)
                             for tok in groups])
        flags = []
        for tok, (c1, n1, e2, r2, ok) in zip(groups, results):
            c1_ref[:, tok] = c1
            n1_ref[:, tok] = n1
            e2_ref[:, tok] = e2.astype(e2_ref.dtype)
            r2_ref[:, tok] = r2.astype(r2_ref.dtype)
            flags.append(ok)
        return jnp.min(jnp.concatenate(flags, axis=1))

    all_exact = run(False)

    @pl.when(all_exact < 0.5)
    def _():
        run(True)


def peer_route(q, keys, *, tb):
    t = q.shape[0]
    nk = PEER_NKEYS
    out = lambda dt: jax.ShapeDtypeStruct((PEER_HEADS, nk, t), dt)
    ospec = pl.BlockSpec((None, nk, tb), lambda i, h: (h, 0, i))
    return pl.pallas_call(
        _peer_route_body,
        out_shape=(out(F32), out(F32), out(BF16), out(BF16)),
        grid=(t // tb, PEER_HEADS),
        in_specs=[pl.BlockSpec((tb, 2 * nk), lambda i, h: (i, h)),
                  pl.BlockSpec((None, 2, nk, nk), lambda i, h: (h, 0, 0, 0))],
        out_specs=(ospec,) * 4,
        compiler_params=_params(2),
        name="peer_route",
    )(q, keys)


def _gelu_tanh(x):
    c0 = 0.7978845608028654
    c1 = 0.7978845608028654 * 0.044715
    return (0.5 * x) * (1.0 + jnp.tanh(x * (c0 + c1 * (x * x))))


def _peer_dense_body(x_ref, g_ref, c1_ref, n1_ref, e2_ref, r2_ref, u_ref, vt_ref, *rest, final):
    if final:
        gf_ref, o_ref, xn_s, acc_s = rest
    else:
        o_ref, xn_s, acc_s = rest
    j = pl.program_id(1)
    nk = PEER_NKEYS
    tb = x_ref.shape[0]
    grp = u_ref.shape[0] // nk
    sub = e2_ref.shape[2]

    @pl.when(j == 0)
    def _():
        xn_s[...] = _rms(x_ref[...], g_ref[...]).T.astype(BF16)
        acc_s[...] = jnp.zeros_like(acc_s)

    per = MXU_DEPTH // nk
    n_chunks = grp // per
    d_model = vt_ref.shape[0]
    xn = xn_s[...]
    scores, gates, weights, chunk_w = {}, {}, {}, {}

    def score_pieces(c):
        def piece(ii):
            scores[ii] = _dot(u_ref[ii * nk:(ii + 1) * nk, :], xn)
        return [functools.partial(piece, c * per + i) for i in range(per)]

    def gate_pieces(c):
        def piece(ii, h):
            row = pl.ds(j * grp + ii, 1)
            c1 = jnp.broadcast_to(c1_ref[h, row, :], (sub, tb)).astype(e2_ref.dtype)[None]
            n1 = jnp.broadcast_to(n1_ref[h, row, :], (sub, tb)).astype(e2_ref.dtype)[None]
            e2 = e2_ref[h]
            term = jnp.where(r2_ref[h] < n1, e2, jnp.zeros_like(e2)) * c1
            gates[ii] = term if h == 0 else gates[ii] + term
        return [functools.partial(piece, c * per + i, h) for i in range(per) for h in range(PEER_HEADS)]

    def act_pieces(c):
        def piece(ii):
            act = _gelu_tanh(scores.pop(ii).astype(e2_ref.dtype))
            weights[ii] = act * gates.pop(ii).reshape(nk, tb)
        return [functools.partial(piece, c * per + i) for i in range(per)]

    def value_pieces(c):
        ex = slice(c * MXU_DEPTH, (c + 1) * MXU_DEPTH)

        def piece(m):
            if m == 0:
                chunk_w[c] = jnp.concatenate([weights.pop(c * per + i) for i in range(per)], axis=0)
            rows = slice(m * MXU_DEPTH, (m + 1) * MXU_DEPTH)
            acc_s[rows, :] += _dot(vt_ref[rows, ex], chunk_w[c])
        return [functools.partial(piece, m) for m in range(d_model // MXU_DEPTH)]

    _merge_run(score_pieces(0), gate_pieces(0))
    for c in range(n_chunks):
        mxu = (score_pieces(c + 1) if c + 1 < n_chunks else []) + (value_pieces(c - 1) if c else [])
        vpu = act_pieces(c) + (gate_pieces(c + 1) if c + 1 < n_chunks else [])
        _merge_run(mxu, vpu)
    _merge_run(value_pieces(n_chunks - 1), [])

    @pl.when(j == pl.num_programs(1) - 1)
    def _():
        y = x_ref[...] + acc_s[...].T
        if final:
            y = _rms(y, gf_ref[...])
        o_ref[...] = y


def peer_dense_residual(x, g, route, u, layer, vt, g_final, *, tb, grp, blocks=None):
    t, d = x.shape
    nk = PEER_NKEYS
    eb = grp * nk
    first, count = blocks if blocks is not None else (0, t // tb)
    c1, n1, e2, r2 = route
    sub = 32 // jnp.dtype(e2.dtype).itemsize
    e2 = e2.reshape(PEER_HEADS, nk // sub, sub, t)
    r2 = r2.reshape(PEER_HEADS, nk // sub, sub, t)
    rspec = pl.BlockSpec((PEER_HEADS, nk, tb), lambda i, j: (0, 0, first + i))
    pspec = pl.BlockSpec((PEER_HEADS, nk // sub, sub, tb), lambda i, j: (0, 0, 0, first + i))
    in_specs = [pl.BlockSpec((tb, d), lambda i, j: (first + i, 0)),
                pl.BlockSpec((1, d), lambda i, j: (0, 0)),
                rspec, rspec, pspec, pspec,
                pl.BlockSpec((None, eb, d), lambda i, j: (layer, j, 0)),
                pl.BlockSpec((d, eb), lambda i, j: (0, j))]
    args = [x, g.reshape(1, d), c1, n1, e2, r2, u, vt]
    final = g_final is not None
    if final:
        in_specs.append(pl.BlockSpec((1, d), lambda i, j: (0, 0)))
        args.append(g_final.reshape(1, d))
    return pl.pallas_call(
        functools.partial(_peer_dense_body, final=final),
        out_shape=jax.ShapeDtypeStruct((count * tb, d), F32),
        grid=(count, (nk * nk) // eb),
        in_specs=in_specs,
        out_specs=pl.BlockSpec((tb, d), lambda i, j: (i, 0)),
        scratch_shapes=[pltpu.VMEM((d, tb), BF16),
                        pltpu.VMEM((d, tb), F32)],
        compiler_params=_params(2),
        name="peer_dense_final" if final else "peer_dense",
    )(*args)


def _transpose_cast_body(x_ref, o_ref):
    o_ref[...] = x_ref[...].T.astype(o_ref.dtype)


def transpose_cast(tables, layer):
    _, rows, cols = tables.shape
    tr = _pick(rows, (1024, 512, 256, 128))
    return pl.pallas_call(
        _transpose_cast_body,
        out_shape=jax.ShapeDtypeStruct((cols, rows), BF16),
        grid=(rows // tr,),
        in_specs=[pl.BlockSpec((None, tr, cols), lambda i: (layer, i, 0))],
        out_specs=pl.BlockSpec((cols, tr), lambda i: (0, i)),
        compiler_params=_params(1),
        name="transpose_cast",
    )(tables)


def _pick(n, prefs):
    for p in prefs:
        if n % p == 0:
            return p
    return n


def _encoder(group_batches, x3, mem3, norm_mix, w_in, ret_norm, gdn_conv, gdn_a_log, gdn_dt_bias,
             gdn_norm, w_out, norm_xa, norm_mem, w_xq, w_xkv, w_xo, norm_ffn, peer_wq, peer_keys,
             peer_u, peer_v, norm_final):
    b, s_len, d = x3.shape
    n_mem = mem3.shape[1]
    t = b * s_len
    depth = w_in.shape[0]
    x = x3.reshape(t, d)
    mem = mem3.reshape(b * n_mem, d)
    n_main = 4 * RET_HEADS * HEAD_DIM + 4 * GDN_HEADS * HEAD_DIM
    tmm = _pick(b * n_mem, (1024, 512, 256, 128))
    ts = _pick(s_len, (512, 256, 128))
    tb_route = _pick(t, (256, 128))
    tb_dense = _pick(s_len, (512, 256, 128))
    peer_u16 = peer_u.astype(BF16)

    for l in range(depth):
        w_main = _deinterleave_rotary_columns(w_in[l, :, :n_main]).astype(BF16)
        w_gate_t = w_in[l, :, n_main:].T.astype(BF16)
        proj, graw = mixer_in_proj(x, norm_mix[l], w_main, w_gate_t, gdn_conv[l], s_len)
        proj3 = proj.reshape(b, s_len, n_main)
        o_r = retention(proj3, ret_norm[l], b, s_len, heads=RET_HEADS_PER_STEP)
        gates = gdn_gates(graw, gdn_a_log[l], gdn_dt_bias[l], b, s_len)
        o_g = gdn_core(proj3, gates, gdn_norm[l], b, s_len, heads=GDN_HEADS_PER_STEP)
        w_o = w_out[l].astype(BF16)
        n_r = RET_HEADS * HEAD_DIM

        kv = norm_matmul(mem, norm_mem[l], w_xkv[l].astype(BF16), tm=tmm, tn=1024)
        x, pq = mixer_xattn_peerq(x, o_r.reshape(t, -1), o_g.reshape(t, -1), w_o[:n_r], w_o[n_r:],
                                  norm_xa[l], w_xq[l].astype(BF16), kv.reshape(b, n_mem, 2 * d),
                                  w_xo[l].astype(BF16), norm_ffn[l], peer_wq[l].astype(BF16),
                                  b, s_len, ts=ts)

        route = peer_route(pq, peer_keys[l].astype(BF16), tb=tb_route)
        dense = functools.partial(peer_dense_residual, x, norm_ffn[l], route, peer_u16, l,
                                  transpose_cast(peer_v, l), tb=tb_dense, grp=16)
        if l < depth - 1:
            x = dense(None)
    outs, first = [], 0
    for nb in group_batches:
        n_blocks = nb * s_len // tb_dense
        outs.append(dense(norm_final, blocks=(first, n_blocks)).reshape(nb, s_len, d))
        first += n_blocks
    return tuple(outs)


def kernel(x_prompt, x_sample, mem_prompt, mem_sample, norm_mix, w_in, ret_norm, gdn_conv, gdn_a_log,
           gdn_dt_bias, gdn_norm, w_out, norm_xa, norm_mem, w_xq, w_xkv, w_xo, norm_ffn, peer_wq,
           peer_keys, peer_u, peer_v, norm_final):
    weights = (norm_mix, w_in, ret_norm, gdn_conv, gdn_a_log, gdn_dt_bias, gdn_norm, w_out,
               norm_xa, norm_mem, w_xq, w_xkv, w_xo, norm_ffn, peer_wq, peer_keys, peer_u, peer_v,
               norm_final)
    if x_prompt.shape[1:] == x_sample.shape[1:] and mem_prompt.shape[1:] == mem_sample.shape[1:]:
        return _encoder((x_prompt.shape[0], x_sample.shape[0]),
                        jnp.concatenate([x_prompt, x_sample], axis=0),
                        jnp.concatenate([mem_prompt, mem_sample], axis=0), *weights)
    return (_encoder((x_prompt.shape[0],), x_prompt, mem_prompt, *weights)[0],
            _encoder((x_sample.shape[0],), x_sample, mem_sample, *weights)[0])
```

```python
import functools

import jax
import jax.numpy as jnp
from jax import lax
from jax.experimental import pallas as pl
from jax.experimental.pallas import tpu as pltpu

F32 = jnp.float32
BF16 = jnp.bfloat16

EPS = 1e-6
LANES = 128
MXU_DEPTH = 256
DENSE_CHUNK = 512
HEAD_DIM = 128
RET_HEADS = 4
GDN_HEADS = 4
RET_CHUNK = 128
GDN_CHUNK = 64
GDN_SUPER = 256
GDN_INV_BASE_LOG2 = 1
GDN_HEADS_PER_STEP = 4
RET_HEADS_PER_STEP = 4
ROPE_BASE = 10000.0
XA_HEADS = 4
PEER_HEADS = 8
PEER_NKEYS = 128
PEER_TOPK = 16
TAKEN = 2.0 ** 100
TAKEN_STEP = 2.0 ** 96
FAST_PAD = -(2.0 ** 90)
FAST_SCORE_BOUND = 2.0 ** 80
VMEM_LIMIT = 56 * 1024 * 1024

_NT = (((1,), (1,)), ((), ()))
_TN = (((0,), (0,)), ((), ()))


def _params(n_axes):
    return pltpu.CompilerParams(
        dimension_semantics=("arbitrary",) * n_axes, vmem_limit_bytes=VMEM_LIMIT)


def _dot(a, b):
    return jnp.dot(a, b, preferred_element_type=F32)


def _dot_nt(a, b):
    return lax.dot_general(a, b, _NT, preferred_element_type=F32)


def _dot_tn(a, b):
    return lax.dot_general(a, b, _TN, preferred_element_type=F32)


def _split(a):
    hi = a.astype(BF16)
    return hi, (a - hi.astype(F32)).astype(BF16)


def _dot3(a, b):
    (ah, al), (bh, bl) = a, b
    return _dot(jnp.concatenate([ah, ah, al], axis=1), jnp.concatenate([bh, bl, bh], axis=0))


def _rms(x, g):
    return x * lax.rsqrt(jnp.mean(x * x, axis=-1, keepdims=True) + EPS) * g


def _norm_matmul_body(x_ref, g_ref, w_ref, o_ref, xn_ref):
    @pl.when(pl.program_id(1) == 0)
    def _():
        xn_ref[...] = _rms(x_ref[...], g_ref[...]).astype(BF16)

    o_ref[...] = _dot(xn_ref[...], w_ref[...]).astype(o_ref.dtype)


def norm_matmul(x, g, w, *, tm, tn, out_dtype=F32):
    t, d = x.shape
    n = w.shape[1]
    return pl.pallas_call(
        _norm_matmul_body,
        out_shape=jax.ShapeDtypeStruct((t, n), out_dtype),
        grid=(t // tm, n // tn),
        in_specs=[pl.BlockSpec((tm, d), lambda i, j: (i, 0)),
                  pl.BlockSpec((1, d), lambda i, j: (0, 0)),
                  pl.BlockSpec((d, tn), lambda i, j: (0, j))],
        out_specs=pl.BlockSpec((tm, tn), lambda i, j: (i, j)),
        scratch_shapes=[pltpu.VMEM((tm, d), BF16)],
        compiler_params=_params(2),
        name="norm_matmul",
    )(x, g.reshape(1, d), w)


def _mixer_in_proj_body(x_ref, g_ref, w_ref, ws_ref, cos_ref, sin_ref, cw_ref, o_ref, os_ref, xn_ref):
    j = pl.program_id(1)
    hd = HEAD_DIM
    s_len = x_ref.shape[0]

    @pl.when(j == 0)
    def _():
        xn = _rms(x_ref[...], g_ref[...]).astype(BF16)
        xn_ref[...] = xn
        os_ref[...] = _dot_nt(ws_ref[...], xn)

    o_ref[...] = _dot(xn_ref[...], w_ref[...])

    @pl.when(j == 0)
    def _():
        cos = cos_ref[...]
        sin = sin_ref[...]
        for head in range(2 * RET_HEADS):
            cols = slice(head * hd, (head + 1) * hd)
            x = o_ref[:, cols]
            y = x * cos + pltpu.roll(x, hd // 2, 1) * sin
            o_ref[:, cols] = y * (hd ** -0.5) if head >= RET_HEADS else y

    def conv_silu(cols):
        x = o_ref[:, cols]
        t = lax.broadcasted_iota(jnp.int32, x.shape, 0)
        prev = jnp.where(t == 0, 0.0, pltpu.roll(x, 1, 0))
        nxt = jnp.where(t == s_len - 1, 0.0, pltpu.roll(x, s_len - 1, 0))
        return jax.nn.silu(prev * cw_ref[0:1, cols] + x * cw_ref[1:2, cols] + nxt * cw_ref[2:3, cols])

    @pl.when(j == 2)
    def _():
        for head in range(2 * GDN_HEADS):
            cols = slice(head * hd, (head + 1) * hd)
            act = conv_silu(cols)
            inv = lax.rsqrt(jnp.sum(act * act, axis=-1, keepdims=True) + EPS)
            o_ref[:, cols] = act * (inv * (hd ** -0.5) if head < GDN_HEADS else inv)

    @pl.when(j == 3)
    def _():
        for head in range(GDN_HEADS):
            cols = slice(head * hd, (head + 1) * hd)
            o_ref[:, cols] = conv_silu(cols)


def mixer_in_proj(x, g, w, ws_t, conv_w, s_len):
    t, d = x.shape
    n = w.shape[1]
    rows = ws_t.shape[0]
    tn = 2 * RET_HEADS * HEAD_DIM
    assert tn == 2 * GDN_HEADS * HEAD_DIM and n == 4 * tn
    cos, sin = _rotary_tables(s_len)
    conv_w = jnp.pad(conv_w, ((0, 0), (0, 2 * tn - conv_w.shape[1])))
    return pl.pallas_call(
        _mixer_in_proj_body,
        out_shape=(jax.ShapeDtypeStruct((t, n), F32),
                   jax.ShapeDtypeStruct((rows, t), F32)),
        grid=(t // s_len, n // tn),
        in_specs=[pl.BlockSpec((s_len, d), lambda i, j: (i, 0)),
                  pl.BlockSpec((1, d), lambda i, j: (0, 0)),
                  pl.BlockSpec((d, tn), lambda i, j: (0, j)),
                  pl.BlockSpec((rows, d), lambda i, j: (0, 0)),
                  pl.BlockSpec((s_len, HEAD_DIM), lambda i, j: (0, 0)),
                  pl.BlockSpec((s_len, HEAD_DIM), lambda i, j: (0, 0)),
                  pl.BlockSpec((3, tn), lambda i, j: (0, jnp.maximum(j - 2, 0)))],
        out_specs=(pl.BlockSpec((s_len, tn), lambda i, j: (i, j)),
                   pl.BlockSpec((rows, s_len), lambda i, j: (0, i))),
        scratch_shapes=[pltpu.VMEM((s_len, d), BF16)],
        compiler_params=_params(2),
        name="mixer_in_proj",
    )(x, g.reshape(1, d), w, ws_t, cos, sin, conv_w)


def _retention_body(qr_s, kr_s, v_ref, g_ref, dec_ref, rn_ref, o_ref, of_s, ob_s):
    s_len = qr_s.shape[0]
    hd = HEAD_DIM
    heads = qr_s.shape[1] // hd
    c = RET_CHUNK
    n = s_len // c

    def forward(hh, i, st):
        r = pl.ds(pl.multiple_of(i * c, c), c)
        lanes = slice(hh * hd, (hh + 1) * hd)
        qc, kc, vb = qr_s[r, lanes], kr_s[r, lanes], v_ref[r, lanes].astype(BF16)
        sc = _dot_nt(qc.astype(BF16), kc.astype(BF16)) * dec_ref[hh, 0]
        inter = _dot((qc * dec_ref[hh, 1]).astype(BF16), st.astype(BF16))
        st = st * dec_ref[hh, 5] + _dot_tn((kc * dec_ref[hh, 2]).astype(BF16), vb)
        yield
        return inter + _dot(sc.astype(BF16), vb), st

    def backward(hh, i, st):
        r = pl.ds(pl.multiple_of(i * c, c), c)
        lanes = slice(hh * hd, (hh + 1) * hd)
        qc, kc, vb = qr_s[r, lanes], kr_s[r, lanes], v_ref[r, lanes].astype(BF16)
        inter = _dot((qc * dec_ref[hh, 3]).astype(BF16), st.astype(BF16))
        st = st * dec_ref[hh, 6] + _dot_tn((kc * dec_ref[hh, 4]).astype(BF16), vb)
        yield
        return inter, st

    def step(t, carry):
        fwd_rows = pl.ds(pl.multiple_of(t * c, c), c)
        bwd_rows = pl.ds(pl.multiple_of((n - 1 - t) * c, c), c)
        gens = []
        for hh in range(heads):
            gens.append(forward(hh, t, carry[2 * hh]))
            gens.append(backward(hh, n - 1 - t, carry[2 * hh + 1]))
        res = _lockstep(gens)
        for hh in range(heads):
            lanes = slice(hh * hd, (hh + 1) * hd)
            of_s[fwd_rows, lanes] = res[2 * hh][0]
            ob_s[bwd_rows, lanes] = res[2 * hh + 1][0]
        return tuple(st for _, st in res)

    zero = jnp.zeros((hd, hd), F32)
    lax.fori_loop(0, n, step, (zero,) * (2 * heads))

    for hh in range(heads):
        lanes = slice(hh * hd, (hh + 1) * hd)
        o = of_s[:, lanes] + ob_s[:, lanes]
        mu = jnp.mean(o, axis=-1, keepdims=True)
        var = jnp.mean(jnp.square(o - mu), axis=-1, keepdims=True)
        y = (o - mu) * lax.rsqrt(var + EPS)
        y = y * rn_ref[:, lanes] * jax.nn.silu(g_ref[:, lanes])
        o_ref[:, lanes] = y.astype(o_ref.dtype)


def _deinterleave_rotary_columns(w):
    n_qk = 2 * RET_HEADS * HEAD_DIM
    qk = w[:, :n_qk].reshape(w.shape[0], 2 * RET_HEADS, HEAD_DIM // 2, 2)
    qk = jnp.swapaxes(qk, 2, 3).reshape(w.shape[0], n_qk)
    return jnp.concatenate([qk, w[:, n_qk:]], axis=1)


def _rotary_tables(s_len):
    d = HEAD_DIM
    inv = ROPE_BASE ** (-jnp.arange(0, d, 2, dtype=F32) / d)
    ang = jnp.arange(s_len, dtype=F32)[:, None] * inv[None, :]
    cos = jnp.concatenate([jnp.cos(ang), jnp.cos(ang)], axis=1)
    sin = jnp.concatenate([-jnp.sin(ang), jnp.sin(ang)], axis=1)
    return cos, sin


def _retention_tables():
    d = HEAD_DIM
    h = jnp.arange(RET_HEADS, dtype=F32)
    lg_f = jnp.log1p(-jnp.exp2(-5.0 - h))[:, None, None]
    lg_b = jnp.log1p(-jnp.exp2(-5.5 - h))[:, None, None]
    c = RET_CHUNK
    pos = jnp.arange(c, dtype=F32)
    diff = (pos[:, None] - pos[None, :])[None]
    dmat = jnp.where(diff >= 0, jnp.exp(lg_f * jnp.where(diff >= 0, diff, 0.0)),
                     jnp.exp(lg_b * jnp.where(diff < 0, -diff, 0.0)))
    col = lambda v: jnp.broadcast_to(v, (RET_HEADS, c, d))
    p = pos[None, :, None]
    dec = jnp.stack([
        dmat,
        col(jnp.exp(lg_f * (p + 1.0))),
        col(jnp.exp(lg_f * (c - 1.0 - p))),
        col(jnp.exp(lg_b * (c - p))),
        col(jnp.exp(lg_b * p)),
        col(jnp.exp(lg_f * c)),
        col(jnp.exp(lg_b * c)),
    ], axis=1)
    return dec


def retention(proj, ret_norm, b, s_len, *, heads):
    dec = _retention_tables()
    hd = HEAD_DIM
    w = heads * hd
    ng = RET_HEADS // heads
    head_blk = lambda off: pl.BlockSpec((None, s_len, w), lambda i, h: (i, 0, off + h))
    return pl.pallas_call(
        _retention_body,
        out_shape=jax.ShapeDtypeStruct((b, s_len, RET_HEADS * hd), BF16),
        grid=(b, ng),
        in_specs=[head_blk(0), head_blk(ng), head_blk(2 * ng), head_blk(3 * ng),
                  pl.BlockSpec((heads, 7, RET_CHUNK, hd), lambda i, h: (h, 0, 0, 0)),
                  pl.BlockSpec((1, w), lambda i, h: (0, h))],
        out_specs=pl.BlockSpec((None, s_len, w), lambda i, h: (i, 0, h)),
        scratch_shapes=[pltpu.VMEM((s_len, w), F32)] * 2,
        compiler_params=_params(2),
        name="retention",
    )(proj, proj, proj, proj, dec, ret_norm.reshape(1, -1))


def _gdn_gates_body(raw_ref, al_ref, dt_ref, o_ref):
    s_len = raw_ref.shape[-1]
    nh = 2 * GDN_HEADS
    gb = raw_ref[0:nh, :]
    ga = raw_ref[nh:2 * nh, :]
    beta = jax.nn.sigmoid(gb)
    g = -jnp.exp(al_ref[...]) * jax.nn.softplus(ga + dt_ref[...])
    pos = lax.broadcasted_iota(jnp.int32, (nh, s_len), 1) & (GDN_CHUNK - 1)
    fwd = g
    rev = g
    k = 1
    while k < GDN_CHUNK:
        fwd = fwd + jnp.where(pos >= k, pltpu.roll(fwd, k, 1), 0.0)
        rev = rev + jnp.where(pos < GDN_CHUNK - k, pltpu.roll(rev, s_len - k, 1), 0.0)
        k *= 2
    total = fwd + rev - g
    for h in range(GDN_HEADS):
        for d, cum in enumerate((fwd, rev)):
            r = d * GDN_HEADS + h
            o_ref[h, 3 * d:3 * d + 1, :] = cum[r:r + 1, :]
            o_ref[h, 3 * d + 1:3 * d + 2, :] = beta[r:r + 1, :]
            o_ref[h, 3 * d + 2:3 * d + 3, :] = total[r:r + 1, :]
        o_ref[h, 6:8, :] = jnp.zeros((2, s_len), F32)


def gdn_gates(raw, a_log, dt_bias, b, s_len):
    nh = 2 * GDN_HEADS
    return pl.pallas_call(
        _gdn_gates_body,
        out_shape=jax.ShapeDtypeStruct((b, GDN_HEADS, 8, s_len), F32),
        grid=(b,),
        in_specs=[pl.BlockSpec((2 * nh, s_len), lambda i: (0, i)),
                  pl.BlockSpec((nh, 1), lambda i: (0, 0)),
                  pl.BlockSpec((nh, 1), lambda i: (0, 0))],
        out_specs=pl.BlockSpec((None, GDN_HEADS, 8, s_len), lambda i: (i, 0, 0, 0)),
        compiler_params=_params(1),
        name="gdn_gates",
    )(raw, a_log.reshape(nh, 1), dt_bias.reshape(nh, 1))


def _lockstep_stages(gens):
    results = [None] * len(gens)
    live = list(range(len(gens)))
    while live:
        for i in list(live):
            try:
                next(gens[i])
            except StopIteration as done:
                results[i] = done.value
                live.remove(i)
        yield
    return results


def _lockstep(gens):
    stages = _lockstep_stages(gens)
    while True:
        try:
            next(stages)
        except StopIteration as done:
            return done.value


def _merge_run(a, b):
    ia = ib = 0
    while ia < len(a) or ib < len(b):
        if ib >= len(b) or (ia < len(a) and (ia + 1) * len(b) <= (ib + 1) * len(a)):
            a[ia]()
            ia += 1
        else:
            b[ib]()
            ib += 1


def _tri_inverse(m, ri, ci):
    def same_block(log2_size):
        return (ri >> log2_size) == (ci >> log2_size)

    base = GDN_INV_BASE_LOG2
    eye = (ri == ci).astype(F32)
    diag = jnp.where(same_block(base), -m, 0.0)
    nm = _split(diag)
    p = eye + diag
    for _ in range(base - 1):
        nm = _split(_dot3(nm, nm))
        yield
        p = p + _dot3(_split(p), nm)
        yield
    for size in range(base, GDN_CHUNK.bit_length() - 1):
        off = jnp.where(same_block(size + 1) & jnp.logical_not(same_block(size)), m, 0.0)
        p16 = p.astype(BF16)
        pc = _dot(p16, off.astype(BF16)).astype(BF16)
        yield
        p = p - _dot(pc, p16)
        yield
    return p


def _gdn_super_chunk(q, k, v, gcol, grow, bcol, glcol, st, rev):
    r = GDN_SUPER
    c = GDN_CHUNK
    ri = lax.broadcasted_iota(jnp.int32, (r, r), 0)
    ci = lax.broadcasted_iota(jnp.int32, (r, r), 1)
    same = (ri >> 6) == (ci >> 6)
    if rev:
        incl = same & (ri <= ci)
        strict = same & (ri < ci)
    else:
        incl = same & (ri >= ci)
        strict = same & (ri > ci)
    decay = jnp.exp(jnp.where(incl, gcol - grow, -jnp.inf))
    kb = k * bcol
    vb = v * bcol
    k16 = k.astype(BF16)
    m = jnp.where(strict, _dot_nt(kb.astype(BF16), k16) * decay, 0.0)
    qk = jnp.where(incl, _dot_nt(q.astype(BF16), k16) * decay, 0.0).astype(BF16)
    yield
    t = yield from _tri_inverse(m, ri, ci)
    eg = jnp.exp(gcol)
    rhs = jnp.concatenate([vb, kb * eg], axis=1).astype(BF16)
    uw = _dot(t.astype(BF16), rhs)
    u = uw[:, :HEAD_DIM]
    w = uw[:, HEAD_DIM:].astype(BF16)
    qd = (q * eg).astype(BF16)
    kt = (k * jnp.exp(glcol - gcol)).astype(BF16)
    yield
    outs = [None] * (r // c)
    order = range(r // c - 1, -1, -1) if rev else range(r // c)
    for i in order:
        sl = slice(i * c, (i + 1) * c)
        s16 = st.astype(BF16)
        vn = u[sl] - _dot(w[sl], s16)
        vn16 = vn.astype(BF16)
        yield
        pair = slice((i // 2) * 2 * c, (i // 2 + 1) * 2 * c)
        zero = jnp.zeros_like(vn16)
        vpad = jnp.concatenate([zero, vn16] if i % 2 else [vn16, zero], axis=0)
        outs[i] = _dot(qd[sl], s16) + _dot(qk[sl, pair], vpad)
        st = st * jnp.exp(glcol[i * c:i * c + 1, :]) + _dot_tn(kt[sl], vn16)
        yield
    return jnp.concatenate(outs, axis=0), st


def _gdn_core_body(q_ref, k_ref, v_ref, z_ref, row_ref, gn_ref, o_ref, col_s, of_s, ob_s):
    s_len = q_ref.shape[0]
    hd = HEAD_DIM
    heads = q_ref.shape[1] // hd
    r = GDN_SUPER
    n = s_len // r

    for hh in range(heads):
        rows8 = jnp.concatenate([row_ref[hh, i] for i in range(n)], axis=1)
        padded = jnp.concatenate([rows8, jnp.zeros((LANES - 8, s_len), F32)], axis=0)
        col_s[hh] = padded.T

    def one(hh, i, st, rev):
        base = 3 if rev else 0
        rows = pl.ds(pl.multiple_of(i * r, r), r)
        lanes = slice(hh * hd, (hh + 1) * hd)
        cols = col_s[hh, rows, :]
        gcol = cols[:, base:base + 1]
        bcol = cols[:, base + 1:base + 2]
        glcol = cols[:, base + 2:base + 3]
        grow = row_ref[hh, i][base:base + 1, :]
        return _gdn_super_chunk(q_ref[rows, lanes], k_ref[rows, lanes], v_ref[rows, lanes],
                                gcol, grow, bcol, glcol, st, rev)

    def step(t, carry):
        fwd_rows = pl.ds(pl.multiple_of(t * r, r), r)
        bwd_rows = pl.ds(pl.multiple_of((n - 1 - t) * r, r), r)
        gens = []
        for hh in range(heads):
            gens.append(one(hh, t, carry[2 * hh], False))
            gens.append(one(hh, n - 1 - t, carry[2 * hh + 1], True))
        res = _lockstep(gens)
        for hh in range(heads):
            lanes = slice(hh * hd, (hh + 1) * hd)
            of_s[fwd_rows, lanes] = res[2 * hh][0]
            ob_s[bwd_rows, lanes] = res[2 * hh + 1][0]
        return tuple(st for _, st in res)

    zero = jnp.zeros((hd, hd), F32)
    lax.fori_loop(0, n, step, (zero,) * (2 * heads))
    for hh in range(heads):
        lanes = slice(hh * hd, (hh + 1) * hd)
        o = of_s[:, lanes] + ob_s[:, lanes]
        y = o * lax.rsqrt(jnp.mean(o * o, axis=-1, keepdims=True) + EPS) * gn_ref[...]
        o_ref[:, lanes] = (y * jax.nn.silu(z_ref[:, lanes])).astype(o_ref.dtype)


def gdn_core(proj, gates, gdn_norm, b, s_len, *, heads):
    hd = HEAD_DIM
    n_super = s_len // GDN_SUPER
    gate_row = gates.reshape(b, GDN_HEADS, 8, n_super, GDN_SUPER).transpose(0, 1, 3, 2, 4)
    w = heads * hd
    ng = GDN_HEADS // heads
    first = 4 * RET_HEADS // heads
    qkv_blk = lambda off: pl.BlockSpec((None, s_len, w), lambda i, h: (i, 0, first + off + h))
    return pl.pallas_call(
        _gdn_core_body,
        out_shape=jax.ShapeDtypeStruct((b, s_len, GDN_HEADS * hd), BF16),
        grid=(b, ng),
        in_specs=[qkv_blk(0), qkv_blk(ng), qkv_blk(2 * ng),
                  pl.BlockSpec((None, s_len, w), lambda i, h: (i, 0, first + 3 * ng + h),
                               pipeline_mode=pl.Buffered(1)),
                  pl.BlockSpec((None, heads, n_super, 8, GDN_SUPER), lambda i, h: (i, h, 0, 0, 0)),
                  pl.BlockSpec((1, hd), lambda i, h: (0, 0))],
        out_specs=pl.BlockSpec((None, s_len, w), lambda i, h: (i, 0, h)),
        scratch_shapes=[pltpu.VMEM((heads, s_len, LANES), F32),
                        pltpu.VMEM((s_len, w), F32), pltpu.VMEM((s_len, w), F32)],
        compiler_params=_params(2),
        name="gdn_core",
    )(proj, proj, proj, proj, gate_row, gdn_norm.reshape(1, hd))


def _mixer_xattn_body(x_ref, a1_ref, a2_ref, w1_ref, w2_ref, gxa_ref, wq_ref, kv_ref, wo_ref,
                      gffn_ref, wpq_ref, o_ref, pq_ref):
    x = x_ref[...] + _dot(a1_ref[...], w1_ref[...]) + _dot(a2_ref[...], w2_ref[...])
    d = x.shape[1]
    dh = d // XA_HEADS
    q = _dot(_rms(x, gxa_ref[...]).astype(BF16), wq_ref[...])
    outs = []
    for h in range(XA_HEADS):
        kh = kv_ref[:, h * dh:(h + 1) * dh].astype(BF16)
        vh = kv_ref[:, d + h * dh:d + (h + 1) * dh].astype(BF16)
        sc = _dot_nt(q[:, h * dh:(h + 1) * dh].astype(BF16), kh) * (dh ** -0.5)
        p = jax.nn.softmax(sc, axis=-1)
        outs.append(_dot(p.astype(BF16), vh).astype(BF16))
    x = x + _dot(jnp.concatenate(outs, axis=1), wo_ref[...])
    o_ref[...] = x
    pq_ref[...] = _dot(_rms(x, gffn_ref[...]).astype(BF16), wpq_ref[...])


def mixer_xattn_peerq(x, a1, a2, w1, w2, g_xa, wq, kv, wo, g_ffn, wpq, b, s_len, *, ts):
    t, d = x.shape
    k1, k2 = a1.shape[1], a2.shape[1]
    m = kv.shape[1]
    npq = wpq.shape[1]
    nblk = s_len // ts
    tok = lambda n: pl.BlockSpec((ts, n), lambda i, j: (i * nblk + j, 0))
    whole = lambda r, c: pl.BlockSpec((r, c), lambda i, j: (0, 0))
    return pl.pallas_call(
        _mixer_xattn_body,
        out_shape=(jax.ShapeDtypeStruct((t, d), F32), jax.ShapeDtypeStruct((t, npq), F32)),
        grid=(b, nblk),
        in_specs=[tok(d), tok(k1), tok(k2), whole(k1, d), whole(k2, d),
                  whole(1, d), whole(d, d),
                  pl.BlockSpec((None, m, 2 * d), lambda i, j: (i, 0, 0)),
                  whole(d, d), whole(1, d), whole(d, npq)],
        out_specs=(tok(d), tok(npq)),
        compiler_params=_params(2),
        name="mixer_xattn_peerq",
    )(x, a1, a2, w1, w2, g_xa.reshape(1, d), wq, kv, wo, g_ffn.reshape(1, d), wpq)


def _extract_topk(s, k, break_ties):
    vals = []
    if break_ties:
        n_rows = s.shape[0]
        iota = lax.broadcasted_iota(jnp.int32, s.shape, 0).astype(F32)
        rank = jnp.full(s.shape, float(k), F32)
        for r in range(k):
            m = jnp.max(s, axis=0, keepdims=True)
            idx = jnp.min(jnp.where(s == m, iota, float(n_rows)), axis=0, keepdims=True)
            hit = iota == idx
            rank = jnp.where(hit, float(r), rank)
            s = jnp.where(hit, -jnp.inf, s)
            vals.append(m)
            yield
        taken = rank < float(k)
    else:
        for r in range(k):
            m = jnp.max(s, axis=0, keepdims=True)
            s = jnp.where(s == m, -(TAKEN + r * TAKEN_STEP), s)
            vals.append(m)
            yield
        taken = s <= -TAKEN
        rank = jnp.where(taken, s * (-1.0 / TAKEN_STEP) - TAKEN / TAKEN_STEP, float(k))
    count = jnp.sum(jnp.where(taken, 1.0, 0.0), axis=0, keepdims=True)
    return rank, vals, count


def _route_tokens(q, keys_ref, break_ties):
    nk = PEER_NKEYS
    kk = PEER_TOPK
    s1 = _dot_nt(keys_ref[0], q[:, :nk])
    s2 = _dot_nt(keys_ref[1], q[:, nk:])
    (r1, v1, n_sel1), (r2, v2, n_sel2) = yield from _lockstep_stages(
        [_extract_topk(s1, kk, break_ties), _extract_topk(s2, kk, break_ties)])
    v1m = jnp.concatenate(v1, axis=0)
    v2m = jnp.concatenate(v2, axis=0)
    pad = -jnp.inf if break_ties else FAST_PAD
    row8 = lax.broadcasted_iota(jnp.int32, (8, LANES), 0)
    groups = [v1[0] + v2m[0:8], v1[0] + v2m[8:16]]
    for a in range(1, 8):
        groups.append(jnp.where(row8 < kk // (a + 1), v1[a] + v2m[0:8], pad))
    groups.append(v1m[8:16] + v2[0])
    cand = jnp.concatenate(groups, axis=0)
    rc, _, n_selc = yield from _extract_topk(cand, kk, break_ties)
    sel = rc < float(kk)
    top = v1[0] + v2[0]
    z = jnp.sum(jnp.where(sel, jnp.exp(cand - top), 0.0), axis=0, keepdims=True)
    sel_f = jnp.where(sel, 1.0, 0.0)
    n1 = jnp.zeros((nk, LANES), F32)
    for a in range(kk):
        if a == 0:
            cnt = jnp.sum(sel_f[0:16], axis=0, keepdims=True)
        elif a < 8:
            cnt = jnp.sum(sel_f[8 + 8 * a:16 + 8 * a], axis=0, keepdims=True)
        else:
            cnt = sel_f[64 + a:65 + a]
        n1 = jnp.where(r1 == float(a), cnt, n1)
    c1 = jnp.where(r1 < float(kk), jnp.exp(s1 - v1[0]) / z, 0.0)
    e2 = jnp.exp(s2 - v2[0])
    exact = ((n_sel1 == float(kk)) & (n_sel2 == float(kk)) & (n_selc == float(kk))
             & (v1[kk - 1] > -FAST_SCORE_BOUND) & (v2[kk - 1] > -FAST_SCORE_BOUND))
    return c1, n1, e2, r2, jnp.where(exact, 1.0, 0.0)


def _peer_route_body(q_ref, keys_ref, c1_ref, n1_ref, e2_ref, r2_ref):
    tb = q_ref.shape[0]

    def run(break_ties):
        groups = [slice(blk * LANES, (blk + 1) * LANES) for blk in range(tb // LANES)]
        results = _lockstep([_route_tokens(q_ref[tok, :].astype(BF16), keys_ref, break_ties)
                             for tok in groups])
        flags = []
        for tok, (c1, n1, e2, r2, ok) in zip(groups, results):
            c1_ref[:, tok] = c1
            n1_ref[:, tok] = n1
            e2_ref[:, tok] = e2.astype(e2_ref.dtype)
            r2_ref[:, tok] = r2.astype(r2_ref.dtype)
            flags.append(ok)
        return jnp.min(jnp.concatenate(flags, axis=1))

    all_exact = run(False)

    @pl.when(all_exact < 0.5)
    def _():
        run(True)


def peer_route(q, keys, *, tb):
    t = q.shape[0]
    nk = PEER_NKEYS
    out = lambda dt: jax.ShapeDtypeStruct((PEER_HEADS, nk, t), dt)
    ospec = pl.BlockSpec((None, nk, tb), lambda i, h: (h, 0, i))
    return pl.pallas_call(
        _peer_route_body,
        out_shape=(out(F32), out(F32), out(BF16), out(BF16)),
        grid=(t // tb, PEER_HEADS),
        in_specs=[pl.BlockSpec((tb, 2 * nk), lambda i, h: (i, h)),
                  pl.BlockSpec((None, 2, nk, nk), lambda i, h: (h, 0, 0, 0))],
        out_specs=(ospec,) * 4,
        compiler_params=_params(2),
        name="peer_route",
    )(q, keys)


def _gelu_tanh(x):
    c0 = 0.7978845608028654
    c1 = 0.7978845608028654 * 0.044715
    return (0.5 * x) * (1.0 + jnp.tanh(x * (c0 + c1 * (x * x))))


def _peer_dense_body(x_ref, g_ref, c1_ref, n1_ref, e2_ref, r2_ref, u_ref, vt_ref, *rest, final):
    if final:
        gf_ref, o_ref, xn_s, acc_s = rest
    else:
        o_ref, xn_s, acc_s = rest
    j = pl.program_id(1)
    nk = PEER_NKEYS
    tb = x_ref.shape[0]
    grp = u_ref.shape[0] // nk
    sub = e2_ref.shape[2]

    @pl.when(j == 0)
    def _():
        xn_s[...] = _rms(x_ref[...], g_ref[...]).T.astype(BF16)
        acc_s[...] = jnp.zeros_like(acc_s)

    per = DENSE_CHUNK // nk
    n_chunks = grp // per
    d_model = vt_ref.shape[0]
    xn = xn_s[...]
    scores, gates, weights, chunk_w = {}, {}, {}, {}

    def score_pieces(c):
        def piece(ii):
            scores[ii] = _dot(u_ref[ii * nk:(ii + 1) * nk, :], xn)
        return [functools.partial(piece, c * per + i) for i in range(per)]

    def gate_pieces(c):
        def piece(ii, h):
            row = pl.ds(j * grp + ii, 1)
            c1 = jnp.broadcast_to(c1_ref[h, row, :], (sub, tb)).astype(e2_ref.dtype)[None]
            n1 = jnp.broadcast_to(n1_ref[h, row, :], (sub, tb)).astype(e2_ref.dtype)[None]
            e2 = e2_ref[h]
            term = jnp.where(r2_ref[h] < n1, e2, jnp.zeros_like(e2)) * c1
            gates[ii] = term if h == 0 else gates[ii] + term
        return [functools.partial(piece, c * per + i, h) for i in range(per) for h in range(PEER_HEADS)]

    def act_pieces(c):
        def piece(ii):
            act = _gelu_tanh(scores.pop(ii).astype(e2_ref.dtype))
            weights[ii] = act * gates.pop(ii).reshape(nk, tb)
        return [functools.partial(piece, c * per + i) for i in range(per)]

    def value_pieces(c):
        ex = slice(c * DENSE_CHUNK, (c + 1) * DENSE_CHUNK)

        def piece(m):
            if m == 0:
                chunk_w[c] = jnp.concatenate([weights.pop(c * per + i) for i in range(per)], axis=0)
            rows = slice(m * MXU_DEPTH, (m + 1) * MXU_DEPTH)
            acc_s[rows, :] += _dot(vt_ref[rows, ex], chunk_w[c])
        return [functools.partial(piece, m) for m in range(d_model // MXU_DEPTH)]

    _merge_run(score_pieces(0), gate_pieces(0))
    for c in range(n_chunks):
        mxu = (score_pieces(c + 1) if c + 1 < n_chunks else []) + (value_pieces(c - 1) if c else [])
        vpu = act_pieces(c) + (gate_pieces(c + 1) if c + 1 < n_chunks else [])
        _merge_run(mxu, vpu)
    _merge_run(value_pieces(n_chunks - 1), [])

    @pl.when(j == pl.num_programs(1) - 1)
    def _():
        y = x_ref[...] + acc_s[...].T
        if final:
            y = _rms(y, gf_ref[...])
        o_ref[...] = y


def peer_dense_residual(x, g, route, u, layer, vt, g_final, *, tb, grp, blocks=None):
    t, d = x.shape
    nk = PEER_NKEYS
    eb = grp * nk
    first, count = blocks if blocks is not None else (0, t // tb)
    c1, n1, e2, r2 = route
    sub = 32 // jnp.dtype(e2.dtype).itemsize
    e2 = e2.reshape(PEER_HEADS, nk // sub, sub, t)
    r2 = r2.reshape(PEER_HEADS, nk // sub, sub, t)
    rspec = pl.BlockSpec((PEER_HEADS, nk, tb), lambda i, j: (0, 0, first + i))
    pspec = pl.BlockSpec((PEER_HEADS, nk // sub, sub, tb), lambda i, j: (0, 0, 0, first + i))
    in_specs = [pl.BlockSpec((tb, d), lambda i, j: (first + i, 0)),
                pl.BlockSpec((1, d), lambda i, j: (0, 0)),
                rspec, rspec, pspec, pspec,
                pl.BlockSpec((None, eb, d), lambda i, j: (layer, j, 0)),
                pl.BlockSpec((d, eb), lambda i, j: (0, j))]
    args = [x, g.reshape(1, d), c1, n1, e2, r2, u, vt]
    final = g_final is not None
    if final:
        in_specs.append(pl.BlockSpec((1, d), lambda i, j: (0, 0)))
        args.append(g_final.reshape(1, d))
    return pl.pallas_call(
        functools.partial(_peer_dense_body, final=final),
        out_shape=jax.ShapeDtypeStruct((count * tb, d), F32),
        grid=(count, (nk * nk) // eb),
        in_specs=in_specs,
        out_specs=pl.BlockSpec((tb, d), lambda i, j: (i, 0)),
        scratch_shapes=[pltpu.VMEM((d, tb), BF16),
                        pltpu.VMEM((d, tb), F32)],
        compiler_params=_params(2),
        name="peer_dense_final" if final else "peer_dense",
    )(*args)


def _transpose_cast_body(x_ref, o_ref):
    o_ref[...] = x_ref[...].T.astype(o_ref.dtype)


def transpose_cast(tables, layer):
    _, rows, cols = tables.shape
    tr = _pick(rows, (1024, 512, 256, 128))
    return pl.pallas_call(
        _transpose_cast_body,
        out_shape=jax.ShapeDtypeStruct((cols, rows), BF16),
        grid=(rows // tr,),
        in_specs=[pl.BlockSpec((None, tr, cols), lambda i: (layer, i, 0))],
        out_specs=pl.BlockSpec((cols, tr), lambda i: (0, i)),
        compiler_params=_params(1),
        name="transpose_cast",
    )(tables)


def _pick(n, prefs):
    for p in prefs:
        if n % p == 0:
            return p
    return n


def _encoder(group_batches, x3, mem3, norm_mix, w_in, ret_norm, gdn_conv, gdn_a_log, gdn_dt_bias,
             gdn_norm, w_out, norm_xa, norm_mem, w_xq, w_xkv, w_xo, norm_ffn, peer_wq, peer_keys,
             peer_u, peer_v, norm_final):
    b, s_len, d = x3.shape
    n_mem = mem3.shape[1]
    t = b * s_len
    depth = w_in.shape[0]
    x = x3.reshape(t, d)
    mem = mem3.reshape(b * n_mem, d)
    n_main = 4 * RET_HEADS * HEAD_DIM + 4 * GDN_HEADS * HEAD_DIM
    tmm = _pick(b * n_mem, (1024, 512, 256, 128))
    ts = _pick(s_len, (512, 256, 128))
    tb_route = _pick(t, (256, 128))
    tb_dense = _pick(s_len, (512, 256, 128))
    peer_u16 = peer_u.astype(BF16)

    for l in range(depth):
        w_main = _deinterleave_rotary_columns(w_in[l, :, :n_main]).astype(BF16)
        w_gate_t = w_in[l, :, n_main:].T.astype(BF16)
        proj, graw = mixer_in_proj(x, norm_mix[l], w_main, w_gate_t, gdn_conv[l], s_len)
        proj3 = proj.reshape(b, s_len, n_main)
        o_r = retention(proj3, ret_norm[l], b, s_len, heads=RET_HEADS_PER_STEP)
        gates = gdn_gates(graw, gdn_a_log[l], gdn_dt_bias[l], b, s_len)
        o_g = gdn_core(proj3, gates, gdn_norm[l], b, s_len, heads=GDN_HEADS_PER_STEP)
        w_o = w_out[l].astype(BF16)
        n_r = RET_HEADS * HEAD_DIM

        kv = norm_matmul(mem, norm_mem[l], w_xkv[l].astype(BF16), tm=tmm, tn=1024)
        x, pq = mixer_xattn_peerq(x, o_r.reshape(t, -1), o_g.reshape(t, -1), w_o[:n_r], w_o[n_r:],
                                  norm_xa[l], w_xq[l].astype(BF16), kv.reshape(b, n_mem, 2 * d),
                                  w_xo[l].astype(BF16), norm_ffn[l], peer_wq[l].astype(BF16),
                                  b, s_len, ts=ts)

        route = peer_route(pq, peer_keys[l].astype(BF16), tb=tb_route)
        dense = functools.partial(peer_dense_residual, x, norm_ffn[l], route, peer_u16, l,
                                  transpose_cast(peer_v, l), tb=tb_dense, grp=16)
        if l < depth - 1:
            x = dense(None)
    outs, first = [], 0
    for nb in group_batches:
        n_blocks = nb * s_len // tb_dense
        outs.append(dense(norm_final, blocks=(first, n_blocks)).reshape(nb, s_len, d))
        first += n_blocks
    return tuple(outs)


def kernel(x_prompt, x_sample, mem_prompt, mem_sample, norm_mix, w_in, ret_norm, gdn_conv, gdn_a_log,
           gdn_dt_bias, gdn_norm, w_out, norm_xa, norm_mem, w_xq, w_xkv, w_xo, norm_ffn, peer_wq,
           peer_keys, peer_u, peer_v, norm_final):
    weights = (norm_mix, w_in, ret_norm, gdn_conv, gdn_a_log, gdn_dt_bias, gdn_norm, w_out,
               norm_xa, norm_mem, w_xq, w_xkv, w_xo, norm_ffn, peer_wq, peer_keys, peer_u, peer_v,
               norm_final)
    if x_prompt.shape[1:] == x_sample.shape[1:] and mem_prompt.shape[1:] == mem_sample.shape[1:]:
        return _encoder((x_prompt.shape[0], x_sample.shape[0]),
                        jnp.concatenate([x_prompt, x_sample], axis=0),
                        jnp.concatenate([mem_prompt, mem_sample], axis=0), *weights)
    return (_encoder((x_prompt.shape[0],), x_prompt, mem_prompt, *weights)[0],
            _encoder((x_sample.shape[0],), x_sample, mem_sample, *weights)[0])
```

```python
import functools

import jax
import jax.numpy as jnp
from jax import lax
from jax.experimental import pallas as pl
from jax.experimental.pallas import tpu as pltpu

F32 = jnp.float32
BF16 = jnp.bfloat16

EPS = 1e-6
LANES = 128
MXU_DEPTH = 256
DENSE_CHUNK = 1024
HEAD_DIM = 128
RET_HEADS = 4
GDN_HEADS = 4
RET_CHUNK = 128
GDN_CHUNK = 64
GDN_SUPER = 256
GDN_INV_BASE_LOG2 = 1
GDN_HEADS_PER_STEP = 4
RET_HEADS_PER_STEP = 4
ROPE_BASE = 10000.0
XA_HEADS = 4
PEER_HEADS = 8
PEER_NKEYS = 128
PEER_TOPK = 16
TAKEN = 2.0 ** 100
TAKEN_STEP = 2.0 ** 96
FAST_PAD = -(2.0 ** 90)
FAST_SCORE_BOUND = 2.0 ** 80
VMEM_LIMIT = 56 * 1024 * 1024

_NT = (((1,), (1,)), ((), ()))
_TN = (((0,), (0,)), ((), ()))


def _params(n_axes):
    return pltpu.CompilerParams(
        dimension_semantics=("arbitrary",) * n_axes, vmem_limit_bytes=VMEM_LIMIT)


def _dot(a, b):
    return jnp.dot(a, b, preferred_element_type=F32)


def _dot_nt(a, b):
    return lax.dot_general(a, b, _NT, preferred_element_type=F32)


def _dot_tn(a, b):
    return lax.dot_general(a, b, _TN, preferred_element_type=F32)


def _split(a):
    hi = a.astype(BF16)
    return hi, (a - hi.astype(F32)).astype(BF16)


def _dot3(a, b):
    (ah, al), (bh, bl) = a, b
    return _dot(jnp.concatenate([ah, ah, al], axis=1), jnp.concatenate([bh, bl, bh], axis=0))


def _rms(x, g):
    return x * lax.rsqrt(jnp.mean(x * x, axis=-1, keepdims=True) + EPS) * g


def _norm_matmul_body(x_ref, g_ref, w_ref, o_ref, xn_ref):
    @pl.when(pl.program_id(1) == 0)
    def _():
        xn_ref[...] = _rms(x_ref[...], g_ref[...]).astype(BF16)

    o_ref[...] = _dot(xn_ref[...], w_ref[...]).astype(o_ref.dtype)


def norm_matmul(x, g, w, *, tm, tn, out_dtype=F32):
    t, d = x.shape
    n = w.shape[1]
    return pl.pallas_call(
        _norm_matmul_body,
        out_shape=jax.ShapeDtypeStruct((t, n), out_dtype),
        grid=(t // tm, n // tn),
        in_specs=[pl.BlockSpec((tm, d), lambda i, j: (i, 0)),
                  pl.BlockSpec((1, d), lambda i, j: (0, 0)),
                  pl.BlockSpec((d, tn), lambda i, j: (0, j))],
        out_specs=pl.BlockSpec((tm, tn), lambda i, j: (i, j)),
        scratch_shapes=[pltpu.VMEM((tm, d), BF16)],
        compiler_params=_params(2),
        name="norm_matmul",
    )(x, g.reshape(1, d), w)


def _mixer_in_proj_body(x_ref, g_ref, w_ref, ws_ref, cos_ref, sin_ref, cw_ref, o_ref, os_ref, xn_ref):
    j = pl.program_id(1)
    hd = HEAD_DIM
    s_len = x_ref.shape[0]

    @pl.when(j == 0)
    def _():
        xn = _rms(x_ref[...], g_ref[...]).astype(BF16)
        xn_ref[...] = xn
        os_ref[...] = _dot_nt(ws_ref[...], xn)

    o_ref[...] = _dot(xn_ref[...], w_ref[...])

    @pl.when(j == 0)
    def _():
        cos = cos_ref[...]
        sin = sin_ref[...]
        for head in range(2 * RET_HEADS):
            cols = slice(head * hd, (head + 1) * hd)
            x = o_ref[:, cols]
            y = x * cos + pltpu.roll(x, hd // 2, 1) * sin
            o_ref[:, cols] = y * (hd ** -0.5) if head >= RET_HEADS else y

    def conv_silu(cols):
        x = o_ref[:, cols]
        t = lax.broadcasted_iota(jnp.int32, x.shape, 0)
        prev = jnp.where(t == 0, 0.0, pltpu.roll(x, 1, 0))
        nxt = jnp.where(t == s_len - 1, 0.0, pltpu.roll(x, s_len - 1, 0))
        return jax.nn.silu(prev * cw_ref[0:1, cols] + x * cw_ref[1:2, cols] + nxt * cw_ref[2:3, cols])

    @pl.when(j == 2)
    def _():
        for head in range(2 * GDN_HEADS):
            cols = slice(head * hd, (head + 1) * hd)
            act = conv_silu(cols)
            inv = lax.rsqrt(jnp.sum(act * act, axis=-1, keepdims=True) + EPS)
            o_ref[:, cols] = act * (inv * (hd ** -0.5) if head < GDN_HEADS else inv)

    @pl.when(j == 3)
    def _():
        for head in range(GDN_HEADS):
            cols = slice(head * hd, (head + 1) * hd)
            o_ref[:, cols] = conv_silu(cols)


def mixer_in_proj(x, g, w, ws_t, conv_w, s_len):
    t, d = x.shape
    n = w.shape[1]
    rows = ws_t.shape[0]
    tn = 2 * RET_HEADS * HEAD_DIM
    assert tn == 2 * GDN_HEADS * HEAD_DIM and n == 4 * tn
    cos, sin = _rotary_tables(s_len)
    conv_w = jnp.pad(conv_w, ((0, 0), (0, 2 * tn - conv_w.shape[1])))
    return pl.pallas_call(
        _mixer_in_proj_body,
        out_shape=(jax.ShapeDtypeStruct((t, n), F32),
                   jax.ShapeDtypeStruct((rows, t), F32)),
        grid=(t // s_len, n // tn),
        in_specs=[pl.BlockSpec((s_len, d), lambda i, j: (i, 0)),
                  pl.BlockSpec((1, d), lambda i, j: (0, 0)),
                  pl.BlockSpec((d, tn), lambda i, j: (0, j)),
                  pl.BlockSpec((rows, d), lambda i, j: (0, 0)),
                  pl.BlockSpec((s_len, HEAD_DIM), lambda i, j: (0, 0)),
                  pl.BlockSpec((s_len, HEAD_DIM), lambda i, j: (0, 0)),
                  pl.BlockSpec((3, tn), lambda i, j: (0, jnp.maximum(j - 2, 0)))],
        out_specs=(pl.BlockSpec((s_len, tn), lambda i, j: (i, j)),
                   pl.BlockSpec((rows, s_len), lambda i, j: (0, i))),
        scratch_shapes=[pltpu.VMEM((s_len, d), BF16)],
        compiler_params=_params(2),
        name="mixer_in_proj",
    )(x, g.reshape(1, d), w, ws_t, cos, sin, conv_w)


def _retention_body(qr_s, kr_s, v_ref, g_ref, dec_ref, rn_ref, o_ref, of_s, ob_s):
    s_len = qr_s.shape[0]
    hd = HEAD_DIM
    heads = qr_s.shape[1] // hd
    c = RET_CHUNK
    n = s_len // c

    def forward(hh, i, st):
        r = pl.ds(pl.multiple_of(i * c, c), c)
        lanes = slice(hh * hd, (hh + 1) * hd)
        qc, kc, vb = qr_s[r, lanes], kr_s[r, lanes], v_ref[r, lanes].astype(BF16)
        sc = _dot_nt(qc.astype(BF16), kc.astype(BF16)) * dec_ref[hh, 0]
        inter = _dot((qc * dec_ref[hh, 1]).astype(BF16), st.astype(BF16))
        st = st * dec_ref[hh, 5] + _dot_tn((kc * dec_ref[hh, 2]).astype(BF16), vb)
        yield
        return inter + _dot(sc.astype(BF16), vb), st

    def backward(hh, i, st):
        r = pl.ds(pl.multiple_of(i * c, c), c)
        lanes = slice(hh * hd, (hh + 1) * hd)
        qc, kc, vb = qr_s[r, lanes], kr_s[r, lanes], v_ref[r, lanes].astype(BF16)
        inter = _dot((qc * dec_ref[hh, 3]).astype(BF16), st.astype(BF16))
        st = st * dec_ref[hh, 6] + _dot_tn((kc * dec_ref[hh, 4]).astype(BF16), vb)
        yield
        return inter, st

    def step(t, carry):
        fwd_rows = pl.ds(pl.multiple_of(t * c, c), c)
        bwd_rows = pl.ds(pl.multiple_of((n - 1 - t) * c, c), c)
        gens = []
        for hh in range(heads):
            gens.append(forward(hh, t, carry[2 * hh]))
            gens.append(backward(hh, n - 1 - t, carry[2 * hh + 1]))
        res = _lockstep(gens)
        for hh in range(heads):
            lanes = slice(hh * hd, (hh + 1) * hd)
            of_s[fwd_rows, lanes] = res[2 * hh][0]
            ob_s[bwd_rows, lanes] = res[2 * hh + 1][0]
        return tuple(st for _, st in res)

    zero = jnp.zeros((hd, hd), F32)
    lax.fori_loop(0, n, step, (zero,) * (2 * heads))

    for hh in range(heads):
        lanes = slice(hh * hd, (hh + 1) * hd)
        o = of_s[:, lanes] + ob_s[:, lanes]
        mu = jnp.mean(o, axis=-1, keepdims=True)
        var = jnp.mean(jnp.square(o - mu), axis=-1, keepdims=True)
        y = (o - mu) * lax.rsqrt(var + EPS)
        y = y * rn_ref[:, lanes] * jax.nn.silu(g_ref[:, lanes])
        o_ref[:, lanes] = y.astype(o_ref.dtype)


def _deinterleave_rotary_columns(w):
    n_qk = 2 * RET_HEADS * HEAD_DIM
    qk = w[:, :n_qk].reshape(w.shape[0], 2 * RET_HEADS, HEAD_DIM // 2, 2)
    qk = jnp.swapaxes(qk, 2, 3).reshape(w.shape[0], n_qk)
    return jnp.concatenate([qk, w[:, n_qk:]], axis=1)


def _rotary_tables(s_len):
    d = HEAD_DIM
    inv = ROPE_BASE ** (-jnp.arange(0, d, 2, dtype=F32) / d)
    ang = jnp.arange(s_len, dtype=F32)[:, None] * inv[None, :]
    cos = jnp.concatenate([jnp.cos(ang), jnp.cos(ang)], axis=1)
    sin = jnp.concatenate([-jnp.sin(ang), jnp.sin(ang)], axis=1)
    return cos, sin


def _retention_tables():
    d = HEAD_DIM
    h = jnp.arange(RET_HEADS, dtype=F32)
    lg_f = jnp.log1p(-jnp.exp2(-5.0 - h))[:, None, None]
    lg_b = jnp.log1p(-jnp.exp2(-5.5 - h))[:, None, None]
    c = RET_CHUNK
    pos = jnp.arange(c, dtype=F32)
    diff = (pos[:, None] - pos[None, :])[None]
    dmat = jnp.where(diff >= 0, jnp.exp(lg_f * jnp.where(diff >= 0, diff, 0.0)),
                     jnp.exp(lg_b * jnp.where(diff < 0, -diff, 0.0)))
    col = lambda v: jnp.broadcast_to(v, (RET_HEADS, c, d))
    p = pos[None, :, None]
    dec = jnp.stack([
        dmat,
        col(jnp.exp(lg_f * (p + 1.0))),
        col(jnp.exp(lg_f * (c - 1.0 - p))),
        col(jnp.exp(lg_b * (c - p))),
        col(jnp.exp(lg_b * p)),
        col(jnp.exp(lg_f * c)),
        col(jnp.exp(lg_b * c)),
    ], axis=1)
    return dec


def retention(proj, ret_norm, b, s_len, *, heads):
    dec = _retention_tables()
    hd = HEAD_DIM
    w = heads * hd
    ng = RET_HEADS // heads
    head_blk = lambda off: pl.BlockSpec((None, s_len, w), lambda i, h: (i, 0, off + h))
    return pl.pallas_call(
        _retention_body,
        out_shape=jax.ShapeDtypeStruct((b, s_len, RET_HEADS * hd), BF16),
        grid=(b, ng),
        in_specs=[head_blk(0), head_blk(ng), head_blk(2 * ng), head_blk(3 * ng),
                  pl.BlockSpec((heads, 7, RET_CHUNK, hd), lambda i, h: (h, 0, 0, 0)),
                  pl.BlockSpec((1, w), lambda i, h: (0, h))],
        out_specs=pl.BlockSpec((None, s_len, w), lambda i, h: (i, 0, h)),
        scratch_shapes=[pltpu.VMEM((s_len, w), F32)] * 2,
        compiler_params=_params(2),
        name="retention",
    )(proj, proj, proj, proj, dec, ret_norm.reshape(1, -1))


def _gdn_gates_body(raw_ref, al_ref, dt_ref, o_ref):
    s_len = raw_ref.shape[-1]
    nh = 2 * GDN_HEADS
    gb = raw_ref[0:nh, :]
    ga = raw_ref[nh:2 * nh, :]
    beta = jax.nn.sigmoid(gb)
    g = -jnp.exp(al_ref[...]) * jax.nn.softplus(ga + dt_ref[...])
    pos = lax.broadcasted_iota(jnp.int32, (nh, s_len), 1) & (GDN_CHUNK - 1)
    fwd = g
    rev = g
    k = 1
    while k < GDN_CHUNK:
        fwd = fwd + jnp.where(pos >= k, pltpu.roll(fwd, k, 1), 0.0)
        rev = rev + jnp.where(pos < GDN_CHUNK - k, pltpu.roll(rev, s_len - k, 1), 0.0)
        k *= 2
    total = fwd + rev - g
    for h in range(GDN_HEADS):
        for d, cum in enumerate((fwd, rev)):
            r = d * GDN_HEADS + h
            o_ref[h, 3 * d:3 * d + 1, :] = cum[r:r + 1, :]
            o_ref[h, 3 * d + 1:3 * d + 2, :] = beta[r:r + 1, :]
            o_ref[h, 3 * d + 2:3 * d + 3, :] = total[r:r + 1, :]
        o_ref[h, 6:8, :] = jnp.zeros((2, s_len), F32)


def gdn_gates(raw, a_log, dt_bias, b, s_len):
    nh = 2 * GDN_HEADS
    return pl.pallas_call(
        _gdn_gates_body,
        out_shape=jax.ShapeDtypeStruct((b, GDN_HEADS, 8, s_len), F32),
        grid=(b,),
        in_specs=[pl.BlockSpec((2 * nh, s_len), lambda i: (0, i)),
                  pl.BlockSpec((nh, 1), lambda i: (0, 0)),
                  pl.BlockSpec((nh, 1), lambda i: (0, 0))],
        out_specs=pl.BlockSpec((None, GDN_HEADS, 8, s_len), lambda i: (i, 0, 0, 0)),
        compiler_params=_params(1),
        name="gdn_gates",
    )(raw, a_log.reshape(nh, 1), dt_bias.reshape(nh, 1))


def _lockstep_stages(gens):
    results = [None] * len(gens)
    live = list(range(len(gens)))
    while live:
        for i in list(live):
            try:
                next(gens[i])
            except StopIteration as done:
                results[i] = done.value
                live.remove(i)
        yield
    return results


def _lockstep(gens):
    stages = _lockstep_stages(gens)
    while True:
        try:
            next(stages)
        except StopIteration as done:
            return done.value


def _merge_run(a, b):
    ia = ib = 0
    while ia < len(a) or ib < len(b):
        if ib >= len(b) or (ia < len(a) and (ia + 1) * len(b) <= (ib + 1) * len(a)):
            a[ia]()
            ia += 1
        else:
            b[ib]()
            ib += 1


def _tri_inverse(m, ri, ci):
    def same_block(log2_size):
        return (ri >> log2_size) == (ci >> log2_size)

    base = GDN_INV_BASE_LOG2
    eye = (ri == ci).astype(F32)
    diag = jnp.where(same_block(base), -m, 0.0)
    nm = _split(diag)
    p = eye + diag
    for _ in range(base - 1):
        nm = _split(_dot3(nm, nm))
        yield
        p = p + _dot3(_split(p), nm)
        yield
    for size in range(base, GDN_CHUNK.bit_length() - 1):
        off = jnp.where(same_block(size + 1) & jnp.logical_not(same_block(size)), m, 0.0)
        p16 = p.astype(BF16)
        pc = _dot(p16, off.astype(BF16)).astype(BF16)
        yield
        p = p - _dot(pc, p16)
        yield
    return p


def _gdn_super_chunk(q, k, v, gcol, grow, bcol, glcol, st, rev):
    r = GDN_SUPER
    c = GDN_CHUNK
    ri = lax.broadcasted_iota(jnp.int32, (r, r), 0)
    ci = lax.broadcasted_iota(jnp.int32, (r, r), 1)
    same = (ri >> 6) == (ci >> 6)
    if rev:
        incl = same & (ri <= ci)
        strict = same & (ri < ci)
    else:
        incl = same & (ri >= ci)
        strict = same & (ri > ci)
    decay = jnp.exp(jnp.where(incl, gcol - grow, -jnp.inf))
    kb = k * bcol
    vb = v * bcol
    k16 = k.astype(BF16)
    m = jnp.where(strict, _dot_nt(kb.astype(BF16), k16) * decay, 0.0)
    qk = jnp.where(incl, _dot_nt(q.astype(BF16), k16) * decay, 0.0).astype(BF16)
    yield
    t = yield from _tri_inverse(m, ri, ci)
    eg = jnp.exp(gcol)
    rhs = jnp.concatenate([vb, kb * eg], axis=1).astype(BF16)
    uw = _dot(t.astype(BF16), rhs)
    u = uw[:, :HEAD_DIM]
    w = uw[:, HEAD_DIM:].astype(BF16)
    qd = (q * eg).astype(BF16)
    kt = (k * jnp.exp(glcol - gcol)).astype(BF16)
    yield
    outs = [None] * (r // c)
    order = range(r // c - 1, -1, -1) if rev else range(r // c)
    for i in order:
        sl = slice(i * c, (i + 1) * c)
        s16 = st.astype(BF16)
        vn = u[sl] - _dot(w[sl], s16)
        vn16 = vn.astype(BF16)
        yield
        pair = slice((i // 2) * 2 * c, (i // 2 + 1) * 2 * c)
        zero = jnp.zeros_like(vn16)
        vpad = jnp.concatenate([zero, vn16] if i % 2 else [vn16, zero], axis=0)
        outs[i] = _dot(qd[sl], s16) + _dot(qk[sl, pair], vpad)
        st = st * jnp.exp(glcol[i * c:i * c + 1, :]) + _dot_tn(kt[sl], vn16)
        yield
    return jnp.concatenate(outs, axis=0), st


def _gdn_core_body(q_ref, k_ref, v_ref, z_ref, row_ref, gn_ref, o_ref, col_s, of_s, ob_s):
    s_len = q_ref.shape[0]
    hd = HEAD_DIM
    heads = q_ref.shape[1] // hd
    r = GDN_SUPER
    n = s_len // r

    for hh in range(heads):
        rows8 = jnp.concatenate([row_ref[hh, i] for i in range(n)], axis=1)
        padded = jnp.concatenate([rows8, jnp.zeros((LANES - 8, s_len), F32)], axis=0)
        col_s[hh] = padded.T

    def one(hh, i, st, rev):
        base = 3 if rev else 0
        rows = pl.ds(pl.multiple_of(i * r, r), r)
        lanes = slice(hh * hd, (hh + 1) * hd)
        cols = col_s[hh, rows, :]
        gcol = cols[:, base:base + 1]
        bcol = cols[:, base + 1:base + 2]
        glcol = cols[:, base + 2:base + 3]
        grow = row_ref[hh, i][base:base + 1, :]
        return _gdn_super_chunk(q_ref[rows, lanes], k_ref[rows, lanes], v_ref[rows, lanes],
                                gcol, grow, bcol, glcol, st, rev)

    def step(t, carry):
        fwd_rows = pl.ds(pl.multiple_of(t * r, r), r)
        bwd_rows = pl.ds(pl.multiple_of((n - 1 - t) * r, r), r)
        gens = []
        for hh in range(heads):
            gens.append(one(hh, t, carry[2 * hh], False))
            gens.append(one(hh, n - 1 - t, carry[2 * hh + 1], True))
        res = _lockstep(gens)
        for hh in range(heads):
            lanes = slice(hh * hd, (hh + 1) * hd)
            of_s[fwd_rows, lanes] = res[2 * hh][0]
            ob_s[bwd_rows, lanes] = res[2 * hh + 1][0]
        return tuple(st for _, st in res)

    zero = jnp.zeros((hd, hd), F32)
    lax.fori_loop(0, n, step, (zero,) * (2 * heads))
    for hh in range(heads):
        lanes = slice(hh * hd, (hh + 1) * hd)
        o = of_s[:, lanes] + ob_s[:, lanes]
        y = o * lax.rsqrt(jnp.mean(o * o, axis=-1, keepdims=True) + EPS) * gn_ref[...]
        o_ref[:, lanes] = (y * jax.nn.silu(z_ref[:, lanes])).astype(o_ref.dtype)


def gdn_core(proj, gates, gdn_norm, b, s_len, *, heads):
    hd = HEAD_DIM
    n_super = s_len // GDN_SUPER
    gate_row = gates.reshape(b, GDN_HEADS, 8, n_super, GDN_SUPER).transpose(0, 1, 3, 2, 4)
    w = heads * hd
    ng = GDN_HEADS // heads
    first = 4 * RET_HEADS // heads
    qkv_blk = lambda off: pl.BlockSpec((None, s_len, w), lambda i, h: (i, 0, first + off + h))
    return pl.pallas_call(
        _gdn_core_body,
        out_shape=jax.ShapeDtypeStruct((b, s_len, GDN_HEADS * hd), BF16),
        grid=(b, ng),
        in_specs=[qkv_blk(0), qkv_blk(ng), qkv_blk(2 * ng),
                  pl.BlockSpec((None, s_len, w), lambda i, h: (i, 0, first + 3 * ng + h),
                               pipeline_mode=pl.Buffered(1)),
                  pl.BlockSpec((None, heads, n_super, 8, GDN_SUPER), lambda i, h: (i, h, 0, 0, 0)),
                  pl.BlockSpec((1, hd), lambda i, h: (0, 0))],
        out_specs=pl.BlockSpec((None, s_len, w), lambda i, h: (i, 0, h)),
        scratch_shapes=[pltpu.VMEM((heads, s_len, LANES), F32),
                        pltpu.VMEM((s_len, w), F32), pltpu.VMEM((s_len, w), F32)],
        compiler_params=_params(2),
        name="gdn_core",
    )(proj, proj, proj, proj, gate_row, gdn_norm.reshape(1, hd))


def _mixer_xattn_body(x_ref, a1_ref, a2_ref, w1_ref, w2_ref, gxa_ref, wq_ref, kv_ref, wo_ref,
                      gffn_ref, wpq_ref, o_ref, pq_ref):
    x = x_ref[...] + _dot(a1_ref[...], w1_ref[...]) + _dot(a2_ref[...], w2_ref[...])
    d = x.shape[1]
    dh = d // XA_HEADS
    q = _dot(_rms(x, gxa_ref[...]).astype(BF16), wq_ref[...])
    outs = []
    for h in range(XA_HEADS):
        kh = kv_ref[:, h * dh:(h + 1) * dh].astype(BF16)
        vh = kv_ref[:, d + h * dh:d + (h + 1) * dh].astype(BF16)
        sc = _dot_nt(q[:, h * dh:(h + 1) * dh].astype(BF16), kh) * (dh ** -0.5)
        p = jax.nn.softmax(sc, axis=-1)
        outs.append(_dot(p.astype(BF16), vh).astype(BF16))
    x = x + _dot(jnp.concatenate(outs, axis=1), wo_ref[...])
    o_ref[...] = x
    pq_ref[...] = _dot(_rms(x, gffn_ref[...]).astype(BF16), wpq_ref[...])


def mixer_xattn_peerq(x, a1, a2, w1, w2, g_xa, wq, kv, wo, g_ffn, wpq, b, s_len, *, ts):
    t, d = x.shape
    k1, k2 = a1.shape[1], a2.shape[1]
    m = kv.shape[1]
    npq = wpq.shape[1]
    nblk = s_len // ts
    tok = lambda n: pl.BlockSpec((ts, n), lambda i, j: (i * nblk + j, 0))
    whole = lambda r, c: pl.BlockSpec((r, c), lambda i, j: (0, 0))
    return pl.pallas_call(
        _mixer_xattn_body,
        out_shape=(jax.ShapeDtypeStruct((t, d), F32), jax.ShapeDtypeStruct((t, npq), F32)),
        grid=(b, nblk),
        in_specs=[tok(d), tok(k1), tok(k2), whole(k1, d), whole(k2, d),
                  whole(1, d), whole(d, d),
                  pl.BlockSpec((None, m, 2 * d), lambda i, j: (i, 0, 0)),
                  whole(d, d), whole(1, d), whole(d, npq)],
        out_specs=(tok(d), tok(npq)),
        compiler_params=_params(2),
        name="mixer_xattn_peerq",
    )(x, a1, a2, w1, w2, g_xa.reshape(1, d), wq, kv, wo, g_ffn.reshape(1, d), wpq)


def _extract_topk(s, k, break_ties):
    vals = []
    if break_ties:
        n_rows = s.shape[0]
        iota = lax.broadcasted_iota(jnp.int32, s.shape, 0).astype(F32)
        rank = jnp.full(s.shape, float(k), F32)
        for r in range(k):
            m = jnp.max(s, axis=0, keepdims=True)
            idx = jnp.min(jnp.where(s == m, iota, float(n_rows)), axis=0, keepdims=True)
            hit = iota == idx
            rank = jnp.where(hit, float(r), rank)
            s = jnp.where(hit, -jnp.inf, s)
            vals.append(m)
            yield
        taken = rank < float(k)
    else:
        for r in range(k):
            m = jnp.max(s, axis=0, keepdims=True)
            s = jnp.where(s == m, -(TAKEN + r * TAKEN_STEP), s)
            vals.append(m)
            yield
        taken = s <= -TAKEN
        rank = jnp.where(taken, s * (-1.0 / TAKEN_STEP) - TAKEN / TAKEN_STEP, float(k))
    count = jnp.sum(jnp.where(taken, 1.0, 0.0), axis=0, keepdims=True)
    return rank, vals, count


def _route_tokens(q, keys_ref, break_ties):
    nk = PEER_NKEYS
    kk = PEER_TOPK
    s1 = _dot_nt(keys_ref[0], q[:, :nk])
    s2 = _dot_nt(keys_ref[1], q[:, nk:])
    (r1, v1, n_sel1), (r2, v2, n_sel2) = yield from _lockstep_stages(
        [_extract_topk(s1, kk, break_ties), _extract_topk(s2, kk, break_ties)])
    v1m = jnp.concatenate(v1, axis=0)
    v2m = jnp.concatenate(v2, axis=0)
    pad = -jnp.inf if break_ties else FAST_PAD
    row8 = lax.broadcasted_iota(jnp.int32, (8, LANES), 0)
    groups = [v1[0] + v2m[0:8], v1[0] + v2m[8:16]]
    for a in range(1, 8):
        groups.append(jnp.where(row8 < kk // (a + 1), v1[a] + v2m[0:8], pad))
    groups.append(v1m[8:16] + v2[0])
    cand = jnp.concatenate(groups, axis=0)
    rc, _, n_selc = yield from _extract_topk(cand, kk, break_ties)
    sel = rc < float(kk)
    top = v1[0] + v2[0]
    z = jnp.sum(jnp.where(sel, jnp.exp(cand - top), 0.0), axis=0, keepdims=True)
    sel_f = jnp.where(sel, 1.0, 0.0)
    n1 = jnp.zeros((nk, LANES), F32)
    for a in range(kk):
        if a == 0:
            cnt = jnp.sum(sel_f[0:16], axis=0, keepdims=True)
        elif a < 8:
            cnt = jnp.sum(sel_f[8 + 8 * a:16 + 8 * a], axis=0, keepdims=True)
        else:
            cnt = sel_f[64 + a:65 + a]
        n1 = jnp.where(r1 == float(a), cnt, n1)
    c1 = jnp.where(r1 < float(kk), jnp.exp(s1 - v1[0]) / z, 0.0)
    e2 = jnp.exp(s2 - v2[0])
    exact = ((n_sel1 == float(kk)) & (n_sel2 == float(kk)) & (n_selc == float(kk))
             & (v1[kk - 1] > -FAST_SCORE_BOUND) & (v2[kk - 1] > -FAST_SCORE_BOUND))
    return c1, n1, e2, r2, jnp.where(exact, 1.0, 0.0)


def _peer_route_body(q_ref, keys_ref, c1_ref, n1_ref, e2_ref, r2_ref):
    tb = q_ref.shape[0]

    def run(break_ties):
        groups = [slice(blk * LANES, (blk + 1) * LANES) for blk in range(tb // LANES)]
        results = _lockstep([_route_tokens(q_ref[tok, :].astype(BF16), keys_ref, break_ties)
                             for tok in groups])
        flags = []
        for tok, (c1, n1, e2, r2, ok) in zip(groups, results):
            c1_ref[:, tok] = c1
            n1_ref[:, tok] = n1
            e2_ref[:, tok] = e2.astype(e2_ref.dtype)
            r2_ref[:, tok] = r2.astype(r2_ref.dtype)
            flags.append(ok)
        return jnp.min(jnp.concatenate(flags, axis=1))

    all_exact = run(False)

    @pl.when(all_exact < 0.5)
    def _():
        run(True)


def peer_route(q, keys, *, tb):
    t = q.shape[0]
    nk = PEER_NKEYS
    out = lambda dt: jax.ShapeDtypeStruct((PEER_HEADS, nk, t), dt)
    ospec = pl.BlockSpec((None, nk, tb), lambda i, h: (h, 0, i))
    return pl.pallas_call(
        _peer_route_body,
        out_shape=(out(F32), out(F32), out(BF16), out(BF16)),
        grid=(t // tb, PEER_HEADS),
        in_specs=[pl.BlockSpec((tb, 2 * nk), lambda i, h: (i, h)),
                  pl.BlockSpec((None, 2, nk, nk), lambda i, h: (h, 0, 0, 0))],
        out_specs=(ospec,) * 4,
        compiler_params=_params(2),
        name="peer_route",
    )(q, keys)


def _gelu_tanh(x):
    c0 = 0.7978845608028654
    c1 = 0.7978845608028654 * 0.044715
    return (0.5 * x) * (1.0 + jnp.tanh(x * (c0 + c1 * (x * x))))


def _peer_dense_body(x_ref, g_ref, c1_ref, n1_ref, e2_ref, r2_ref, u_ref, vt_ref, *rest, final):
    if final:
        gf_ref, o_ref, xn_s, acc_s = rest
    else:
        o_ref, xn_s, acc_s = rest
    j = pl.program_id(1)
    nk = PEER_NKEYS
    tb = x_ref.shape[0]
    grp = u_ref.shape[0] // nk
    sub = e2_ref.shape[2]

    @pl.when(j == 0)
    def _():
        xn_s[...] = _rms(x_ref[...], g_ref[...]).T.astype(BF16)
        acc_s[...] = jnp.zeros_like(acc_s)

    per = DENSE_CHUNK // nk
    n_chunks = grp // per
    d_model = vt_ref.shape[0]
    xn = xn_s[...]
    scores, gates, weights, chunk_w = {}, {}, {}, {}

    def score_pieces(c):
        def piece(ii):
            scores[ii] = _dot(u_ref[ii * nk:(ii + 1) * nk, :], xn)
        return [functools.partial(piece, c * per + i) for i in range(per)]

    def gate_pieces(c):
        def piece(ii, h):
            row = pl.ds(j * grp + ii, 1)
            c1 = jnp.broadcast_to(c1_ref[h, row, :], (sub, tb)).astype(e2_ref.dtype)[None]
            n1 = jnp.broadcast_to(n1_ref[h, row, :], (sub, tb)).astype(e2_ref.dtype)[None]
            e2 = e2_ref[h]
            term = jnp.where(r2_ref[h] < n1, e2, jnp.zeros_like(e2)) * c1
            gates[ii] = term if h == 0 else gates[ii] + term
        return [functools.partial(piece, c * per + i, h) for i in range(per) for h in range(PEER_HEADS)]

    def act_pieces(c):
        def piece(ii):
            act = _gelu_tanh(scores.pop(ii).astype(e2_ref.dtype))
            weights[ii] = act * gates.pop(ii).reshape(nk, tb)
        return [functools.partial(piece, c * per + i) for i in range(per)]

    def value_pieces(c):
        ex = slice(c * DENSE_CHUNK, (c + 1) * DENSE_CHUNK)

        def piece(m):
            if m == 0:
                chunk_w[c] = jnp.concatenate([weights.pop(c * per + i) for i in range(per)], axis=0)
            rows = slice(m * MXU_DEPTH, (m + 1) * MXU_DEPTH)
            acc_s[rows, :] += _dot(vt_ref[rows, ex], chunk_w[c])
        return [functools.partial(piece, m) for m in range(d_model // MXU_DEPTH)]

    _merge_run(score_pieces(0), gate_pieces(0))
    for c in range(n_chunks):
        mxu = (score_pieces(c + 1) if c + 1 < n_chunks else []) + (value_pieces(c - 1) if c else [])
        vpu = act_pieces(c) + (gate_pieces(c + 1) if c + 1 < n_chunks else [])
        _merge_run(mxu, vpu)
    _merge_run(value_pieces(n_chunks - 1), [])

    @pl.when(j == pl.num_programs(1) - 1)
    def _():
        y = x_ref[...] + acc_s[...].T
        if final:
            y = _rms(y, gf_ref[...])
        o_ref[...] = y


def peer_dense_residual(x, g, route, u, layer, vt, g_final, *, tb, grp, blocks=None):
    t, d = x.shape
    nk = PEER_NKEYS
    eb = grp * nk
    first, count = blocks if blocks is not None else (0, t // tb)
    c1, n1, e2, r2 = route
    sub = 32 // jnp.dtype(e2.dtype).itemsize
    e2 = e2.reshape(PEER_HEADS, nk // sub, sub, t)
    r2 = r2.reshape(PEER_HEADS, nk // sub, sub, t)
    rspec = pl.BlockSpec((PEER_HEADS, nk, tb), lambda i, j: (0, 0, first + i))
    pspec = pl.BlockSpec((PEER_HEADS, nk // sub, sub, tb), lambda i, j: (0, 0, 0, first + i))
    in_specs = [pl.BlockSpec((tb, d), lambda i, j: (first + i, 0)),
                pl.BlockSpec((1, d), lambda i, j: (0, 0)),
                rspec, rspec, pspec, pspec,
                pl.BlockSpec((None, eb, d), lambda i, j: (layer, j, 0)),
                pl.BlockSpec((d, eb), lambda i, j: (0, j))]
    args = [x, g.reshape(1, d), c1, n1, e2, r2, u, vt]
    final = g_final is not None
    if final:
        in_specs.append(pl.BlockSpec((1, d), lambda i, j: (0, 0)))
        args.append(g_final.reshape(1, d))
    return pl.pallas_call(
        functools.partial(_peer_dense_body, final=final),
        out_shape=jax.ShapeDtypeStruct((count * tb, d), F32),
        grid=(count, (nk * nk) // eb),
        in_specs=in_specs,
        out_specs=pl.BlockSpec((tb, d), lambda i, j: (i, 0)),
        scratch_shapes=[pltpu.VMEM((d, tb), BF16),
                        pltpu.VMEM((d, tb), F32)],
        compiler_params=_params(2),
        name="peer_dense_final" if final else "peer_dense",
    )(*args)


def _transpose_cast_body(x_ref, o_ref):
    o_ref[...] = x_ref[...].T.astype(o_ref.dtype)


def transpose_cast(tables, layer):
    _, rows, cols = tables.shape
    tr = _pick(rows, (1024, 512, 256, 128))
    return pl.pallas_call(
        _transpose_cast_body,
        out_shape=jax.ShapeDtypeStruct((cols, rows), BF16),
        grid=(rows // tr,),
        in_specs=[pl.BlockSpec((None, tr, cols), lambda i: (layer, i, 0))],
        out_specs=pl.BlockSpec((cols, tr), lambda i: (0, i)),
        compiler_params=_params(1),
        name="transpose_cast",
    )(tables)


def _pick(n, prefs):
    for p in prefs:
        if n % p == 0:
            return p
    return n


def _encoder(group_batches, x3, mem3, norm_mix, w_in, ret_norm, gdn_conv, gdn_a_log, gdn_dt_bias,
             gdn_norm, w_out, norm_xa, norm_mem, w_xq, w_xkv, w_xo, norm_ffn, peer_wq, peer_keys,
             peer_u, peer_v, norm_final):
    b, s_len, d = x3.shape
    n_mem = mem3.shape[1]
    t = b * s_len
    depth = w_in.shape[0]
    x = x3.reshape(t, d)
    mem = mem3.reshape(b * n_mem, d)
    n_main = 4 * RET_HEADS * HEAD_DIM + 4 * GDN_HEADS * HEAD_DIM
    tmm = _pick(b * n_mem, (1024, 512, 256, 128))
    ts = _pick(s_len, (512, 256, 128))
    tb_route = _pick(t, (256, 128))
    tb_dense = _pick(s_len, (512, 256, 128))
    peer_u16 = peer_u.astype(BF16)

    for l in range(depth):
        w_main = _deinterleave_rotary_columns(w_in[l, :, :n_main]).astype(BF16)
        w_gate_t = w_in[l, :, n_main:].T.astype(BF16)
        proj, graw = mixer_in_proj(x, norm_mix[l], w_main, w_gate_t, gdn_conv[l], s_len)
        proj3 = proj.reshape(b, s_len, n_main)
        o_r = retention(proj3, ret_norm[l], b, s_len, heads=RET_HEADS_PER_STEP)
        gates = gdn_gates(graw, gdn_a_log[l], gdn_dt_bias[l], b, s_len)
        o_g = gdn_core(proj3, gates, gdn_norm[l], b, s_len, heads=GDN_HEADS_PER_STEP)
        w_o = w_out[l].astype(BF16)
        n_r = RET_HEADS * HEAD_DIM

        kv = norm_matmul(mem, norm_mem[l], w_xkv[l].astype(BF16), tm=tmm, tn=1024)
        x, pq = mixer_xattn_peerq(x, o_r.reshape(t, -1), o_g.reshape(t, -1), w_o[:n_r], w_o[n_r:],
                                  norm_xa[l], w_xq[l].astype(BF16), kv.reshape(b, n_mem, 2 * d),
                                  w_xo[l].astype(BF16), norm_ffn[l], peer_wq[l].astype(BF16),
                                  b, s_len, ts=ts)

        route = peer_route(pq, peer_keys[l].astype(BF16), tb=tb_route)
        dense = functools.partial(peer_dense_residual, x, norm_ffn[l], route, peer_u16, l,
                                  transpose_cast(peer_v, l), tb=tb_dense, grp=16)
        if l < depth - 1:
            x = dense(None)
    outs, first = [], 0
    for nb in group_batches:
        n_blocks = nb * s_len // tb_dense
        outs.append(dense(norm_final, blocks=(first, n_blocks)).reshape(nb, s_len, d))
        first += n_blocks
    return tuple(outs)


def kernel(x_prompt, x_sample, mem_prompt, mem_sample, norm_mix, w_in, ret_norm, gdn_conv, gdn_a_log,
           gdn_dt_bias, gdn_norm, w_out, norm_xa, norm_mem, w_xq, w_xkv, w_xo, norm_ffn, peer_wq,
           peer_keys, peer_u, peer_v, norm_final):
    weights = (norm_mix, w_in, ret_norm, gdn_conv, gdn_a_log, gdn_dt_bias, gdn_norm, w_out,
               norm_xa, norm_mem, w_xq, w_xkv, w_xo, norm_ffn, peer_wq, peer_keys, peer_u, peer_v,
               norm_final)
    if x_prompt.shape[1:] == x_sample.shape[1:] and mem_prompt.shape[1:] == mem_sample.shape[1:]:
        return _encoder((x_prompt.shape[0], x_sample.shape[0]),
                        jnp.concatenate([x_prompt, x_sample], axis=0),
                        jnp.concatenate([mem_prompt, mem_sample], axis=0), *weights)
    return (_encoder((x_prompt.shape[0],), x_prompt, mem_prompt, *weights)[0],
            _encoder((x_sample.shape[0],), x_sample, mem_sample, *weights)[0])
```

```python
import functools

import jax
import jax.numpy as jnp
from jax import lax
from jax.experimental import pallas as pl
from jax.experimental.pallas import tpu as pltpu

F32 = jnp.float32
BF16 = jnp.bfloat16

EPS = 1e-6
LANES = 128
MXU_DEPTH = 256
DENSE_CHUNK = 512
HEAD_DIM = 128
RET_HEADS = 4
GDN_HEADS = 4
RET_CHUNK = 128
GDN_CHUNK = 64
GDN_SUPER = 256
GDN_INV_BASE_LOG2 = 1
GDN_HEADS_PER_STEP = 4
RET_HEADS_PER_STEP = 4
ROPE_BASE = 10000.0
XA_HEADS = 4
PEER_HEADS = 8
PEER_NKEYS = 128
PEER_TOPK = 16
TAKEN = 2.0 ** 100
TAKEN_STEP = 2.0 ** 96
FAST_PAD = -(2.0 ** 90)
FAST_SCORE_BOUND = 2.0 ** 80
VMEM_LIMIT = 56 * 1024 * 1024

_NT = (((1,), (1,)), ((), ()))
_TN = (((0,), (0,)), ((), ()))


def _params(n_axes):
    return pltpu.CompilerParams(
        dimension_semantics=("arbitrary",) * n_axes, vmem_limit_bytes=VMEM_LIMIT)


def _dot(a, b):
    return jnp.dot(a, b, preferred_element_type=F32)


def _dot_nt(a, b):
    return lax.dot_general(a, b, _NT, preferred_element_type=F32)


def _dot_tn(a, b):
    return lax.dot_general(a, b, _TN, preferred_element_type=F32)


def _split(a):
    hi = a.astype(BF16)
    return hi, (a - hi.astype(F32)).astype(BF16)


def _dot3(a, b):
    (ah, al), (bh, bl) = a, b
    return _dot(jnp.concatenate([ah, ah, al], axis=1), jnp.concatenate([bh, bl, bh], axis=0))


def _rms(x, g):
    return x * lax.rsqrt(jnp.mean(x * x, axis=-1, keepdims=True) + EPS) * g


def _norm_matmul_body(x_ref, g_ref, w_ref, o_ref, xn_ref):
    @pl.when(pl.program_id(1) == 0)
    def _():
        xn_ref[...] = _rms(x_ref[...], g_ref[...]).astype(BF16)

    o_ref[...] = _dot(xn_ref[...], w_ref[...]).astype(o_ref.dtype)


def norm_matmul(x, g, w, *, tm, tn, out_dtype=F32):
    t, d = x.shape
    n = w.shape[1]
    return pl.pallas_call(
        _norm_matmul_body,
        out_shape=jax.ShapeDtypeStruct((t, n), out_dtype),
        grid=(t // tm, n // tn),
        in_specs=[pl.BlockSpec((tm, d), lambda i, j: (i, 0)),
                  pl.BlockSpec((1, d), lambda i, j: (0, 0)),
                  pl.BlockSpec((d, tn), lambda i, j: (0, j))],
        out_specs=pl.BlockSpec((tm, tn), lambda i, j: (i, j)),
        scratch_shapes=[pltpu.VMEM((tm, d), BF16)],
        compiler_params=_params(2),
        name="norm_matmul",
    )(x, g.reshape(1, d), w)


def _mixer_in_proj_body(x_ref, g_ref, w_ref, ws_ref, cos_ref, sin_ref, cw_ref, o_ref, os_ref, xn_ref):
    j = pl.program_id(1)
    hd = HEAD_DIM
    s_len = x_ref.shape[0]

    @pl.when(j == 0)
    def _():
        xn = _rms(x_ref[...], g_ref[...]).astype(BF16)
        xn_ref[...] = xn
        os_ref[...] = _dot_nt(ws_ref[...], xn)

    o_ref[...] = _dot(xn_ref[...], w_ref[...])

    @pl.when(j == 0)
    def _():
        cos = cos_ref[...]
        sin = sin_ref[...]
        for head in range(2 * RET_HEADS):
            cols = slice(head * hd, (head + 1) * hd)
            x = o_ref[:, cols]
            y = x * cos + pltpu.roll(x, hd // 2, 1) * sin
            o_ref[:, cols] = y * (hd ** -0.5) if head >= RET_HEADS else y

    def conv_silu(cols):
        x = o_ref[:, cols]
        t = lax.broadcasted_iota(jnp.int32, x.shape, 0)
        prev = jnp.where(t == 0, 0.0, pltpu.roll(x, 1, 0))
        nxt = jnp.where(t == s_len - 1, 0.0, pltpu.roll(x, s_len - 1, 0))
        return jax.nn.silu(prev * cw_ref[0:1, cols] + x * cw_ref[1:2, cols] + nxt * cw_ref[2:3, cols])

    @pl.when(j == 2)
    def _():
        for head in range(2 * GDN_HEADS):
            cols = slice(head * hd, (head + 1) * hd)
            act = conv_silu(cols)
            inv = lax.rsqrt(jnp.sum(act * act, axis=-1, keepdims=True) + EPS)
            o_ref[:, cols] = act * (inv * (hd ** -0.5) if head < GDN_HEADS else inv)

    @pl.when(j == 3)
    def _():
        for head in range(GDN_HEADS):
            cols = slice(head * hd, (head + 1) * hd)
            o_ref[:, cols] = conv_silu(cols)


def mixer_in_proj(x, g, w, ws_t, conv_w, s_len):
    t, d = x.shape
    n = w.shape[1]
    rows = ws_t.shape[0]
    tn = 2 * RET_HEADS * HEAD_DIM
    assert tn == 2 * GDN_HEADS * HEAD_DIM and n == 4 * tn
    cos, sin = _rotary_tables(s_len)
    conv_w = jnp.pad(conv_w, ((0, 0), (0, 2 * tn - conv_w.shape[1])))
    return pl.pallas_call(
        _mixer_in_proj_body,
        out_shape=(jax.ShapeDtypeStruct((t, n), F32),
                   jax.ShapeDtypeStruct((rows, t), F32)),
        grid=(t // s_len, n // tn),
        in_specs=[pl.BlockSpec((s_len, d), lambda i, j: (i, 0)),
                  pl.BlockSpec((1, d), lambda i, j: (0, 0)),
                  pl.BlockSpec((d, tn), lambda i, j: (0, j)),
                  pl.BlockSpec((rows, d), lambda i, j: (0, 0)),
                  pl.BlockSpec((s_len, HEAD_DIM), lambda i, j: (0, 0)),
                  pl.BlockSpec((s_len, HEAD_DIM), lambda i, j: (0, 0)),
                  pl.BlockSpec((3, tn), lambda i, j: (0, jnp.maximum(j - 2, 0)))],
        out_specs=(pl.BlockSpec((s_len, tn), lambda i, j: (i, j)),
                   pl.BlockSpec((rows, s_len), lambda i, j: (0, i))),
        scratch_shapes=[pltpu.VMEM((s_len, d), BF16)],
        compiler_params=_params(2),
        name="mixer_in_proj",
    )(x, g.reshape(1, d), w, ws_t, cos, sin, conv_w)


def _retention_body(qr_s, kr_s, v_ref, g_ref, dec_ref, rn_ref, o_ref, of_s, ob_s):
    s_len = qr_s.shape[0]
    hd = HEAD_DIM
    heads = qr_s.shape[1] // hd
    c = RET_CHUNK
    n = s_len // c

    def forward(hh, i, st):
        r = pl.ds(pl.multiple_of(i * c, c), c)
        lanes = slice(hh * hd, (hh + 1) * hd)
        qc, kc, vb = qr_s[r, lanes], kr_s[r, lanes], v_ref[r, lanes].astype(BF16)
        sc = _dot_nt(qc.astype(BF16), kc.astype(BF16)) * dec_ref[hh, 0]
        inter = _dot((qc * dec_ref[hh, 1]).astype(BF16), st.astype(BF16))
        st = st * dec_ref[hh, 5] + _dot_tn((kc * dec_ref[hh, 2]).astype(BF16), vb)
        yield
        return inter + _dot(sc.astype(BF16), vb), st

    def backward(hh, i, st):
        r = pl.ds(pl.multiple_of(i * c, c), c)
        lanes = slice(hh * hd, (hh + 1) * hd)
        qc, kc, vb = qr_s[r, lanes], kr_s[r, lanes], v_ref[r, lanes].astype(BF16)
        inter = _dot((qc * dec_ref[hh, 3]).astype(BF16), st.astype(BF16))
        st = st * dec_ref[hh, 6] + _dot_tn((kc * dec_ref[hh, 4]).astype(BF16), vb)
        yield
        return inter, st

    def step(t, carry):
        fwd_rows = pl.ds(pl.multiple_of(t * c, c), c)
        bwd_rows = pl.ds(pl.multiple_of((n - 1 - t) * c, c), c)
        gens = []
        for hh in range(heads):
            gens.append(forward(hh, t, carry[2 * hh]))
            gens.append(backward(hh, n - 1 - t, carry[2 * hh + 1]))
        res = _lockstep(gens)
        for hh in range(heads):
            lanes = slice(hh * hd, (hh + 1) * hd)
            of_s[fwd_rows, lanes] = res[2 * hh][0]
            ob_s[bwd_rows, lanes] = res[2 * hh + 1][0]
        return tuple(st for _, st in res)

    zero = jnp.zeros((hd, hd), F32)
    lax.fori_loop(0, n, step, (zero,) * (2 * heads))

    for hh in range(heads):
        lanes = slice(hh * hd, (hh + 1) * hd)
        o = of_s[:, lanes] + ob_s[:, lanes]
        mu = jnp.mean(o, axis=-1, keepdims=True)
        var = jnp.mean(jnp.square(o - mu), axis=-1, keepdims=True)
        y = (o - mu) * lax.rsqrt(var + EPS)
        y = y * rn_ref[:, lanes] * jax.nn.silu(g_ref[:, lanes])
        o_ref[:, lanes] = y.astype(o_ref.dtype)


def _deinterleave_rotary_columns(w):
    n_qk = 2 * RET_HEADS * HEAD_DIM
    qk = w[:, :n_qk].reshape(w.shape[0], 2 * RET_HEADS, HEAD_DIM // 2, 2)
    qk = jnp.swapaxes(qk, 2, 3).reshape(w.shape[0], n_qk)
    return jnp.concatenate([qk, w[:, n_qk:]], axis=1)


def _rotary_tables(s_len):
    d = HEAD_DIM
    inv = ROPE_BASE ** (-jnp.arange(0, d, 2, dtype=F32) / d)
    ang = jnp.arange(s_len, dtype=F32)[:, None] * inv[None, :]
    cos = jnp.concatenate([jnp.cos(ang), jnp.cos(ang)], axis=1)
    sin = jnp.concatenate([-jnp.sin(ang), jnp.sin(ang)], axis=1)
    return cos, sin


def _retention_tables():
    d = HEAD_DIM
    h = jnp.arange(RET_HEADS, dtype=F32)
    lg_f = jnp.log1p(-jnp.exp2(-5.0 - h))[:, None, None]
    lg_b = jnp.log1p(-jnp.exp2(-5.5 - h))[:, None, None]
    c = RET_CHUNK
    pos = jnp.arange(c, dtype=F32)
    diff = (pos[:, None] - pos[None, :])[None]
    dmat = jnp.where(diff >= 0, jnp.exp(lg_f * jnp.where(diff >= 0, diff, 0.0)),
                     jnp.exp(lg_b * jnp.where(diff < 0, -diff, 0.0)))
    col = lambda v: jnp.broadcast_to(v, (RET_HEADS, c, d))
    p = pos[None, :, None]
    dec = jnp.stack([
        dmat,
        col(jnp.exp(lg_f * (p + 1.0))),
        col(jnp.exp(lg_f * (c - 1.0 - p))),
        col(jnp.exp(lg_b * (c - p))),
        col(jnp.exp(lg_b * p)),
        col(jnp.exp(lg_f * c)),
        col(jnp.exp(lg_b * c)),
    ], axis=1)
    return dec


def retention(proj, ret_norm, b, s_len, *, heads):
    dec = _retention_tables()
    hd = HEAD_DIM
    w = heads * hd
    ng = RET_HEADS // heads
    head_blk = lambda off: pl.BlockSpec((None, s_len, w), lambda i, h: (i, 0, off + h))
    return pl.pallas_call(
        _retention_body,
        out_shape=jax.ShapeDtypeStruct((b, s_len, RET_HEADS * hd), BF16),
        grid=(b, ng),
        in_specs=[head_blk(0), head_blk(ng), head_blk(2 * ng), head_blk(3 * ng),
                  pl.BlockSpec((heads, 7, RET_CHUNK, hd), lambda i, h: (h, 0, 0, 0)),
                  pl.BlockSpec((1, w), lambda i, h: (0, h))],
        out_specs=pl.BlockSpec((None, s_len, w), lambda i, h: (i, 0, h)),
        scratch_shapes=[pltpu.VMEM((s_len, w), F32)] * 2,
        compiler_params=_params(2),
        name="retention",
    )(proj, proj, proj, proj, dec, ret_norm.reshape(1, -1))


def _gdn_gates_body(raw_ref, al_ref, dt_ref, o_ref):
    s_len = raw_ref.shape[-1]
    nh = 2 * GDN_HEADS
    gb = raw_ref[0:nh, :]
    ga = raw_ref[nh:2 * nh, :]
    beta = jax.nn.sigmoid(gb)
    g = -jnp.exp(al_ref[...]) * jax.nn.softplus(ga + dt_ref[...])
    pos = lax.broadcasted_iota(jnp.int32, (nh, s_len), 1) & (GDN_CHUNK - 1)
    fwd = g
    rev = g
    k = 1
    while k < GDN_CHUNK:
        fwd = fwd + jnp.where(pos >= k, pltpu.roll(fwd, k, 1), 0.0)
        rev = rev + jnp.where(pos < GDN_CHUNK - k, pltpu.roll(rev, s_len - k, 1), 0.0)
        k *= 2
    total = fwd + rev - g
    for h in range(GDN_HEADS):
        for d, cum in enumerate((fwd, rev)):
            r = d * GDN_HEADS + h
            o_ref[h, 3 * d:3 * d + 1, :] = cum[r:r + 1, :]
            o_ref[h, 3 * d + 1:3 * d + 2, :] = beta[r:r + 1, :]
            o_ref[h, 3 * d + 2:3 * d + 3, :] = total[r:r + 1, :]
        o_ref[h, 6:8, :] = jnp.zeros((2, s_len), F32)


def gdn_gates(raw, a_log, dt_bias, b, s_len):
    nh = 2 * GDN_HEADS
    return pl.pallas_call(
        _gdn_gates_body,
        out_shape=jax.ShapeDtypeStruct((b, GDN_HEADS, 8, s_len), F32),
        grid=(b,),
        in_specs=[pl.BlockSpec((2 * nh, s_len), lambda i: (0, i)),
                  pl.BlockSpec((nh, 1), lambda i: (0, 0)),
                  pl.BlockSpec((nh, 1), lambda i: (0, 0))],
        out_specs=pl.BlockSpec((None, GDN_HEADS, 8, s_len), lambda i: (i, 0, 0, 0)),
        compiler_params=_params(1),
        name="gdn_gates",
    )(raw, a_log.reshape(nh, 1), dt_bias.reshape(nh, 1))


def _lockstep_stages(gens):
    results = [None] * len(gens)
    live = list(range(len(gens)))
    while live:
        for i in list(live):
            try:
                next(gens[i])
            except StopIteration as done:
                results[i] = done.value
                live.remove(i)
        yield
    return results


def _lockstep(gens):
    stages = _lockstep_stages(gens)
    while True:
        try:
            next(stages)
        except StopIteration as done:
            return done.value


def _merge_run(a, b):
    ia = ib = 0
    while ia < len(a) or ib < len(b):
        if ib >= len(b) or (ia < len(a) and (ia + 1) * len(b) <= (ib + 1) * len(a)):
            a[ia]()
            ia += 1
        else:
            b[ib]()
            ib += 1


def _tri_inverse(m, ri, ci):
    def same_block(log2_size):
        return (ri >> log2_size) == (ci >> log2_size)

    base = GDN_INV_BASE_LOG2
    eye = (ri == ci).astype(F32)
    diag = jnp.where(same_block(base), -m, 0.0)
    nm = _split(diag)
    p = eye + diag
    for _ in range(base - 1):
        nm = _split(_dot3(nm, nm))
        yield
        p = p + _dot3(_split(p), nm)
        yield
    for size in range(base, GDN_CHUNK.bit_length() - 1):
        off = jnp.where(same_block(size + 1) & jnp.logical_not(same_block(size)), m, 0.0)
        p16 = p.astype(BF16)
        pc = _dot(p16, off.astype(BF16)).astype(BF16)
        yield
        p = p - _dot(pc, p16)
        yield
    return p


def _gdn_super_chunk(q, k, v, gcol, grow, bcol, glcol, st, rev):
    r = GDN_SUPER
    c = GDN_CHUNK
    ri = lax.broadcasted_iota(jnp.int32, (r, r), 0)
    ci = lax.broadcasted_iota(jnp.int32, (r, r), 1)
    same = (ri >> 6) == (ci >> 6)
    if rev:
        incl = same & (ri <= ci)
        strict = same & (ri < ci)
    else:
        incl = same & (ri >= ci)
        strict = same & (ri > ci)
    decay = jnp.exp(jnp.where(incl, gcol - grow, -jnp.inf))
    kb = k * bcol
    vb = v * bcol
    k16 = k.astype(BF16)
    m = jnp.where(strict, _dot_nt(kb.astype(BF16), k16) * decay, 0.0)
    qk = jnp.where(incl, _dot_nt(q.astype(BF16), k16) * decay, 0.0).astype(BF16)
    yield
    t = yield from _tri_inverse(m, ri, ci)
    eg = jnp.exp(gcol)
    rhs = jnp.concatenate([vb, kb * eg], axis=1).astype(BF16)
    uw = _dot(t.astype(BF16), rhs)
    u = uw[:, :HEAD_DIM]
    w = uw[:, HEAD_DIM:].astype(BF16)
    qd = (q * eg).astype(BF16)
    kt = (k * jnp.exp(glcol - gcol)).astype(BF16)
    yield
    outs = [None] * (r // c)
    order = range(r // c - 1, -1, -1) if rev else range(r // c)
    for i in order:
        sl = slice(i * c, (i + 1) * c)
        s16 = st.astype(BF16)
        vn = u[sl] - _dot(w[sl], s16)
        vn16 = vn.astype(BF16)
        yield
        pair = slice((i // 2) * 2 * c, (i // 2 + 1) * 2 * c)
        zero = jnp.zeros_like(vn16)
        vpad = jnp.concatenate([zero, vn16] if i % 2 else [vn16, zero], axis=0)
        outs[i] = _dot(qd[sl], s16) + _dot(qk[sl, pair], vpad)
        st = st * jnp.exp(glcol[i * c:i * c + 1, :]) + _dot_tn(kt[sl], vn16)
        yield
    return jnp.concatenate(outs, axis=0), st


def _gdn_core_body(q_ref, k_ref, v_ref, z_ref, row_ref, gn_ref, o_ref, col_s, of_s, ob_s):
    s_len = q_ref.shape[0]
    hd = HEAD_DIM
    heads = q_ref.shape[1] // hd
    r = GDN_SUPER
    n = s_len // r

    for hh in range(heads):
        rows8 = jnp.concatenate([row_ref[hh, i] for i in range(n)], axis=1)
        padded = jnp.concatenate([rows8, jnp.zeros((LANES - 8, s_len), F32)], axis=0)
        col_s[hh] = padded.T

    def one(hh, i, st, rev):
        base = 3 if rev else 0
        rows = pl.ds(pl.multiple_of(i * r, r), r)
        lanes = slice(hh * hd, (hh + 1) * hd)
        cols = col_s[hh, rows, :]
        gcol = cols[:, base:base + 1]
        bcol = cols[:, base + 1:base + 2]
        glcol = cols[:, base + 2:base + 3]
        grow = row_ref[hh, i][base:base + 1, :]
        return _gdn_super_chunk(q_ref[rows, lanes], k_ref[rows, lanes], v_ref[rows, lanes],
                                gcol, grow, bcol, glcol, st, rev)

    def step(t, carry):
        fwd_rows = pl.ds(pl.multiple_of(t * r, r), r)
        bwd_rows = pl.ds(pl.multiple_of((n - 1 - t) * r, r), r)
        gens = []
        for hh in range(heads):
            gens.append(one(hh, t, carry[2 * hh], False))
            gens.append(one(hh, n - 1 - t, carry[2 * hh + 1], True))
        res = _lockstep(gens)
        for hh in range(heads):
            lanes = slice(hh * hd, (hh + 1) * hd)
            of_s[fwd_rows, lanes] = res[2 * hh][0]
            ob_s[bwd_rows, lanes] = res[2 * hh + 1][0]
        return tuple(st for _, st in res)

    zero = jnp.zeros((hd, hd), F32)
    lax.fori_loop(0, n, step, (zero,) * (2 * heads))
    for hh in range(heads):
        lanes = slice(hh * hd, (hh + 1) * hd)
        o = of_s[:, lanes] + ob_s[:, lanes]
        y = o * lax.rsqrt(jnp.mean(o * o, axis=-1, keepdims=True) + EPS) * gn_ref[...]
        o_ref[:, lanes] = (y * jax.nn.silu(z_ref[:, lanes])).astype(o_ref.dtype)


def gdn_core(proj, gates, gdn_norm, b, s_len, *, heads):
    hd = HEAD_DIM
    n_super = s_len // GDN_SUPER
    gate_row = gates.reshape(b, GDN_HEADS, 8, n_super, GDN_SUPER).transpose(0, 1, 3, 2, 4)
    w = heads * hd
    ng = GDN_HEADS // heads
    first = 4 * RET_HEADS // heads
    qkv_blk = lambda off: pl.BlockSpec((None, s_len, w), lambda i, h: (i, 0, first + off + h))
    return pl.pallas_call(
        _gdn_core_body,
        out_shape=jax.ShapeDtypeStruct((b, s_len, GDN_HEADS * hd), BF16),
        grid=(b, ng),
        in_specs=[qkv_blk(0), qkv_blk(ng), qkv_blk(2 * ng),
                  pl.BlockSpec((None, s_len, w), lambda i, h: (i, 0, first + 3 * ng + h),
                               pipeline_mode=pl.Buffered(1)),
                  pl.BlockSpec((None, heads, n_super, 8, GDN_SUPER), lambda i, h: (i, h, 0, 0, 0)),
                  pl.BlockSpec((1, hd), lambda i, h: (0, 0))],
        out_specs=pl.BlockSpec((None, s_len, w), lambda i, h: (i, 0, h)),
        scratch_shapes=[pltpu.VMEM((heads, s_len, LANES), F32),
                        pltpu.VMEM((s_len, w), F32), pltpu.VMEM((s_len, w), F32)],
        compiler_params=_params(2),
        name="gdn_core",
    )(proj, proj, proj, proj, gate_row, gdn_norm.reshape(1, hd))


def _mixer_xattn_body(x_ref, a1_ref, a2_ref, w1_ref, w2_ref, gxa_ref, wq_ref, kv_ref, wo_ref,
                      gffn_ref, wpq_ref, o_ref, pq_ref):
    x = x_ref[...] + _dot(a1_ref[...], w1_ref[...]) + _dot(a2_ref[...], w2_ref[...])
    d = x.shape[1]
    dh = d // XA_HEADS
    q = _dot(_rms(x, gxa_ref[...]).astype(BF16), wq_ref[...])
    outs = []
    for h in range(XA_HEADS):
        kh = kv_ref[:, h * dh:(h + 1) * dh].astype(BF16)
        vh = kv_ref[:, d + h * dh:d + (h + 1) * dh].astype(BF16)
        sc = _dot_nt(q[:, h * dh:(h + 1) * dh].astype(BF16), kh) * (dh ** -0.5)
        p = jax.nn.softmax(sc, axis=-1)
        outs.append(_dot(p.astype(BF16), vh).astype(BF16))
    x = x + _dot(jnp.concatenate(outs, axis=1), wo_ref[...])
    o_ref[...] = x
    pq_ref[...] = _dot(_rms(x, gffn_ref[...]).astype(BF16), wpq_ref[...])


def mixer_xattn_peerq(x, a1, a2, w1, w2, g_xa, wq, kv, wo, g_ffn, wpq, b, s_len, *, ts):
    t, d = x.shape
    k1, k2 = a1.shape[1], a2.shape[1]
    m = kv.shape[1]
    npq = wpq.shape[1]
    nblk = s_len // ts
    tok = lambda n: pl.BlockSpec((ts, n), lambda i, j: (i * nblk + j, 0))
    whole = lambda r, c: pl.BlockSpec((r, c), lambda i, j: (0, 0))
    return pl.pallas_call(
        _mixer_xattn_body,
        out_shape=(jax.ShapeDtypeStruct((t, d), F32), jax.ShapeDtypeStruct((t, npq), F32)),
        grid=(b, nblk),
        in_specs=[tok(d), tok(k1), tok(k2), whole(k1, d), whole(k2, d),
                  whole(1, d), whole(d, d),
                  pl.BlockSpec((None, m, 2 * d), lambda i, j: (i, 0, 0)),
                  whole(d, d), whole(1, d), whole(d, npq)],
        out_specs=(tok(d), tok(npq)),
        compiler_params=_params(2),
        name="mixer_xattn_peerq",
    )(x, a1, a2, w1, w2, g_xa.reshape(1, d), wq, kv, wo, g_ffn.reshape(1, d), wpq)


def _extract_topk(s, k, break_ties):
    vals = []
    if break_ties:
        n_rows = s.shape[0]
        iota = lax.broadcasted_iota(jnp.int32, s.shape, 0).astype(F32)
        rank = jnp.full(s.shape, float(k), F32)
        for r in range(k):
            m = jnp.max(s, axis=0, keepdims=True)
            idx = jnp.min(jnp.where(s == m, iota, float(n_rows)), axis=0, keepdims=True)
            hit = iota == idx
            rank = jnp.where(hit, float(r), rank)
            s = jnp.where(hit, -jnp.inf, s)
            vals.append(m)
            yield
        taken = rank < float(k)
    else:
        for r in range(k):
            m = jnp.max(s, axis=0, keepdims=True)
            s = jnp.where(s == m, -(TAKEN + r * TAKEN_STEP), s)
            vals.append(m)
            yield
        taken = s <= -TAKEN
        rank = jnp.where(taken, s * (-1.0 / TAKEN_STEP) - TAKEN / TAKEN_STEP, float(k))
    count = jnp.sum(jnp.where(taken, 1.0, 0.0), axis=0, keepdims=True)
    return rank, vals, count


def _route_tokens(q, keys_ref, break_ties):
    nk = PEER_NKEYS
    kk = PEER_TOPK
    s1 = _dot_nt(keys_ref[0], q[:, :nk])
    s2 = _dot_nt(keys_ref[1], q[:, nk:])
    (r1, v1, n_sel1), (r2, v2, n_sel2) = yield from _lockstep_stages(
        [_extract_topk(s1, kk, break_ties), _extract_topk(s2, kk, break_ties)])
    v1m = jnp.concatenate(v1, axis=0)
    v2m = jnp.concatenate(v2, axis=0)
    pad = -jnp.inf if break_ties else FAST_PAD
    row8 = lax.broadcasted_iota(jnp.int32, (8, LANES), 0)
    groups = [v1[0] + v2m[0:8], v1[0] + v2m[8:16]]
    for a in range(1, 8):
        groups.append(jnp.where(row8 < kk // (a + 1), v1[a] + v2m[0:8], pad))
    groups.append(v1m[8:16] + v2[0])
    cand = jnp.concatenate(groups, axis=0)
    rc, _, n_selc = yield from _extract_topk(cand, kk, break_ties)
    sel = rc < float(kk)
    top = v1[0] + v2[0]
    z = jnp.sum(jnp.where(sel, jnp.exp(cand - top), 0.0), axis=0, keepdims=True)
    sel_f = jnp.where(sel, 1.0, 0.0)
    n1 = jnp.zeros((nk, LANES), F32)
    for a in range(kk):
        if a == 0:
            cnt = jnp.sum(sel_f[0:16], axis=0, keepdims=True)
        elif a < 8:
            cnt = jnp.sum(sel_f[8 + 8 * a:16 + 8 * a], axis=0, keepdims=True)
        else:
            cnt = sel_f[64 + a:65 + a]
        n1 = jnp.where(r1 == float(a), cnt, n1)
    c1 = jnp.where(r1 < float(kk), jnp.exp(s1 - v1[0]) / z, 0.0)
    e2 = jnp.exp(s2 - v2[0])
    exact = ((n_sel1 == float(kk)) & (n_sel2 == float(kk)) & (n_selc == float(kk))
             & (v1[kk - 1] > -FAST_SCORE_BOUND) & (v2[kk - 1] > -FAST_SCORE_BOUND))
    return c1, n1, e2, r2, jnp.where(exact, 1.0, 0.0)


def _peer_route_body(q_ref, keys_ref, c1_ref, n1_ref, e2_ref, r2_ref):
    tb = q_ref.shape[0]

    def run(break_ties):
        groups = [slice(blk * LANES, (blk + 1) * LANES) for blk in range(tb // LANES)]
        results = _lockstep([_route_tokens(q_ref[tok, :].astype(BF16), keys_ref, break_ties)
                             for tok in groups])
        flags = []
        for tok, (c1, n1, e2, r2, ok) in zip(groups, results):
            c1_ref[:, tok] = c1
            n1_ref[:, tok] = n1
            e2_ref[:, tok] = e2.astype(e2_ref.dtype)
            r2_ref[:, tok] = r2.astype(r2_ref.dtype)
            flags.append(ok)
        return jnp.min(jnp.concatenate(flags, axis=1))

    all_exact = run(False)

    @pl.when(all_exact < 0.5)
    def _():
        run(True)


def peer_route(q, keys, *, tb):
    t = q.shape[0]
    nk = PEER_NKEYS
    out = lambda dt: jax.ShapeDtypeStruct((PEER_HEADS, nk, t), dt)
    ospec = pl.BlockSpec((None, nk, tb), lambda i, h: (h, 0, i))
    return pl.pallas_call(
        _peer_route_body,
        out_shape=(out(F32), out(F32), out(BF16), out(BF16)),
        grid=(t // tb, PEER_HEADS),
        in_specs=[pl.BlockSpec((tb, 2 * nk), lambda i, h: (i, h)),
                  pl.BlockSpec((None, 2, nk, nk), lambda i, h: (h, 0, 0, 0))],
        out_specs=(ospec,) * 4,
        compiler_params=_params(2),
        name="peer_route",
    )(q, keys)


def _gelu_tanh(x):
    c0 = 0.7978845608028654
    c1 = 0.7978845608028654 * 0.044715
    return (0.5 * x) * (1.0 + jnp.tanh(x * (c0 + c1 * (x * x))))


def _peer_dense_body(x_ref, g_ref, c1_ref, n1_ref, e2_ref, r2_ref, u_ref, vt_ref, *rest, final):
    if final:
        gf_ref, o_ref, xn_s, acc_s = rest
    else:
        o_ref, xn_s, acc_s = rest
    j = pl.program_id(1)
    nk = PEER_NKEYS
    tb = x_ref.shape[0]
    grp = u_ref.shape[0] // nk
    sub = e2_ref.shape[2]

    @pl.when(j == 0)
    def _():
        xn_s[...] = _rms(x_ref[...], g_ref[...]).T.astype(BF16)
        acc_s[...] = jnp.zeros_like(acc_s)

    per = DENSE_CHUNK // nk
    n_chunks = grp // per
    d_model = vt_ref.shape[0]
    xn = xn_s[...]
    scores, gates, weights, chunk_w = {}, {}, {}, {}

    def score_pieces(c):
        def piece(ii):
            scores[ii] = _dot(u_ref[ii * nk:(ii + 1) * nk, :], xn)
        return [functools.partial(piece, c * per + i) for i in range(per)]

    def gate_pieces(c):
        def piece(ii, h):
            row = pl.ds(j * grp + ii, 1)
            c1 = jnp.broadcast_to(c1_ref[h, row, :], (sub, tb)).astype(e2_ref.dtype)[None]
            n1 = jnp.broadcast_to(n1_ref[h, row, :], (sub, tb)).astype(e2_ref.dtype)[None]
            e2 = e2_ref[h]
            term = jnp.where(r2_ref[h] < n1, e2, jnp.zeros_like(e2)) * c1
            gates[ii] = term if h == 0 else gates[ii] + term
        return [functools.partial(piece, c * per + i, h) for i in range(per) for h in range(PEER_HEADS)]

    def act_pieces(c):
        def piece(ii):
            act = _gelu_tanh(scores.pop(ii).astype(e2_ref.dtype))
            weights[ii] = act * gates.pop(ii).reshape(nk, tb)
        return [functools.partial(piece, c * per + i) for i in range(per)]

    def value_pieces(c):
        ex = slice(c * DENSE_CHUNK, (c + 1) * DENSE_CHUNK)

        def piece(m):
            if m == 0:
                chunk_w[c] = jnp.concatenate([weights.pop(c * per + i) for i in range(per)], axis=0)
            rows = slice(m * MXU_DEPTH, (m + 1) * MXU_DEPTH)
            acc_s[rows, :] += _dot(vt_ref[rows, ex], chunk_w[c])
        return [functools.partial(piece, m) for m in range(d_model // MXU_DEPTH)]

    _merge_run(score_pieces(0), gate_pieces(0))
    for c in range(n_chunks):
        mxu = (score_pieces(c + 1) if c + 1 < n_chunks else []) + (value_pieces(c - 1) if c else [])
        vpu = act_pieces(c) + (gate_pieces(c + 1) if c + 1 < n_chunks else [])
        _merge_run(mxu, vpu)
    _merge_run(value_pieces(n_chunks - 1), [])

    @pl.when(j == pl.num_programs(1) - 1)
    def _():
        y = x_ref[...] + acc_s[...].T
        if final:
            y = _rms(y, gf_ref[...])
        o_ref[...] = y


def peer_dense_residual(x, g, route, u, layer, vt, g_final, *, tb, grp, blocks=None):
    t, d = x.shape
    nk = PEER_NKEYS
    eb = grp * nk
    first, count = blocks if blocks is not None else (0, t // tb)
    c1, n1, e2, r2 = route
    sub = 32 // jnp.dtype(e2.dtype).itemsize
    e2 = e2.reshape(PEER_HEADS, nk // sub, sub, t)
    r2 = r2.reshape(PEER_HEADS, nk // sub, sub, t)
    rspec = pl.BlockSpec((PEER_HEADS, nk, tb), lambda i, j: (0, 0, first + i))
    pspec = pl.BlockSpec((PEER_HEADS, nk // sub, sub, tb), lambda i, j: (0, 0, 0, first + i))
    in_specs = [pl.BlockSpec((tb, d), lambda i, j: (first + i, 0)),
                pl.BlockSpec((1, d), lambda i, j: (0, 0)),
                rspec, rspec, pspec, pspec,
                pl.BlockSpec((None, eb, d), lambda i, j: (layer, j, 0)),
                pl.BlockSpec((d, eb), lambda i, j: (0, j))]
    args = [x, g.reshape(1, d), c1, n1, e2, r2, u, vt]
    final = g_final is not None
    if final:
        in_specs.append(pl.BlockSpec((1, d), lambda i, j: (0, 0)))
        args.append(g_final.reshape(1, d))
    return pl.pallas_call(
        functools.partial(_peer_dense_body, final=final),
        out_shape=jax.ShapeDtypeStruct((count * tb, d), F32),
        grid=(count, (nk * nk) // eb),
        in_specs=in_specs,
        out_specs=pl.BlockSpec((tb, d), lambda i, j: (i, 0)),
        scratch_shapes=[pltpu.VMEM((d, tb), BF16),
                        pltpu.VMEM((d, tb), F32)],
        compiler_params=_params(2),
        name="peer_dense_final" if final else "peer_dense",
    )(*args)


def _transpose_cast_body(x_ref, o_ref):
    o_ref[...] = x_ref[...].T.astype(o_ref.dtype)


def transpose_cast(tables, layer):
    _, rows, cols = tables.shape
    tr = _pick(rows, (1024, 512, 256, 128))
    return pl.pallas_call(
        _transpose_cast_body,
        out_shape=jax.ShapeDtypeStruct((cols, rows), BF16),
        grid=(rows // tr,),
        in_specs=[pl.BlockSpec((None, tr, cols), lambda i: (layer, i, 0))],
        out_specs=pl.BlockSpec((cols, tr), lambda i: (0, i)),
        compiler_params=_params(1),
        name="transpose_cast",
    )(tables)


def _pick(n, prefs):
    for p in prefs:
        if n % p == 0:
            return p
    return n


def _encoder(group_batches, x3, mem3, norm_mix, w_in, ret_norm, gdn_conv, gdn_a_log, gdn_dt_bias,
             gdn_norm, w_out, norm_xa, norm_mem, w_xq, w_xkv, w_xo, norm_ffn, peer_wq, peer_keys,
             peer_u, peer_v, norm_final):
    b, s_len, d = x3.shape
    n_mem = mem3.shape[1]
    t = b * s_len
    depth = w_in.shape[0]
    x = x3.reshape(t, d)
    mem = mem3.reshape(b * n_mem, d)
    n_main = 4 * RET_HEADS * HEAD_DIM + 4 * GDN_HEADS * HEAD_DIM
    tmm = _pick(b * n_mem, (1024, 512, 256, 128))
    ts = _pick(s_len, (512, 256, 128))
    tb_route = _pick(t, (512, 256, 128))
    tb_dense = _pick(s_len, (512, 256, 128))
    peer_u16 = peer_u.astype(BF16)

    for l in range(depth):
        w_main = _deinterleave_rotary_columns(w_in[l, :, :n_main]).astype(BF16)
        w_gate_t = w_in[l, :, n_main:].T.astype(BF16)
        proj, graw = mixer_in_proj(x, norm_mix[l], w_main, w_gate_t, gdn_conv[l], s_len)
        proj3 = proj.reshape(b, s_len, n_main)
        o_r = retention(proj3, ret_norm[l], b, s_len, heads=RET_HEADS_PER_STEP)
        gates = gdn_gates(graw, gdn_a_log[l], gdn_dt_bias[l], b, s_len)
        o_g = gdn_core(proj3, gates, gdn_norm[l], b, s_len, heads=GDN_HEADS_PER_STEP)
        w_o = w_out[l].astype(BF16)
        n_r = RET_HEADS * HEAD_DIM

        kv = norm_matmul(mem, norm_mem[l], w_xkv[l].astype(BF16), tm=tmm, tn=1024)
        x, pq = mixer_xattn_peerq(x, o_r.reshape(t, -1), o_g.reshape(t, -1), w_o[:n_r], w_o[n_r:],
                                  norm_xa[l], w_xq[l].astype(BF16), kv.reshape(b, n_mem, 2 * d),
                                  w_xo[l].astype(BF16), norm_ffn[l], peer_wq[l].astype(BF16),
                                  b, s_len, ts=ts)

        route = peer_route(pq, peer_keys[l].astype(BF16), tb=tb_route)
        dense = functools.partial(peer_dense_residual, x, norm_ffn[l], route, peer_u16, l,
                                  transpose_cast(peer_v, l), tb=tb_dense, grp=16)
        if l < depth - 1:
            x = dense(None)
    outs, first = [], 0
    for nb in group_batches:
        n_blocks = nb * s_len // tb_dense
        outs.append(dense(norm_final, blocks=(first, n_blocks)).reshape(nb, s_len, d))
        first += n_blocks
    return tuple(outs)


def kernel(x_prompt, x_sample, mem_prompt, mem_sample, norm_mix, w_in, ret_norm, gdn_conv, gdn_a_log,
           gdn_dt_bias, gdn_norm, w_out, norm_xa, norm_mem, w_xq, w_xkv, w_xo, norm_ffn, peer_wq,
           peer_keys, peer_u, peer_v, norm_final):
    weights = (norm_mix, w_in, ret_norm, gdn_conv, gdn_a_log, gdn_dt_bias, gdn_norm, w_out,
               norm_xa, norm_mem, w_xq, w_xkv, w_xo, norm_ffn, peer_wq, peer_keys, peer_u, peer_v,
               norm_final)
    if x_prompt.shape[1:] == x_sample.shape[1:] and mem_prompt.shape[1:] == mem_sample.shape[1:]:
        return _encoder((x_prompt.shape[0], x_sample.shape[0]),
                        jnp.concatenate([x_prompt, x_sample], axis=0),
                        jnp.concatenate([mem_prompt, mem_sample], axis=0), *weights)
    return (_encoder((x_prompt.shape[0],), x_prompt, mem_prompt, *weights)[0],
            _encoder((x_sample.shape[0],), x_sample, mem_sample, *weights)[0])
```

```python
import functools

import jax
import jax.numpy as jnp
from jax import lax
from jax.experimental import pallas as pl
from jax.experimental.pallas import tpu as pltpu

F32 = jnp.float32
BF16 = jnp.bfloat16

EPS = 1e-6
LANES = 128
MXU_DEPTH = 256
DENSE_CHUNK = 512
HEAD_DIM = 128
RET_HEADS = 4
GDN_HEADS = 4
RET_CHUNK = 128
GDN_CHUNK = 64
GDN_SUPER = 256
GDN_INV_BASE_LOG2 = 1
GDN_HEADS_PER_STEP = 4
RET_HEADS_PER_STEP = 4
ROPE_BASE = 10000.0
XA_HEADS = 4
PEER_HEADS = 8
PEER_NKEYS = 128
PEER_TOPK = 16
TAKEN = 2.0 ** 100
TAKEN_STEP = 2.0 ** 96
FAST_PAD = -(2.0 ** 90)
FAST_SCORE_BOUND = 2.0 ** 80
VMEM_LIMIT = 56 * 1024 * 1024

_NT = (((1,), (1,)), ((), ()))
_TN = (((0,), (0,)), ((), ()))


def _params(n_axes):
    return pltpu.CompilerParams(
        dimension_semantics=("arbitrary",) * n_axes, vmem_limit_bytes=VMEM_LIMIT)


def _dot(a, b):
    return jnp.dot(a, b, preferred_element_type=F32)


def _dot_nt(a, b):
    return lax.dot_general(a, b, _NT, preferred_element_type=F32)


def _dot_tn(a, b):
    return lax.dot_general(a, b, _TN, preferred_element_type=F32)


def _split(a):
    hi = a.astype(BF16)
    return hi, (a - hi.astype(F32)).astype(BF16)


def _dot3(a, b):
    (ah, al), (bh, bl) = a, b
    return _dot(jnp.concatenate([ah, ah, al], axis=1), jnp.concatenate([bh, bl, bh], axis=0))


def _rms(x, g):
    return x * lax.rsqrt(jnp.mean(x * x, axis=-1, keepdims=True) + EPS) * g


def _norm_matmul_body(x_ref, g_ref, w_ref, o_ref, xn_ref):
    @pl.when(pl.program_id(1) == 0)
    def _():
        xn_ref[...] = _rms(x_ref[...], g_ref[...]).astype(BF16)

    o_ref[...] = _dot(xn_ref[...], w_ref[...]).astype(o_ref.dtype)


def norm_matmul(x, g, w, *, tm, tn, out_dtype=F32):
    t, d = x.shape
    n = w.shape[1]
    return pl.pallas_call(
        _norm_matmul_body,
        out_shape=jax.ShapeDtypeStruct((t, n), out_dtype),
        grid=(t // tm, n // tn),
        in_specs=[pl.BlockSpec((tm, d), lambda i, j: (i, 0)),
                  pl.BlockSpec((1, d), lambda i, j: (0, 0)),
                  pl.BlockSpec((d, tn), lambda i, j: (0, j))],
        out_specs=pl.BlockSpec((tm, tn), lambda i, j: (i, j)),
        scratch_shapes=[pltpu.VMEM((tm, d), BF16)],
        compiler_params=_params(2),
        name="norm_matmul",
    )(x, g.reshape(1, d), w)


def _mixer_in_proj_body(x_ref, g_ref, w_ref, ws_ref, cos_ref, sin_ref, cw_ref, o_ref, os_ref, xn_ref):
    j = pl.program_id(1)
    hd = HEAD_DIM
    s_len = x_ref.shape[0]

    @pl.when(j == 0)
    def _():
        xn = _rms(x_ref[...], g_ref[...]).astype(BF16)
        xn_ref[...] = xn
        os_ref[...] = _dot_nt(ws_ref[...], xn)

    o_ref[...] = _dot(xn_ref[...], w_ref[...])

    @pl.when(j == 0)
    def _():
        cos = cos_ref[...]
        sin = sin_ref[...]
        for head in range(2 * RET_HEADS):
            cols = slice(head * hd, (head + 1) * hd)
            x = o_ref[:, cols]
            y = x * cos + pltpu.roll(x, hd // 2, 1) * sin
            o_ref[:, cols] = y * (hd ** -0.5) if head >= RET_HEADS else y

    def conv_silu(cols):
        x = o_ref[:, cols]
        t = lax.broadcasted_iota(jnp.int32, x.shape, 0)
        prev = jnp.where(t == 0, 0.0, pltpu.roll(x, 1, 0))
        nxt = jnp.where(t == s_len - 1, 0.0, pltpu.roll(x, s_len - 1, 0))
        return jax.nn.silu(prev * cw_ref[0:1, cols] + x * cw_ref[1:2, cols] + nxt * cw_ref[2:3, cols])

    @pl.when(j == 2)
    def _():
        for head in range(2 * GDN_HEADS):
            cols = slice(head * hd, (head + 1) * hd)
            act = conv_silu(cols)
            inv = lax.rsqrt(jnp.sum(act * act, axis=-1, keepdims=True) + EPS)
            o_ref[:, cols] = act * (inv * (hd ** -0.5) if head < GDN_HEADS else inv)

    @pl.when(j == 3)
    def _():
        for head in range(GDN_HEADS):
            cols = slice(head * hd, (head + 1) * hd)
            o_ref[:, cols] = conv_silu(cols)


def mixer_in_proj(x, g, w, ws_t, conv_w, s_len):
    t, d = x.shape
    n = w.shape[1]
    rows = ws_t.shape[0]
    tn = 2 * RET_HEADS * HEAD_DIM
    assert tn == 2 * GDN_HEADS * HEAD_DIM and n == 4 * tn
    cos, sin = _rotary_tables(s_len)
    conv_w = jnp.pad(conv_w, ((0, 0), (0, 2 * tn - conv_w.shape[1])))
    return pl.pallas_call(
        _mixer_in_proj_body,
        out_shape=(jax.ShapeDtypeStruct((t, n), F32),
                   jax.ShapeDtypeStruct((rows, t), F32)),
        grid=(t // s_len, n // tn),
        in_specs=[pl.BlockSpec((s_len, d), lambda i, j: (i, 0)),
                  pl.BlockSpec((1, d), lambda i, j: (0, 0)),
                  pl.BlockSpec((d, tn), lambda i, j: (0, j)),
                  pl.BlockSpec((rows, d), lambda i, j: (0, 0)),
                  pl.BlockSpec((s_len, HEAD_DIM), lambda i, j: (0, 0)),
                  pl.BlockSpec((s_len, HEAD_DIM), lambda i, j: (0, 0)),
                  pl.BlockSpec((3, tn), lambda i, j: (0, jnp.maximum(j - 2, 0)))],
        out_specs=(pl.BlockSpec((s_len, tn), lambda i, j: (i, j)),
                   pl.BlockSpec((rows, s_len), lambda i, j: (0, i))),
        scratch_shapes=[pltpu.VMEM((s_len, d), BF16)],
        compiler_params=_params(2),
        name="mixer_in_proj",
    )(x, g.reshape(1, d), w, ws_t, cos, sin, conv_w)


def _retention_body(qr_s, kr_s, v_ref, g_ref, dec_ref, rn_ref, o_ref, of_s, ob_s):
    s_len = qr_s.shape[0]
    hd = HEAD_DIM
    heads = qr_s.shape[1] // hd
    c = RET_CHUNK
    n = s_len // c

    def forward(hh, i, st):
        r = pl.ds(pl.multiple_of(i * c, c), c)
        lanes = slice(hh * hd, (hh + 1) * hd)
        qc, kc, vb = qr_s[r, lanes], kr_s[r, lanes], v_ref[r, lanes].astype(BF16)
        sc = _dot_nt(qc.astype(BF16), kc.astype(BF16)) * dec_ref[hh, 0]
        inter = _dot((qc * dec_ref[hh, 1]).astype(BF16), st.astype(BF16))
        st = st * dec_ref[hh, 5] + _dot_tn((kc * dec_ref[hh, 2]).astype(BF16), vb)
        yield
        return inter + _dot(sc.astype(BF16), vb), st

    def backward(hh, i, st):
        r = pl.ds(pl.multiple_of(i * c, c), c)
        lanes = slice(hh * hd, (hh + 1) * hd)
        qc, kc, vb = qr_s[r, lanes], kr_s[r, lanes], v_ref[r, lanes].astype(BF16)
        inter = _dot((qc * dec_ref[hh, 3]).astype(BF16), st.astype(BF16))
        st = st * dec_ref[hh, 6] + _dot_tn((kc * dec_ref[hh, 4]).astype(BF16), vb)
        yield
        return inter, st

    def step(t, carry):
        fwd_rows = pl.ds(pl.multiple_of(t * c, c), c)
        bwd_rows = pl.ds(pl.multiple_of((n - 1 - t) * c, c), c)
        gens = []
        for hh in range(heads):
            gens.append(forward(hh, t, carry[2 * hh]))
            gens.append(backward(hh, n - 1 - t, carry[2 * hh + 1]))
        res = _lockstep(gens)
        for hh in range(heads):
            lanes = slice(hh * hd, (hh + 1) * hd)
            of_s[fwd_rows, lanes] = res[2 * hh][0]
            ob_s[bwd_rows, lanes] = res[2 * hh + 1][0]
        return tuple(st for _, st in res)

    zero = jnp.zeros((hd, hd), F32)
    lax.fori_loop(0, n, step, (zero,) * (2 * heads))

    for hh in range(heads):
        lanes = slice(hh * hd, (hh + 1) * hd)
        o = of_s[:, lanes] + ob_s[:, lanes]
        mu = jnp.mean(o, axis=-1, keepdims=True)
        var = jnp.mean(jnp.square(o - mu), axis=-1, keepdims=True)
        y = (o - mu) * lax.rsqrt(var + EPS)
        y = y * rn_ref[:, lanes] * jax.nn.silu(g_ref[:, lanes])
        o_ref[:, lanes] = y.astype(o_ref.dtype)


def _deinterleave_rotary_columns(w):
    n_qk = 2 * RET_HEADS * HEAD_DIM
    qk = w[:, :n_qk].reshape(w.shape[0], 2 * RET_HEADS, HEAD_DIM // 2, 2)
    qk = jnp.swapaxes(qk, 2, 3).reshape(w.shape[0], n_qk)
    return jnp.concatenate([qk, w[:, n_qk:]], axis=1)


def _rotary_tables(s_len):
    d = HEAD_DIM
    inv = ROPE_BASE ** (-jnp.arange(0, d, 2, dtype=F32) / d)
    ang = jnp.arange(s_len, dtype=F32)[:, None] * inv[None, :]
    cos = jnp.concatenate([jnp.cos(ang), jnp.cos(ang)], axis=1)
    sin = jnp.concatenate([-jnp.sin(ang), jnp.sin(ang)], axis=1)
    return cos, sin


def _retention_tables():
    d = HEAD_DIM
    h = jnp.arange(RET_HEADS, dtype=F32)
    lg_f = jnp.log1p(-jnp.exp2(-5.0 - h))[:, None, None]
    lg_b = jnp.log1p(-jnp.exp2(-5.5 - h))[:, None, None]
    c = RET_CHUNK
    pos = jnp.arange(c, dtype=F32)
    diff = (pos[:, None] - pos[None, :])[None]
    dmat = jnp.where(diff >= 0, jnp.exp(lg_f * jnp.where(diff >= 0, diff, 0.0)),
                     jnp.exp(lg_b * jnp.where(diff < 0, -diff, 0.0)))
    col = lambda v: jnp.broadcast_to(v, (RET_HEADS, c, d))
    p = pos[None, :, None]
    dec = jnp.stack([
        dmat,
        col(jnp.exp(lg_f * (p + 1.0))),
        col(jnp.exp(lg_f * (c - 1.0 - p))),
        col(jnp.exp(lg_b * (c - p))),
        col(jnp.exp(lg_b * p)),
        col(jnp.exp(lg_f * c)),
        col(jnp.exp(lg_b * c)),
    ], axis=1)
    return dec


def retention(proj, ret_norm, b, s_len, *, heads):
    dec = _retention_tables()
    hd = HEAD_DIM
    w = heads * hd
    ng = RET_HEADS // heads
    head_blk = lambda off: pl.BlockSpec((None, s_len, w), lambda i, h: (i, 0, off + h))
    return pl.pallas_call(
        _retention_body,
        out_shape=jax.ShapeDtypeStruct((b, s_len, RET_HEADS * hd), BF16),
        grid=(b, ng),
        in_specs=[head_blk(0), head_blk(ng), head_blk(2 * ng), head_blk(3 * ng),
                  pl.BlockSpec((heads, 7, RET_CHUNK, hd), lambda i, h: (h, 0, 0, 0)),
                  pl.BlockSpec((1, w), lambda i, h: (0, h))],
        out_specs=pl.BlockSpec((None, s_len, w), lambda i, h: (i, 0, h)),
        scratch_shapes=[pltpu.VMEM((s_len, w), F32)] * 2,
        compiler_params=_params(2),
        name="retention",
    )(proj, proj, proj, proj, dec, ret_norm.reshape(1, -1))


def _gdn_gates_body(raw_ref, al_ref, dt_ref, o_ref):
    s_len = raw_ref.shape[-1]
    nh = 2 * GDN_HEADS
    gb = raw_ref[0:nh, :]
    ga = raw_ref[nh:2 * nh, :]
    beta = jax.nn.sigmoid(gb)
    g = -jnp.exp(al_ref[...]) * jax.nn.softplus(ga + dt_ref[...])
    pos = lax.broadcasted_iota(jnp.int32, (nh, s_len), 1) & (GDN_CHUNK - 1)
    fwd = g
    rev = g
    k = 1
    while k < GDN_CHUNK:
        fwd = fwd + jnp.where(pos >= k, pltpu.roll(fwd, k, 1), 0.0)
        rev = rev + jnp.where(pos < GDN_CHUNK - k, pltpu.roll(rev, s_len - k, 1), 0.0)
        k *= 2
    total = fwd + rev - g
    for h in range(GDN_HEADS):
        for d, cum in enumerate((fwd, rev)):
            r = d * GDN_HEADS + h
            o_ref[h, 3 * d:3 * d + 1, :] = cum[r:r + 1, :]
            o_ref[h, 3 * d + 1:3 * d + 2, :] = beta[r:r + 1, :]
            o_ref[h, 3 * d + 2:3 * d + 3, :] = total[r:r + 1, :]
        o_ref[h, 6:8, :] = jnp.zeros((2, s_len), F32)


def gdn_gates(raw, a_log, dt_bias, b, s_len):
    nh = 2 * GDN_HEADS
    return pl.pallas_call(
        _gdn_gates_body,
        out_shape=jax.ShapeDtypeStruct((b, GDN_HEADS, 8, s_len), F32),
        grid=(b,),
        in_specs=[pl.BlockSpec((2 * nh, s_len), lambda i: (0, i)),
                  pl.BlockSpec((nh, 1), lambda i: (0, 0)),
                  pl.BlockSpec((nh, 1), lambda i: (0, 0))],
        out_specs=pl.BlockSpec((None, GDN_HEADS, 8, s_len), lambda i: (i, 0, 0, 0)),
        compiler_params=_params(1),
        name="gdn_gates",
    )(raw, a_log.reshape(nh, 1), dt_bias.reshape(nh, 1))


def _lockstep_stages(gens):
    results = [None] * len(gens)
    live = list(range(len(gens)))
    while live:
        for i in list(live):
            try:
                next(gens[i])
            except StopIteration as done:
                results[i] = done.value
                live.remove(i)
        yield
    return results


def _lockstep(gens):
    stages = _lockstep_stages(gens)
    while True:
        try:
            next(stages)
        except StopIteration as done:
            return done.value


def _merge_run(a, b):
    ia = ib = 0
    while ia < len(a) or ib < len(b):
        if ib >= len(b) or (ia < len(a) and (ia + 1) * len(b) <= (ib + 1) * len(a)):
            a[ia]()
            ia += 1
        else:
            b[ib]()
            ib += 1


def _tri_inverse(m, ri, ci):
    def same_block(log2_size):
        return (ri >> log2_size) == (ci >> log2_size)

    base = GDN_INV_BASE_LOG2
    eye = (ri == ci).astype(F32)
    diag = jnp.where(same_block(base), -m, 0.0)
    nm = _split(diag)
    p = eye + diag
    for _ in range(base - 1):
        nm = _split(_dot3(nm, nm))
        yield
        p = p + _dot3(_split(p), nm)
        yield
    for size in range(base, GDN_CHUNK.bit_length() - 1):
        off = jnp.where(same_block(size + 1) & jnp.logical_not(same_block(size)), m, 0.0)
        p16 = p.astype(BF16)
        pc = _dot(p16, off.astype(BF16)).astype(BF16)
        yield
        p = p - _dot(pc, p16)
        yield
    return p


def _gdn_super_chunk(q, k, v, gcol, grow, bcol, glcol, st, rev):
    r = GDN_SUPER
    c = GDN_CHUNK
    ri = lax.broadcasted_iota(jnp.int32, (r, r), 0)
    ci = lax.broadcasted_iota(jnp.int32, (r, r), 1)
    same = (ri >> 6) == (ci >> 6)
    if rev:
        incl = same & (ri <= ci)
        strict = same & (ri < ci)
    else:
        incl = same & (ri >= ci)
        strict = same & (ri > ci)
    decay = jnp.exp(jnp.where(incl, gcol - grow, -jnp.inf))
    kb = k * bcol
    vb = v * bcol
    k16 = k.astype(BF16)
    m = jnp.where(strict, _dot_nt(kb.astype(BF16), k16) * decay, 0.0)
    qk = jnp.where(incl, _dot_nt(q.astype(BF16), k16) * decay, 0.0).astype(BF16)
    yield
    t = yield from _tri_inverse(m, ri, ci)
    eg = jnp.exp(gcol)
    rhs = jnp.concatenate([vb, kb * eg], axis=1).astype(BF16)
    uw = _dot(t.astype(BF16), rhs)
    u = uw[:, :HEAD_DIM]
    w = uw[:, HEAD_DIM:].astype(BF16)
    qd = (q * eg).astype(BF16)
    kt = (k * jnp.exp(glcol - gcol)).astype(BF16)
    yield
    outs = [None] * (r // c)
    order = range(r // c - 1, -1, -1) if rev else range(r // c)
    for i in order:
        sl = slice(i * c, (i + 1) * c)
        s16 = st.astype(BF16)
        vn = u[sl] - _dot(w[sl], s16)
        vn16 = vn.astype(BF16)
        yield
        pair = slice((i // 2) * 2 * c, (i // 2 + 1) * 2 * c)
        zero = jnp.zeros_like(vn16)
        vpad = jnp.concatenate([zero, vn16] if i % 2 else [vn16, zero], axis=0)
        outs[i] = _dot(qd[sl], s16) + _dot(qk[sl, pair], vpad)
        st = st * jnp.exp(glcol[i * c:i * c + 1, :]) + _dot_tn(kt[sl], vn16)
        yield
    return jnp.concatenate(outs, axis=0), st


def _gdn_core_body(q_ref, k_ref, v_ref, z_ref, row_ref, gn_ref, o_ref, col_s, of_s, ob_s):
    s_len = q_ref.shape[0]
    hd = HEAD_DIM
    heads = q_ref.shape[1] // hd
    r = GDN_SUPER
    n = s_len // r

    for hh in range(heads):
        rows8 = jnp.concatenate([row_ref[hh, i] for i in range(n)], axis=1)
        padded = jnp.concatenate([rows8, jnp.zeros((LANES - 8, s_len), F32)], axis=0)
        col_s[hh] = padded.T

    def one(hh, i, st, rev):
        base = 3 if rev else 0
        rows = pl.ds(pl.multiple_of(i * r, r), r)
        lanes = slice(hh * hd, (hh + 1) * hd)
        cols = col_s[hh, rows, :]
        gcol = cols[:, base:base + 1]
        bcol = cols[:, base + 1:base + 2]
        glcol = cols[:, base + 2:base + 3]
        grow = row_ref[hh, i][base:base + 1, :]
        return _gdn_super_chunk(q_ref[rows, lanes], k_ref[rows, lanes], v_ref[rows, lanes],
                                gcol, grow, bcol, glcol, st, rev)

    def step(t, carry):
        fwd_rows = pl.ds(pl.multiple_of(t * r, r), r)
        bwd_rows = pl.ds(pl.multiple_of((n - 1 - t) * r, r), r)
        gens = []
        for hh in range(heads):
            gens.append(one(hh, t, carry[2 * hh], False))
            gens.append(one(hh, n - 1 - t, carry[2 * hh + 1], True))
        res = _lockstep(gens)
        for hh in range(heads):
            lanes = slice(hh * hd, (hh + 1) * hd)
            of_s[fwd_rows, lanes] = res[2 * hh][0]
            ob_s[bwd_rows, lanes] = res[2 * hh + 1][0]
        return tuple(st for _, st in res)

    zero = jnp.zeros((hd, hd), F32)
    lax.fori_loop(0, n, step, (zero,) * (2 * heads))
    for hh in range(heads):
        lanes = slice(hh * hd, (hh + 1) * hd)
        o = of_s[:, lanes] + ob_s[:, lanes]
        y = o * lax.rsqrt(jnp.mean(o * o, axis=-1, keepdims=True) + EPS) * gn_ref[...]
        o_ref[:, lanes] = (y * jax.nn.silu(z_ref[:, lanes])).astype(o_ref.dtype)


def gdn_core(proj, gates, gdn_norm, b, s_len, *, heads):
    hd = HEAD_DIM
    n_super = s_len // GDN_SUPER
    gate_row = gates.reshape(b, GDN_HEADS, 8, n_super, GDN_SUPER).transpose(0, 1, 3, 2, 4)
    w = heads * hd
    ng = GDN_HEADS // heads
    first = 4 * RET_HEADS // heads
    qkv_blk = lambda off: pl.BlockSpec((None, s_len, w), lambda i, h: (i, 0, first + off + h))
    return pl.pallas_call(
        _gdn_core_body,
        out_shape=jax.ShapeDtypeStruct((b, s_len, GDN_HEADS * hd), BF16),
        grid=(b, ng),
        in_specs=[qkv_blk(0), qkv_blk(ng), qkv_blk(2 * ng),
                  pl.BlockSpec((None, s_len, w), lambda i, h: (i, 0, first + 3 * ng + h),
                               pipeline_mode=pl.Buffered(1)),
                  pl.BlockSpec((None, heads, n_super, 8, GDN_SUPER), lambda i, h: (i, h, 0, 0, 0)),
                  pl.BlockSpec((1, hd), lambda i, h: (0, 0))],
        out_specs=pl.BlockSpec((None, s_len, w), lambda i, h: (i, 0, h)),
        scratch_shapes=[pltpu.VMEM((heads, s_len, LANES), F32),
                        pltpu.VMEM((s_len, w), F32), pltpu.VMEM((s_len, w), F32)],
        compiler_params=_params(2),
        name="gdn_core",
    )(proj, proj, proj, proj, gate_row, gdn_norm.reshape(1, hd))


def _mixer_xattn_body(x_ref, a1_ref, a2_ref, w1_ref, w2_ref, gxa_ref, wq_ref, kv_ref, wo_ref,
                      gffn_ref, wpq_ref, o_ref, pq_ref):
    x = x_ref[...] + _dot(a1_ref[...], w1_ref[...]) + _dot(a2_ref[...], w2_ref[...])
    d = x.shape[1]
    dh = d // XA_HEADS
    q = _dot(_rms(x, gxa_ref[...]).astype(BF16), wq_ref[...])
    outs = []
    for h in range(XA_HEADS):
        kh = kv_ref[:, h * dh:(h + 1) * dh].astype(BF16)
        vh = kv_ref[:, d + h * dh:d + (h + 1) * dh].astype(BF16)
        sc = _dot_nt(q[:, h * dh:(h + 1) * dh].astype(BF16), kh) * (dh ** -0.5)
        p = jax.nn.softmax(sc, axis=-1)
        outs.append(_dot(p.astype(BF16), vh).astype(BF16))
    x = x + _dot(jnp.concatenate(outs, axis=1), wo_ref[...])
    o_ref[...] = x
    pq_ref[...] = _dot(_rms(x, gffn_ref[...]).astype(BF16), wpq_ref[...])


def mixer_xattn_peerq(x, a1, a2, w1, w2, g_xa, wq, kv, wo, g_ffn, wpq, b, s_len, *, ts):
    t, d = x.shape
    k1, k2 = a1.shape[1], a2.shape[1]
    m = kv.shape[1]
    npq = wpq.shape[1]
    nblk = s_len // ts
    tok = lambda n: pl.BlockSpec((ts, n), lambda i, j: (i * nblk + j, 0))
    whole = lambda r, c: pl.BlockSpec((r, c), lambda i, j: (0, 0))
    return pl.pallas_call(
        _mixer_xattn_body,
        out_shape=(jax.ShapeDtypeStruct((t, d), F32), jax.ShapeDtypeStruct((t, npq), F32)),
        grid=(b, nblk),
        in_specs=[tok(d), tok(k1), tok(k2), whole(k1, d), whole(k2, d),
                  whole(1, d), whole(d, d),
                  pl.BlockSpec((None, m, 2 * d), lambda i, j: (i, 0, 0)),
                  whole(d, d), whole(1, d), whole(d, npq)],
        out_specs=(tok(d), tok(npq)),
        compiler_params=_params(2),
        name="mixer_xattn_peerq",
    )(x, a1, a2, w1, w2, g_xa.reshape(1, d), wq, kv, wo, g_ffn.reshape(1, d), wpq)


def _extract_topk(s, k, break_ties):
    vals = []
    if break_ties:
        n_rows = s.shape[0]
        iota = lax.broadcasted_iota(jnp.int32, s.shape, 0).astype(F32)
        rank = jnp.full(s.shape, float(k), F32)
        for r in range(k):
            m = jnp.max(s, axis=0, keepdims=True)
            idx = jnp.min(jnp.where(s == m, iota, float(n_rows)), axis=0, keepdims=True)
            hit = iota == idx
            rank = jnp.where(hit, float(r), rank)
            s = jnp.where(hit, -jnp.inf, s)
            vals.append(m)
            yield
        taken = rank < float(k)
    else:
        for r in range(k):
            m = jnp.max(s, axis=0, keepdims=True)
            s = jnp.where(s == m, -(TAKEN + r * TAKEN_STEP), s)
            vals.append(m)
            yield
        taken = s <= -TAKEN
        rank = jnp.where(taken, s * (-1.0 / TAKEN_STEP) - TAKEN / TAKEN_STEP, float(k))
    count = jnp.sum(jnp.where(taken, 1.0, 0.0), axis=0, keepdims=True)
    return rank, vals, count


def _route_tokens(q, keys_ref, break_ties):
    nk = PEER_NKEYS
    kk = PEER_TOPK
    s1 = _dot_nt(keys_ref[0], q[:, :nk])
    s2 = _dot_nt(keys_ref[1], q[:, nk:])
    (r1, v1, n_sel1), (r2, v2, n_sel2) = yield from _lockstep_stages(
        [_extract_topk(s1, kk, break_ties), _extract_topk(s2, kk, break_ties)])
    v1m = jnp.concatenate(v1, axis=0)
    v2m = jnp.concatenate(v2, axis=0)
    pad = -jnp.inf if break_ties else FAST_PAD
    row8 = lax.broadcasted_iota(jnp.int32, (8, LANES), 0)
    groups = [v1[0] + v2m[0:8], v1[0] + v2m[8:16]]
    for a in range(1, 8):
        groups.append(jnp.where(row8 < kk // (a + 1), v1[a] + v2m[0:8], pad))
    groups.append(v1m[8:16] + v2[0])
    cand = jnp.concatenate(groups, axis=0)
    rc, _, n_selc = yield from _extract_topk(cand, kk, break_ties)
    sel = rc < float(kk)
    top = v1[0] + v2[0]
    z = jnp.sum(jnp.where(sel, jnp.exp(cand - top), 0.0), axis=0, keepdims=True)
    sel_f = jnp.where(sel, 1.0, 0.0)
    n1 = jnp.zeros((nk, LANES), F32)
    for a in range(kk):
        if a == 0:
            cnt = jnp.sum(sel_f[0:16], axis=0, keepdims=True)
        elif a < 8:
            cnt = jnp.sum(sel_f[8 + 8 * a:16 + 8 * a], axis=0, keepdims=True)
        else:
            cnt = sel_f[64 + a:65 + a]
        n1 = jnp.where(r1 == float(a), cnt, n1)
    c1 = jnp.where(r1 < float(kk), jnp.exp(s1 - v1[0]) / z, 0.0)
    e2 = jnp.exp(s2 - v2[0])
    exact = ((n_sel1 == float(kk)) & (n_sel2 == float(kk)) & (n_selc == float(kk))
             & (v1[kk - 1] > -FAST_SCORE_BOUND) & (v2[kk - 1] > -FAST_SCORE_BOUND))
    return c1, n1, e2, r2, jnp.where(exact, 1.0, 0.0)


def _peer_route_body(q_ref, keys_ref, c1_ref, n1_ref, e2_ref, r2_ref):
    tb = q_ref.shape[0]

    def run(break_ties):
        groups = [slice(blk * LANES, (blk + 1) * LANES) for blk in range(tb // LANES)]
        results = _lockstep([_route_tokens(q_ref[tok, :].astype(BF16), keys_ref, break_ties)
                             for tok in groups])
        flags = []
        for tok, (c1, n1, e2, r2, ok) in zip(groups, results):
            c1_ref[:, tok] = c1
            n1_ref[:, tok] = n1
            e2_ref[:, tok] = e2.astype(e2_ref.dtype)
            r2_ref[:, tok] = r2.astype(r2_ref.dtype)
            flags.append(ok)
        return jnp.min(jnp.concatenate(flags, axis=1))

    all_exact = run(False)

    @pl.when(all_exact < 0.5)
    def _():
        run(True)


def peer_route(q, keys, *, tb):
    t = q.shape[0]
    nk = PEER_NKEYS
    out = lambda dt: jax.ShapeDtypeStruct((PEER_HEADS, nk, t), dt)
    ospec = pl.BlockSpec((None, nk, tb), lambda i, h: (h, 0, i))
    return pl.pallas_call(
        _peer_route_body,
        out_shape=(out(F32), out(F32), out(BF16), out(BF16)),
        grid=(t // tb, PEER_HEADS),
        in_specs=[pl.BlockSpec((tb, 2 * nk), lambda i, h: (i, h)),
                  pl.BlockSpec((None, 2, nk, nk), lambda i, h: (h, 0, 0, 0))],
        out_specs=(ospec,) * 4,
        compiler_params=_params(2),
        name="peer_route",
    )(q, keys)


def _gelu_tanh(x):
    c0 = 0.7978845608028654
    c1 = 0.7978845608028654 * 0.044715
    return (0.5 * x) * (1.0 + jnp.tanh(x * (c0 + c1 * (x * x))))


def _peer_dense_body(x_ref, g_ref, c1_ref, n1_ref, e2_ref, r2_ref, u_ref, vt_ref, *rest, final):
    if final:
        gf_ref, o_ref, xn_s, acc_s = rest
    else:
        o_ref, xn_s, acc_s = rest
    j = pl.program_id(1)
    nk = PEER_NKEYS
    tb = x_ref.shape[0]
    grp = u_ref.shape[0] // nk
    sub = e2_ref.shape[2]

    @pl.when(j == 0)
    def _():
        xn_s[...] = _rms(x_ref[...], g_ref[...]).T.astype(BF16)
        acc_s[...] = jnp.zeros_like(acc_s)

    per = DENSE_CHUNK // nk
    n_chunks = grp // per
    d_model = vt_ref.shape[0]
    xn = xn_s[...]
    scores, gates, weights, chunk_w = {}, {}, {}, {}

    def score_pieces(c):
        def piece(ii):
            scores[ii] = _dot(u_ref[ii * nk:(ii + 1) * nk, :], xn)
        return [functools.partial(piece, c * per + i) for i in range(per)]

    def gate_pieces(c):
        def piece(ii, h):
            row = pl.ds(j * grp + ii, 1)
            c1 = jnp.broadcast_to(c1_ref[h, row, :], (sub, tb)).astype(e2_ref.dtype)[None]
            n1 = jnp.broadcast_to(n1_ref[h, row, :], (sub, tb)).astype(e2_ref.dtype)[None]
            e2 = e2_ref[h]
            term = jnp.where(r2_ref[h] < n1, e2, jnp.zeros_like(e2)) * c1
            gates[ii] = term if h == 0 else gates[ii] + term
        return [functools.partial(piece, c * per + i, h) for i in range(per) for h in range(PEER_HEADS)]

    def act_pieces(c):
        def piece(ii):
            act = _gelu_tanh(scores.pop(ii).astype(e2_ref.dtype))
            weights[ii] = act * gates.pop(ii).reshape(nk, tb)
        return [functools.partial(piece, c * per + i) for i in range(per)]

    def value_pieces(c):
        ex = slice(c * DENSE_CHUNK, (c + 1) * DENSE_CHUNK)

        def piece(m):
            if m == 0:
                chunk_w[c] = jnp.concatenate([weights.pop(c * per + i) for i in range(per)], axis=0)
            rows = slice(m * MXU_DEPTH, (m + 1) * MXU_DEPTH)
            acc_s[rows, :] += _dot(vt_ref[rows, ex], chunk_w[c])
        return [functools.partial(piece, m) for m in range(d_model // MXU_DEPTH)]

    _merge_run(score_pieces(0), gate_pieces(0))
    for c in range(n_chunks):
        mxu = (score_pieces(c + 1) if c + 1 < n_chunks else []) + (value_pieces(c - 1) if c else [])
        vpu = act_pieces(c) + (gate_pieces(c + 1) if c + 1 < n_chunks else [])
        _merge_run(mxu, vpu)
    _merge_run(value_pieces(n_chunks - 1), [])

    @pl.when(j == pl.num_programs(1) - 1)
    def _():
        y = x_ref[...] + acc_s[...].T
        if final:
            y = _rms(y, gf_ref[...])
        o_ref[...] = y


def peer_dense_residual(x, g, route, u, layer, vt, g_final, *, tb, grp, blocks=None):
    t, d = x.shape
    nk = PEER_NKEYS
    eb = grp * nk
    first, count = blocks if blocks is not None else (0, t // tb)
    c1, n1, e2, r2 = route
    sub = 32 // jnp.dtype(e2.dtype).itemsize
    e2 = e2.reshape(PEER_HEADS, nk // sub, sub, t)
    r2 = r2.reshape(PEER_HEADS, nk // sub, sub, t)
    rspec = pl.BlockSpec((PEER_HEADS, nk, tb), lambda i, j: (0, 0, first + i))
    pspec = pl.BlockSpec((PEER_HEADS, nk // sub, sub, tb), lambda i, j: (0, 0, 0, first + i))
    in_specs = [pl.BlockSpec((tb, d), lambda i, j: (first + i, 0)),
                pl.BlockSpec((1, d), lambda i, j: (0, 0)),
                rspec, rspec, pspec, pspec,
                pl.BlockSpec((None, eb, d), lambda i, j: (layer, j, 0)),
                pl.BlockSpec((d, eb), lambda i, j: (0, j))]
    args = [x, g.reshape(1, d), c1, n1, e2, r2, u, vt]
    final = g_final is not None
    if final:
        in_specs.append(pl.BlockSpec((1, d), lambda i, j: (0, 0)))
        args.append(g_final.reshape(1, d))
    return pl.pallas_call(
        functools.partial(_peer_dense_body, final=final),
        out_shape=jax.ShapeDtypeStruct((count * tb, d), F32),
        grid=(count, (nk * nk) // eb),
        in_specs=in_specs,
        out_specs=pl.BlockSpec((tb, d), lambda i, j: (i, 0)),
        scratch_shapes=[pltpu.VMEM((d, tb), BF16),
                        pltpu.VMEM((d, tb), F32)],
        compiler_params=_params(2),
        name="peer_dense_final" if final else "peer_dense",
    )(*args)


def _transpose_cast_body(x_ref, o_ref):
    o_ref[...] = x_ref[...].T.astype(o_ref.dtype)


def transpose_cast(tables, layer):
    _, rows, cols = tables.shape
    tr = _pick(rows, (1024, 512, 256, 128))
    return pl.pallas_call(
        _transpose_cast_body,
        out_shape=jax.ShapeDtypeStruct((cols, rows), BF16),
        grid=(rows // tr,),
        in_specs=[pl.BlockSpec((None, tr, cols), lambda i: (layer, i, 0))],
        out_specs=pl.BlockSpec((cols, tr), lambda i: (0, i)),
        compiler_params=_params(1),
        name="transpose_cast",
    )(tables)


def _pick(n, prefs):
    for p in prefs:
        if n % p == 0:
            return p
    return n


def _encoder(group_batches, x3, mem3, norm_mix, w_in, ret_norm, gdn_conv, gdn_a_log, gdn_dt_bias,
             gdn_norm, w_out, norm_xa, norm_mem, w_xq, w_xkv, w_xo, norm_ffn, peer_wq, peer_keys,
             peer_u, peer_v, norm_final):
    b, s_len, d = x3.shape
    n_mem = mem3.shape[1]
    t = b * s_len
    depth = w_in.shape[0]
    x = x3.reshape(t, d)
    mem = mem3.reshape(b * n_mem, d)
    n_main = 4 * RET_HEADS * HEAD_DIM + 4 * GDN_HEADS * HEAD_DIM
    tmm = _pick(b * n_mem, (1024, 512, 256, 128))
    ts = _pick(s_len, (512, 256, 128))
    tb_route = _pick(t, (1024, 512, 256, 128))
    tb_dense = _pick(s_len, (512, 256, 128))
    peer_u16 = peer_u.astype(BF16)

    for l in range(depth):
        w_main = _deinterleave_rotary_columns(w_in[l, :, :n_main]).astype(BF16)
        w_gate_t = w_in[l, :, n_main:].T.astype(BF16)
        proj, graw = mixer_in_proj(x, norm_mix[l], w_main, w_gate_t, gdn_conv[l], s_len)
        proj3 = proj.reshape(b, s_len, n_main)
        o_r = retention(proj3, ret_norm[l], b, s_len, heads=RET_HEADS_PER_STEP)
        gates = gdn_gates(graw, gdn_a_log[l], gdn_dt_bias[l], b, s_len)
        o_g = gdn_core(proj3, gates, gdn_norm[l], b, s_len, heads=GDN_HEADS_PER_STEP)
        w_o = w_out[l].astype(BF16)
        n_r = RET_HEADS * HEAD_DIM

        kv = norm_matmul(mem, norm_mem[l], w_xkv[l].astype(BF16), tm=tmm, tn=1024)
        x, pq = mixer_xattn_peerq(x, o_r.reshape(t, -1), o_g.reshape(t, -1), w_o[:n_r], w_o[n_r:],
                                  norm_xa[l], w_xq[l].astype(BF16), kv.reshape(b, n_mem, 2 * d),
                                  w_xo[l].astype(BF16), norm_ffn[l], peer_wq[l].astype(BF16),
                                  b, s_len, ts=ts)

        route = peer_route(pq, peer_keys[l].astype(BF16), tb=tb_route)
        dense = functools.partial(peer_dense_residual, x, norm_ffn[l], route, peer_u16, l,
                                  transpose_cast(peer_v, l), tb=tb_dense, grp=16)
        if l < depth - 1:
            x = dense(None)
    outs, first = [], 0
    for nb in group_batches:
        n_blocks = nb * s_len // tb_dense
        outs.append(dense(norm_final, blocks=(first, n_blocks)).reshape(nb, s_len, d))
        first += n_blocks
    return tuple(outs)


def kernel(x_prompt, x_sample, mem_prompt, mem_sample, norm_mix, w_in, ret_norm, gdn_conv, gdn_a_log,
           gdn_dt_bias, gdn_norm, w_out, norm_xa, norm_mem, w_xq, w_xkv, w_xo, norm_ffn, peer_wq,
           peer_keys, peer_u, peer_v, norm_final):
    weights = (norm_mix, w_in, ret_norm, gdn_conv, gdn_a_log, gdn_dt_bias, gdn_norm, w_out,
               norm_xa, norm_mem, w_xq, w_xkv, w_xo, norm_ffn, peer_wq, peer_keys, peer_u, peer_v,
               norm_final)
    if x_prompt.shape[1:] == x_sample.shape[1:] and mem_prompt.shape[1:] == mem_sample.shape[1:]:
        return _encoder((x_prompt.shape[0], x_sample.shape[0]),
                        jnp.concatenate([x_prompt, x_sample], axis=0),
                        jnp.concatenate([mem_prompt, mem_sample], axis=0), *weights)
    return (_encoder((x_prompt.shape[0],), x_prompt, mem_prompt, *weights)[0],
            _encoder((x_sample.shape[0],), x_sample, mem_sample, *weights)[0])
```
